```python
import math, functools
import jax, jax.numpy as jnp
from jax import lax
import numpy as np

D_MODEL = 1024
BATCH = 4
SEQ = 4096
DEPTH = 1
DEC_BATCH = 128
DEC_SEQ = 8
PAST_LEN = 8192
PAGE_SIZE = 128

SSM_WIDTH = D_MODEL // 2
SSM_GROUP = 16
SSM_GROUPS = SSM_WIDTH // SSM_GROUP
SSM_STATE = 64
DT_MIN = 0.001
DT_MAX = 0.1
N_HEADS = 8
Q_LORA = 384
KV_LORA = 256
QK_NOPE = 64
QK_ROPE = 32
QK_HEAD = QK_NOPE + QK_ROPE
V_HEAD = 64
ROPE_THETA = 10000.0
ATTN_SCALE = QK_HEAD ** -0.5
Q_BLOCK = 128
D_FF = ((8 * D_MODEL // 3 + 255) // 256) * 256
PLE_DIM = 256
EPS = 1e-6
SPLIT_POINTS = (SSM_WIDTH,
                SSM_WIDTH + Q_LORA,
                SSM_WIDTH + Q_LORA + KV_LORA,
                SSM_WIDTH + Q_LORA + KV_LORA + QK_ROPE,
                SSM_WIDTH + Q_LORA + KV_LORA + QK_ROPE + D_MODEL)
D_IN = SPLIT_POINTS[-1] + D_MODEL

kernel_name = 'hybrid_s5_mla_decoder_step'


def rms_norm(x, g):
    xf = x.astype(jnp.float32)
    y = xf * lax.rsqrt(jnp.mean(xf * xf, axis=-1, keepdims=True) + EPS)
    return (y * g.astype(jnp.float32)).astype(x.dtype)


def qk_norm(x_nope, x_rope, g_nope, g_rope_half):
    nf = x_nope.astype(jnp.float32)
    rf = x_rope.astype(jnp.float32)
    ms = (jnp.sum(nf * nf, -1, keepdims=True) + jnp.sum(rf * rf, -1, keepdims=True)) / QK_HEAD
    r = lax.rsqrt(ms + EPS)
    g_rope = jnp.concatenate([g_rope_half, g_rope_half]).astype(jnp.float32)
    return ((nf * r * g_nope.astype(jnp.float32)).astype(x_nope.dtype),
            (rf * r * g_rope).astype(x_rope.dtype))


def rope_tables(pos):
    half = QK_ROPE // 2
    inv = ROPE_THETA ** (-jnp.arange(half, dtype=jnp.float32) / half)
    ang = pos.astype(jnp.float32)[:, None] * inv[None, :]
    return jnp.cos(ang), jnp.sin(ang)


def apply_rope(x, cos, sin):
    half = QK_ROPE // 2
    xf = x.astype(jnp.float32)
    x1, x2 = xf[..., :half], xf[..., half:]
    return jnp.concatenate([x1 * cos - x2 * sin, x2 * cos + x1 * sin], -1).astype(x.dtype)


def ssm_discretize(lam_re, lam_im, log_dt, b_re, b_im):
    lr, li = lam_re.astype(jnp.float32), lam_im.astype(jnp.float32)
    dt = jnp.exp(log_dt.astype(jnp.float32))[:, None]
    mag = jnp.exp(lr * dt)
    lb_re, lb_im = mag * jnp.cos(li * dt), mag * jnp.sin(li * dt)
    num_re, num_im = lb_re - 1.0, lb_im
    den = lr * lr + li * li
    f_re = (num_re * lr + num_im * li) / den
    f_im = (num_im * lr - num_re * li) / den
    br, bi = b_re.astype(jnp.float32), b_im.astype(jnp.float32)
    bb_re = f_re[..., None] * br - f_im[..., None] * bi
    bb_im = f_re[..., None] * bi + f_im[..., None] * br
    return lb_re, lb_im, bb_re, bb_im


def _complex_affine_combine(e1, e2):
    a1r, a1i, b1r, b1i = e1
    a2r, a2i, b2r, b2i = e2
    return (a1r * a2r - a1i * a2i,
            a1r * a2i + a1i * a2r,
            a2r * b1r - a2i * b1i + b2r,
            a2r * b1i + a2i * b1r + b2i)


def ssm_scan(u, h0_re, h0_im, lb_re, lb_im, bb_re, bb_im, c_re, c_im, d):
    bsz, t = u.shape[:2]
    uf = u.astype(jnp.float32)
    ug = uf.reshape(bsz, t, SSM_GROUPS, SSM_GROUP)
    bu_re = jnp.einsum('btgc,gnc->btgn', ug, bb_re)
    bu_im = jnp.einsum('btgc,gnc->btgn', ug, bb_im)
    h0r, h0i = h0_re.astype(jnp.float32), h0_im.astype(jnp.float32)
    bu_re = bu_re.at[:, 0].add(lb_re * h0r - lb_im * h0i)
    bu_im = bu_im.at[:, 0].add(lb_re * h0i + lb_im * h0r)
    a_re = jnp.broadcast_to(lb_re, bu_re.shape)
    a_im = jnp.broadcast_to(lb_im, bu_im.shape)
    _, _, h_re, h_im = lax.associative_scan(_complex_affine_combine, (a_re, a_im, bu_re, bu_im), axis=1)
    y = (jnp.einsum('btgn,gcn->btgc', h_re, c_re.astype(jnp.float32))
         - jnp.einsum('btgn,gcn->btgc', h_im, c_im.astype(jnp.float32)))
    y = y.reshape(bsz, t, SSM_WIDTH) + d.astype(jnp.float32) * uf
    return y.astype(u.dtype), h_re[:, -1], h_im[:, -1]


def mla_keys_values(ckv, kpe, lw):
    k_nope = jnp.einsum('...tc,chd->...thd', ckv, lw['w_uk'])
    v = jnp.einsum('...tc,chd->...thd', ckv, lw['w_uv'])
    k_rope = jnp.broadcast_to(kpe[..., None, :], k_nope.shape[:-1] + (QK_ROPE,))
    k_nope, k_rope = qk_norm(k_nope, k_rope, lw['k_norm_nope_g'], lw['k_norm_rope_g'])
    return jnp.concatenate([k_nope, k_rope], -1), v


def prompt_attend(q, ckv, kpe, lw):
    k, v = mla_keys_values(ckv, kpe, lw)
    bsz, t = q.shape[:2]
    kpos = jnp.arange(t)

    def block(i):
        qb = lax.dynamic_slice_in_dim(q, i * Q_BLOCK, Q_BLOCK, axis=1)
        s = jnp.einsum('bqhd,bkhd->bhqk', qb, k).astype(jnp.float32) * ATTN_SCALE
        qpos = i * Q_BLOCK + jnp.arange(Q_BLOCK)
        s = jnp.where(kpos[None, :] <= qpos[:, None], s, -jnp.inf)
        pr = jax.nn.softmax(s, axis=-1).astype(v.dtype)
        return jnp.einsum('bhqk,bkhd->bqhd', pr, v)

    out = lax.map(block, jnp.arange(t // Q_BLOCK))
    return out.transpose(1, 0, 2, 3, 4).reshape(bsz, t, N_HEADS, V_HEAD)


def sample_attend(q, ckv, kpe, lw, cache_ckv_l, cache_kpe_l, page_table):
    past = page_table.shape[1] * PAGE_SIZE
    tq = q.shape[1]
    kidx = jnp.arange(past + tq)
    qidx = past + jnp.arange(tq)
    mask = kidx[None, :] <= qidx[:, None]

    def one(args):
        q_b, ckv_b, kpe_b, pt = args
        ckv_all = jnp.concatenate([cache_ckv_l[pt].reshape(-1, KV_LORA), ckv_b.astype(cache_ckv_l.dtype)], 0)
        kpe_all = jnp.concatenate([cache_kpe_l[pt].reshape(-1, QK_ROPE), kpe_b.astype(cache_kpe_l.dtype)], 0)
        k, v = mla_keys_values(ckv_all, kpe_all, lw)
        s = jnp.einsum('qhd,khd->hqk', q_b, k).astype(jnp.float32) * ATTN_SCALE
        s = jnp.where(mask[None], s, -jnp.inf)
        pr = jax.nn.softmax(s, axis=-1).astype(v.dtype)
        return jnp.einsum('hqk,khd->qhd', pr, v)

    return lax.map(one, (q, ckv, kpe, page_table))


def decoder_layer(x, p, cos, sin, h0_re, h0_im, attend, lw):
    bsz, t = x.shape[:2]
    h = rms_norm(x, lw['norm_attn_g'])
    proj = h @ lw['w_in']
    u, q_lat, kv_lat, k_rope, g_a, g_b = jnp.split(proj, SPLIT_POINTS, axis=-1)
    lb_re, lb_im, bb_re, bb_im = ssm_discretize(lw['ssm_lam_re'], lw['ssm_lam_im'], lw['ssm_log_dt'],
                                                lw['ssm_b_re'], lw['ssm_b_im'])
    y_ssm, hT_re, hT_im = ssm_scan(u, h0_re, h0_im, lb_re, lb_im, bb_re, bb_im,
                                   lw['ssm_c_re'], lw['ssm_c_im'], lw['ssm_d'])
    z = jax.nn.gelu(y_ssm)
    a_out = z * jax.nn.sigmoid(z @ lw['w_glu'])
    qc = rms_norm(q_lat, lw['q_norm_g'])
    q = jnp.einsum('btc,chd->bthd', qc, lw['w_uq'])
    q_nope = q[..., :QK_NOPE]
    q_rope = apply_rope(q[..., QK_NOPE:], cos[:, None, :], sin[:, None, :])
    q_nope, q_rope = qk_norm(q_nope, q_rope, lw['q_norm_nope_g'], lw['q_norm_rope_g'])
    q = jnp.concatenate([q_nope, q_rope], -1)
    ckv = rms_norm(kv_lat, lw['kv_norm_g'])
    kpe = apply_rope(k_rope, cos, sin)
    b_out = attend(q, ckv, kpe).reshape(bsz, t, N_HEADS * V_HEAD)
    merged = (jax.nn.sigmoid(g_a) * (a_out @ lw['w_branch_a'])
              + jax.nn.sigmoid(g_b) * (b_out @ lw['w_branch_b']))
    x = x + merged @ lw['w_out']
    hf = rms_norm(x, lw['norm_ffn_g'])
    x = x + (jax.nn.silu(hf @ lw['w_gate']) * (hf @ lw['w_up'])) @ lw['w_down']
    gate = jax.nn.sigmoid(rms_norm(x, lw['norm_ple_g']) @ lw['w_ple_gate'])
    x = x + gate * (p @ lw['w_ple'])
    return x, ckv, kpe, hT_re, hT_im


def setup_inputs(seed: int = 0) -> dict:
    key = jax.random.key(seed)
    keys = jax.random.split(key, 64)
    ks = iter([keys[i] for i in range(64)])

    def nrm(shape, scale):
        return jax.random.normal(next(ks), shape, jnp.float32) * scale

    def gain(shape):
        return 1.0 + 0.01 * jax.random.normal(next(ks), shape, jnp.float32)

    n_pages = PAST_LEN // PAGE_SIZE
    n_used = DEC_BATCH * n_pages
    n_pool = n_used + n_used // 4
    L = DEPTH
    inp = {}
    inp['x_prompt'] = nrm((BATCH, SEQ, D_MODEL), 1.0)
    inp['x_sample'] = nrm((DEC_BATCH, DEC_SEQ, D_MODEL), 1.0)
    inp['p_prompt'] = nrm((L, BATCH, SEQ, PLE_DIM), 1.0)
    inp['p_sample'] = nrm((L, DEC_BATCH, DEC_SEQ, PLE_DIM), 1.0)
    inp['cache_ckv'] = nrm((L, n_pool, PAGE_SIZE, KV_LORA), 1.0)
    inp['cache_kpe'] = nrm((L, n_pool, PAGE_SIZE, QK_ROPE), 1.0)
    inp['state_ssm_re'] = nrm((L, DEC_BATCH, SSM_GROUPS, SSM_STATE), 0.3)
    inp['state_ssm_im'] = nrm((L, DEC_BATCH, SSM_GROUPS, SSM_STATE), 0.3)
    inp['page_table'] = jax.random.permutation(next(ks), n_pool)[:n_used].reshape(DEC_BATCH, n_pages).astype(jnp.int32)
    inp['norm_attn_g'] = gain((L, D_MODEL))
    inp['w_in'] = nrm((L, D_MODEL, D_IN), D_MODEL ** -0.5)
    inp['q_norm_g'] = gain((L, Q_LORA))
    inp['w_uq'] = nrm((L, Q_LORA, N_HEADS, QK_HEAD), Q_LORA ** -0.5)
    inp['kv_norm_g'] = gain((L, KV_LORA))
    inp['w_uk'] = nrm((L, KV_LORA, N_HEADS, QK_NOPE), KV_LORA ** -0.5)
    inp['w_uv'] = nrm((L, KV_LORA, N_HEADS, V_HEAD), KV_LORA ** -0.5)
    inp['q_norm_nope_g'] = gain((L, QK_NOPE))
    inp['q_norm_rope_g'] = gain((L, QK_ROPE // 2))
    inp['k_norm_nope_g'] = gain((L, QK_NOPE))
    inp['k_norm_rope_g'] = gain((L, QK_ROPE // 2))
    inp['ssm_lam_re'] = -0.5 + nrm((L, SSM_GROUPS, SSM_STATE), 0.01)
    inp['ssm_lam_im'] = (math.pi * jnp.arange(SSM_STATE, dtype=jnp.float32))[None, None, :] + nrm((L, SSM_GROUPS, SSM_STATE), 0.01)
    inp['ssm_log_dt'] = jax.random.uniform(next(ks), (L, SSM_GROUPS), jnp.float32, math.log(DT_MIN), math.log(DT_MAX))
    inp['ssm_b_re'] = nrm((L, SSM_GROUPS, SSM_STATE, SSM_GROUP), (2 * SSM_GROUP) ** -0.5)
    inp['ssm_b_im'] = nrm((L, SSM_GROUPS, SSM_STATE, SSM_GROUP), (2 * SSM_GROUP) ** -0.5)
    inp['ssm_c_re'] = nrm((L, SSM_GROUPS, SSM_GROUP, SSM_STATE), SSM_STATE ** -0.5)
    inp['ssm_c_im'] = nrm((L, SSM_GROUPS, SSM_GROUP, SSM_STATE), SSM_STATE ** -0.5)
    inp['ssm_d'] = nrm((L, SSM_WIDTH), 1.0)
    inp['w_glu'] = nrm((L, SSM_WIDTH, SSM_WIDTH), SSM_WIDTH ** -0.5)
    inp['w_branch_a'] = nrm((L, SSM_WIDTH, D_MODEL), SSM_WIDTH ** -0.5)
    inp['w_branch_b'] = nrm((L, N_HEADS * V_HEAD, D_MODEL), (N_HEADS * V_HEAD) ** -0.5)
    inp['w_out'] = nrm((L, D_MODEL, D_MODEL), D_MODEL ** -0.5)
    inp['norm_ffn_g'] = gain((L, D_MODEL))
    inp['w_gate'] = nrm((L, D_MODEL, D_FF), D_MODEL ** -0.5)
    inp['w_up'] = nrm((L, D_MODEL, D_FF), D_MODEL ** -0.5)
    inp['w_down'] = nrm((L, D_FF, D_MODEL), D_FF ** -0.5)
    inp['norm_ple_g'] = gain((L, D_MODEL))
    inp['w_ple_gate'] = nrm((L, D_MODEL, D_MODEL), D_MODEL ** -0.5)
    inp['w_ple'] = nrm((L, PLE_DIM, D_MODEL), PLE_DIM ** -0.5)
    return inp


def reference(x_prompt, x_sample, p_prompt, p_sample, cache_ckv, cache_kpe, state_ssm_re, state_ssm_im,
              page_table, norm_attn_g, w_in, q_norm_g, w_uq, kv_norm_g, w_uk, w_uv, q_norm_nope_g,
              q_norm_rope_g, k_norm_nope_g, k_norm_rope_g, ssm_lam_re, ssm_lam_im, ssm_log_dt, ssm_b_re,
              ssm_b_im, ssm_c_re, ssm_c_im, ssm_d, w_glu, w_branch_a, w_branch_b, w_out, norm_ffn_g,
              w_gate, w_up, w_down, norm_ple_g, w_ple_gate, w_ple):
    cos_p, sin_p = rope_tables(jnp.arange(x_prompt.shape[1]))
    cos_s, sin_s = rope_tables(PAST_LEN + jnp.arange(x_sample.shape[1]))
    h0_prompt = jnp.zeros((x_prompt.shape[0], SSM_GROUPS, SSM_STATE), jnp.float32)
    xp, xs = x_prompt, x_sample
    ckv_p, kpe_p, hre_p, him_p = [], [], [], []
    ckv_s, kpe_s, hre_s, him_s = [], [], [], []
    for i in range(DEPTH):
        lw = dict(norm_attn_g=norm_attn_g[i], w_in=w_in[i], q_norm_g=q_norm_g[i], w_uq=w_uq[i],
                  kv_norm_g=kv_norm_g[i], w_uk=w_uk[i], w_uv=w_uv[i], q_norm_nope_g=q_norm_nope_g[i],
                  q_norm_rope_g=q_norm_rope_g[i], k_norm_nope_g=k_norm_nope_g[i],
                  k_norm_rope_g=k_norm_rope_g[i], ssm_lam_re=ssm_lam_re[i], ssm_lam_im=ssm_lam_im[i],
                  ssm_log_dt=ssm_log_dt[i], ssm_b_re=ssm_b_re[i], ssm_b_im=ssm_b_im[i],
                  ssm_c_re=ssm_c_re[i], ssm_c_im=ssm_c_im[i], ssm_d=ssm_d[i], w_glu=w_glu[i],
                  w_branch_a=w_branch_a[i], w_branch_b=w_branch_b[i], w_out=w_out[i],
                  norm_ffn_g=norm_ffn_g[i], w_gate=w_gate[i], w_up=w_up[i], w_down=w_down[i],
                  norm_ple_g=norm_ple_g[i], w_ple_gate=w_ple_gate[i], w_ple=w_ple[i])
        attend_p = functools.partial(prompt_attend, lw=lw)
        attend_s = functools.partial(sample_attend, lw=lw, cache_ckv_l=cache_ckv[i],
                                     cache_kpe_l=cache_kpe[i], page_table=page_table)
        xp, c1, k1, r1, m1 = decoder_layer(xp, p_prompt[i], cos_p, sin_p, h0_prompt, h0_prompt, attend_p, lw)
        xs, c2, k2, r2, m2 = decoder_layer(xs, p_sample[i], cos_s, sin_s, state_ssm_re[i], state_ssm_im[i], attend_s, lw)
        ckv_p.append(c1); kpe_p.append(k1); hre_p.append(r1); him_p.append(m1)
        ckv_s.append(c2); kpe_s.append(k2); hre_s.append(r2); him_s.append(m2)
    return (xp, xs,
            jnp.stack(ckv_p), jnp.stack(kpe_p), jnp.stack(hre_p), jnp.stack(him_p),
            jnp.stack(ckv_s), jnp.stack(kpe_s), jnp.stack(hre_s), jnp.stack(him_s))
```

```python
import functools
import math

import numpy as np
import jax
import jax.numpy as jnp
from jax import lax
from jax.experimental import pallas as pl
from jax.experimental.pallas import tpu as pltpu

N_HEADS = 8
QK_NOPE = 64
QK_ROPE = 32
QK_HEAD = QK_NOPE + QK_ROPE
V_HEAD = 64
HEAD_BLOCK = 128
ROPE_THETA = 10000.0
ATTN_SCALE = QK_HEAD ** -0.5
SSM_GROUP = 16
SSM_STATE = 64
PAGE_SIZE = 128
EPS = 1e-6

LANES = 128
SUBLANES = 8
VMEM_LIMIT_BYTES = 56 * 1024 * 1024

BF16 = jnp.bfloat16
F32 = jnp.float32
NEG_INF = float("-inf")


def _dot(a, b):
    return jnp.dot(a, b, preferred_element_type=F32)


def _dot_nt(a, b):
    return lax.dot_general(a, b, (((1,), (1,)), ((), ())), preferred_element_type=F32)


def _split_dot(x, w):
    hi = x.astype(BF16)
    lo = (x - hi.astype(F32)).astype(BF16)
    return _dot(hi, w) + _dot(lo, w)


def _const_spec(shape):
    nd = len(shape)
    return pl.BlockSpec(shape, lambda *_: (0,) * nd, pipeline_mode=pl.Buffered(1))


def _params(*sem):
    return pltpu.CompilerParams(dimension_semantics=sem, vmem_limit_bytes=VMEM_LIMIT_BYTES)


def _discretize_kernel(lr_ref, li_ref, ldt_ref, br_ref, bi_ref, lbr_ref, lbi_ref, bbr_ref, bbi_ref):
    lr, li = lr_ref[...], li_ref[...]
    dt = jnp.exp(ldt_ref[...])
    mag = jnp.exp(lr * dt)
    lb_re, lb_im = mag * jnp.cos(li * dt), mag * jnp.sin(li * dt)
    num_re, num_im = lb_re - 1.0, lb_im
    den = lr * lr + li * li
    f_re = (num_re * lr + num_im * li) / den
    f_im = (num_im * lr - num_re * li) / den
    br, bi = br_ref[...], bi_ref[...]
    lbr_ref[...] = lb_re
    lbi_ref[...] = lb_im
    bbr_ref[...] = f_re * br - f_im * bi
    bbi_ref[...] = f_re * bi + f_im * br


def _discretize(lam_re, lam_im, log_dt, b_re, b_im):
    g, n, c = b_re.shape
    rep = lambda a: jnp.repeat(a, c, axis=-1)
    args = (rep(lam_re), rep(lam_im), jnp.broadcast_to(log_dt[:, None], (g, n * c)),
            b_re.reshape(g, n * c), b_im.reshape(g, n * c))
    sds = jax.ShapeDtypeStruct((g, n * c), F32)
    lbr, lbi, bbr, bbi = pl.pallas_call(_discretize_kernel, out_shape=(sds,) * 4, name="discretize")(*args)
    lb_re = lbr.reshape(g, n, c)[..., 0]
    lb_im = lbi.reshape(g, n, c)[..., 0]
    return lb_re, lb_im, bbr.reshape(g, n, c), bbi.reshape(g, n, c)


def _in_proj_kernel(x_ref, cos_ref, sin_ref, gattn_ref, wu_ref, wq_ref, wkv_ref, wkr_ref, wga_ref, wgb_ref,
                    gq_ref, wuq_ref, gkv_ref, wuk_ref, wuv_ref, gqh_ref, gkh_ref, e_ref, et_ref,
                    u_ref, q_ref, k_ref, v_ref, ckv_ref, kpe_ref, sga_ref, sgb_ref):
    x = x_ref[...]
    h = x * lax.rsqrt(jnp.mean(x * x, axis=-1, keepdims=True) + EPS) * gattn_ref[...]
    hb = h.astype(BF16)
    cos, sin = cos_ref[...], sin_ref[...]
    cos8 = jnp.concatenate([cos] * N_HEADS, axis=1)
    sin8 = jnp.concatenate([sin] * N_HEADS, axis=1)

    u_ref[...] = _dot(hb, wu_ref[...])
    sga_ref[...] = jax.nn.sigmoid(_dot(hb, wga_ref[...])).astype(sga_ref.dtype)
    sgb_ref[...] = jax.nn.sigmoid(_dot(hb, wgb_ref[...])).astype(sgb_ref.dtype)

    def head_norm(t, gain):
        ssq = _split_dot(t * t, e_ref[...])
        r = lax.rsqrt(ssq * (1.0 / QK_HEAD) + EPS)
        return t * _split_dot(r, et_ref[...]) * gain

    q_lat = _dot(hb, wq_ref[...])
    qc = q_lat * lax.rsqrt(jnp.mean(q_lat * q_lat, axis=-1, keepdims=True) + EPS) * gq_ref[...]
    qq = _dot(qc.astype(BF16), wuq_ref[...])
    hw = N_HEADS * HEAD_BLOCK
    q_pre = qq[:, :hw] * cos8 + qq[:, hw:] * sin8
    q_ref[...] = (head_norm(q_pre, gqh_ref[...]) * ATTN_SCALE).astype(q_ref.dtype)

    kv_lat = _dot(hb, wkv_ref[...])
    ckv = kv_lat * lax.rsqrt(jnp.mean(kv_lat * kv_lat, axis=-1, keepdims=True) + EPS) * gkv_ref[...]
    ckv_ref[...] = ckv
    kr = _dot(hb, wkr_ref[...])
    kpe_wide = kr[:, :HEAD_BLOCK] * cos + kr[:, HEAD_BLOCK:] * sin
    kpe_ref[...] = kpe_wide[:, QK_NOPE:QK_NOPE + QK_ROPE]

    cb = ckv.astype(BF16)
    k_pre = _dot(cb, wuk_ref[...]) + jnp.concatenate([kpe_wide] * N_HEADS, axis=1)
    k_ref[...] = head_norm(k_pre, gkh_ref[...]).astype(k_ref.dtype)
    v_ref[...] = _dot(cb, wuv_ref[...]).astype(v_ref.dtype)


def _in_proj(x, cos_t, sin_t, w, tm):
    n, d = x.shape
    assert n % tm == 0
    row = lambda width: pl.BlockSpec((tm, width), lambda i: (i, 0))
    consts = [w[k] for k in ("g_attn", "w_u", "w_q", "w_kv", "w_kr", "w_ga", "w_gb", "g_q", "w_uq", "g_kv",
                             "w_uk", "w_uv", "g_qh", "g_kh", "e", "et")]
    hw = N_HEADS * HEAD_BLOCK
    out_shape = (jax.ShapeDtypeStruct((n, w["w_u"].shape[1]), F32),
                 jax.ShapeDtypeStruct((n, hw), BF16),
                 jax.ShapeDtypeStruct((n, hw), BF16),
                 jax.ShapeDtypeStruct((n, N_HEADS * V_HEAD), BF16),
                 jax.ShapeDtypeStruct((n, w["w_kv"].shape[1]), F32),
                 jax.ShapeDtypeStruct((n, QK_ROPE), F32),
                 jax.ShapeDtypeStruct((n, d), BF16),
                 jax.ShapeDtypeStruct((n, d), BF16))
    return pl.pallas_call(
        _in_proj_kernel,
        grid=(n // tm,),
        in_specs=[row(d), row(LANES), row(LANES)] + [_const_spec(c.shape) for c in consts],
        out_specs=tuple(row(s.shape[1]) for s in out_shape),
        out_shape=out_shape,
        compiler_params=_params("parallel"),
        name="in_proj",
    )(x, cos_t, sin_t, *consts)


def _ssm_kernel(u_ref, h0_ref, ar_ref, ai_ref, wbu_ref, wy_ref, d_ref, wglu_ref,
                *rest, chains, steps, emit_y):
    if emit_y:
        a_ref, ht_ref, bu_ref, hc_ref = rest
    else:
        ht_ref, bu_ref, hc_ref = rest
    i = pl.program_id(0)
    ns = ar_ref.shape[1]

    @pl.when(i == 0)
    def _():
        hc_ref[...] = h0_ref[...]

    u = u_ref[...]
    bu_ref[...] = _dot(u.astype(BF16), wbu_ref[...])

    def step(j, carry):
        rows = pl.ds(pl.multiple_of(j * chains, SUBLANES), chains)
        ar, ai = ar_ref[...], ai_ref[...]
        hr, hi = hc_ref[:, :ns], hc_ref[:, ns:]
        nr = ar * hr - ai * hi + bu_ref[rows, :ns]
        ni = ar * hi + ai * hr + bu_ref[rows, ns:]
        hc_ref[:, :ns] = nr
        hc_ref[:, ns:] = ni
        if emit_y:
            bu_ref[rows, :ns] = nr
            bu_ref[rows, ns:] = ni
        return carry

    lax.fori_loop(0, steps, step, 0)
    ht_ref[...] = hc_ref[...]
    if emit_y:
        y = _dot(bu_ref[...].astype(BF16), wy_ref[...]) + d_ref[...] * u
        z = jax.nn.gelu(y)
        a_ref[...] = (z * jax.nn.sigmoid(_dot(z.astype(BF16), wglu_ref[...]))).astype(a_ref.dtype)


def _ssm(u_rows, h0, w, chains, steps, emit_y):
    n, width = u_rows.shape
    rows = chains * steps
    assert n % rows == 0
    ns2 = h0.shape[1]
    consts = [w[k] for k in ("a_re", "a_im", "w_bu", "w_y", "ssm_d", "w_glu")]
    ht_sds = jax.ShapeDtypeStruct((chains, ns2), F32)
    ht_spec = pl.BlockSpec((chains, ns2), lambda i: (0, 0))
    if emit_y:
        out_shape = (jax.ShapeDtypeStruct((n, width), BF16), ht_sds)
        out_specs = (pl.BlockSpec((rows, width), lambda i: (i, 0)), ht_spec)
    else:
        out_shape, out_specs = ht_sds, ht_spec
    return pl.pallas_call(
        functools.partial(_ssm_kernel, chains=chains, steps=steps, emit_y=emit_y),
        grid=(n // rows,),
        in_specs=[pl.BlockSpec((rows, width), lambda i: (i, 0)), _const_spec(h0.shape)]
        + [_const_spec(c.shape) for c in consts],
        out_specs=out_specs,
        out_shape=out_shape,
        scratch_shapes=[pltpu.VMEM((rows, ns2), F32), pltpu.VMEM((chains, ns2), F32)],
        compiler_params=_params("arbitrary"),
        name="ssm_scan" if emit_y else "ssm_end_state",
    )(u_rows, h0, *consts)


def _ssm_combine_kernel(hend_ref, ar_ref, ai_ref, h0_ref, *, seg_len, n_seg):
    ns = ar_ref.shape[1]
    pr, pi = ar_ref[...], ai_ref[...]
    qr, qi = None, None
    e = seg_len
    while e:
        if e & 1:
            qr, qi = (pr, pi) if qr is None else (qr * pr - qi * pi, qr * pi + qi * pr)
        e >>= 1
        if e:
            pr, pi = pr * pr - pi * pi, 2.0 * pr * pi
    er, ei = hend_ref[:, :ns], hend_ref[:, ns:]
    seg = lax.broadcasted_iota(jnp.int32, er.shape, 0) % n_seg
    hr, hi = jnp.zeros_like(er), jnp.zeros_like(ei)
    for s in range(1, n_seg):
        xr = qr * hr - qi * hi + er
        xi = qr * hi + qi * hr + ei
        hr = jnp.where(seg == s, pltpu.roll(xr, 1, 0), hr)
        hi = jnp.where(seg == s, pltpu.roll(xi, 1, 0), hi)
    h0_ref[:, :ns] = hr
    h0_ref[:, ns:] = hi


def _ssm_combine(hend, w, seg_len, n_seg):
    return pl.pallas_call(
        functools.partial(_ssm_combine_kernel, seg_len=seg_len, n_seg=n_seg),
        out_shape=jax.ShapeDtypeStruct(hend.shape, F32),
        name="ssm_combine",
    )(hend, w["a_re"], w["a_im"])


def _flash_kernel(q_ref, k_ref, v_ref, o_ref, *, tq):
    i = pl.program_id(2)
    lane = lax.broadcasted_iota(jnp.int32, (tq, 2 * V_HEAD), 1)
    outs = []
    for hh in range(2):
        q = q_ref[:, hh * HEAD_BLOCK:(hh + 1) * HEAD_BLOCK]

        def block(j, carry, masked):
            m, l, acc = carry
            rows = pl.ds(pl.multiple_of(j * tq, tq), tq)
            s = _dot_nt(q, k_ref[rows, hh * HEAD_BLOCK:(hh + 1) * HEAD_BLOCK])
            if masked:
                qi = lax.broadcasted_iota(jnp.int32, s.shape, 0)
                ki = lax.broadcasted_iota(jnp.int32, s.shape, 1)
                s = jnp.where(ki <= qi, s, NEG_INF)
            m_new = jnp.maximum(m, jnp.max(s, axis=1, keepdims=True))
            p = jnp.exp(s - m_new)
            alpha = jnp.exp(m - m_new)
            l = alpha * l + jnp.sum(p, axis=1, keepdims=True)
            acc = alpha * acc + _dot(p.astype(BF16), v_ref[rows, :])
            return m_new, l, acc

        init = (jnp.full((tq, 1), NEG_INF, F32), jnp.zeros((tq, 1), F32), jnp.zeros((tq, 2 * V_HEAD), F32))
        carry = lax.fori_loop(0, i, functools.partial(block, masked=False), init)
        m, l, acc = block(i, carry, True)
        outs.append(acc / l)
    o_ref[...] = jnp.where(lane < V_HEAD, outs[0], outs[1]).astype(o_ref.dtype)


def _flash(q, k, v, batch, seq, tq):
    nq = seq // tq
    pairs = N_HEADS // 2
    return pl.pallas_call(
        functools.partial(_flash_kernel, tq=tq),
        grid=(batch, pairs, nq),
        in_specs=[pl.BlockSpec((tq, 2 * HEAD_BLOCK), lambda b, p, i: (b * nq + i, p)),
                  pl.BlockSpec((seq, 2 * HEAD_BLOCK), lambda b, p, i: (b, p)),
                  pl.BlockSpec((seq, 2 * V_HEAD), lambda b, p, i: (b, p))],
        out_specs=pl.BlockSpec((tq, 2 * V_HEAD), lambda b, p, i: (b * nq + i, p)),
        out_shape=jax.ShapeDtypeStruct((batch * seq, N_HEADS * V_HEAD), BF16),
        compiler_params=_params("parallel", "parallel", "arbitrary"),
        name="flash",
    )(q, k, v)


def _absorb_kernel(q_ref, gk_ref, wabs_ref, sel_ref, qabs_ref, qr_ref):
    qg = (q_ref[...].astype(F32) * gk_ref[...]).astype(BF16)
    qabs_ref[...] = _dot(qg, wabs_ref[...]).astype(qabs_ref.dtype)
    qr_ref[...] = _dot(qg, sel_ref[...]).astype(qr_ref.dtype)


def _absorb(q, w):
    n = q.shape[0]
    return pl.pallas_call(
        _absorb_kernel,
        out_shape=(jax.ShapeDtypeStruct((n, w["w_abs"].shape[1]), BF16),
                   jax.ShapeDtypeStruct((n, w["sel_rope"].shape[1]), BF16)),
        compiler_params=_params(),
        name="absorb",
    )(q, w["g_kh"], w["w_abs"], w["sel_rope"])


def _paged_attn_kernel(pt_ref, qabs_ref, qr_ref, ckvn_ref, kpen_ref, wukt_ref, ckv_hbm, kpe_hbm,
                       o_ref, ckv_buf, kpe_buf, lhs_ref, m_ref, l_ref, acc_ref, sem_c, sem_k,
                       *, n_chunks, chunk_pages, pages_per_seq, n_new):
    s = pl.program_id(0)
    total = pl.num_programs(0)
    c = s % n_chunks
    slot = s % 2
    kv_lora = wukt_ref.shape[1]
    n_k = wukt_ref.shape[0]
    rows_q = qabs_ref.shape[1]

    def page_copies(step, slt):
        seq, chunk = step // n_chunks, step % n_chunks
        base = seq * pages_per_seq + chunk * chunk_pages
        out = []
        for p in range(chunk_pages):
            page = pt_ref[base + p]
            dst = pl.ds(p * PAGE_SIZE, PAGE_SIZE)
            out.append(pltpu.make_async_copy(ckv_hbm.at[page], ckv_buf.at[slt, dst, :], sem_c.at[slt]))
            out.append(pltpu.make_async_copy(kpe_hbm.at[page], kpe_buf.at[slt, dst, :], sem_k.at[slt]))
        return out

    @pl.when(s == 0)
    def _():
        lhs_ref[:n_k, :] = wukt_ref[...]
        for cp in page_copies(s, slot):
            cp.start()

    @pl.when(s + 1 < total)
    def _():
        for cp in page_copies(s + 1, 1 - slot):
            cp.start()

    @pl.when(c == 0)
    def _():
        lhs_ref[n_k:, :] = qabs_ref[0]
        m_ref[...] = jnp.full(m_ref.shape, NEG_INF, F32)
        l_ref[...] = jnp.zeros(l_ref.shape, F32)
        acc_ref[...] = jnp.zeros(acc_ref.shape, F32)

    for cp in page_copies(s, slot):
        cp.wait()

    qr = qr_ref[0][:, :QK_ROPE]
    ones = jnp.ones((SUBLANES, QK_ROPE), BF16)

    def attend(ckv, kpe, mask):
        tk = ckv.shape[0]
        cb = ckv.astype(BF16)
        both = _dot_nt(lhs_ref[...], cb)
        kt = both[:n_k]
        ssq = jnp.sum((kt * kt).reshape(N_HEADS, QK_NOPE, tk), axis=1)
        kp2 = kpe * kpe
        kp2_hi = kp2.astype(BF16)
        kp2_lo = (kp2 - kp2_hi.astype(F32)).astype(BF16)
        ssq = ssq + _dot_nt(ones, kp2_hi) + _dot_nt(ones, kp2_lo)
        r = lax.rsqrt(ssq * (1.0 / QK_HEAD) + EPS)
        sc = both[n_k:] + _dot_nt(qr, kpe.astype(BF16))
        sc = (sc.reshape(n_new, N_HEADS, tk) * r[None]).reshape(rows_q, tk)
        if mask is not None:
            sc = jnp.where(mask, sc, NEG_INF)
        m_old = m_ref[...]
        m_new = jnp.maximum(m_old, jnp.max(sc, axis=1, keepdims=True))
        p = jnp.exp(sc - m_new)
        alpha = jnp.exp(m_old - m_new)
        l_ref[...] = alpha * l_ref[...] + jnp.sum(p, axis=1, keepdims=True)
        acc_ref[...] = alpha * acc_ref[...] + _dot(p.astype(BF16), cb)
        m_ref[...] = m_new

    attend(ckv_buf[slot], kpe_buf[slot], None)

    @pl.when(c == n_chunks - 1)
    def _():
        tkn = ckvn_ref.shape[1]
        key = lax.broadcasted_iota(jnp.int32, (rows_q, tkn), 1)
        tok = lax.broadcasted_iota(jnp.int32, (rows_q, tkn), 0) // N_HEADS
        attend(ckvn_ref[0], kpen_ref[0], key <= tok)
        o_ref[0] = (acc_ref[...] / l_ref[...]).astype(o_ref.dtype)


def _paged_attn(page_table, qabs, qr, ckv_new, kpe_new, w_ukt, cache_ckv, cache_kpe, chunk_pages):
    n_seq, pages_per_seq = page_table.shape
    assert pages_per_seq % chunk_pages == 0
    n_chunks = pages_per_seq // chunk_pages
    rows_q, kv_lora = qabs.shape[1:]
    n_new = rows_q // N_HEADS
    tk = chunk_pages * PAGE_SIZE
    seq_block = lambda shape: pl.BlockSpec((1,) + shape, lambda s, pt: (s // n_chunks, 0, 0))
    grid_spec = pltpu.PrefetchScalarGridSpec(
        num_scalar_prefetch=1,
        grid=(n_seq * n_chunks,),
        in_specs=[seq_block(qabs.shape[1:]), seq_block(qr.shape[1:]), seq_block(ckv_new.shape[1:]),
                  seq_block(kpe_new.shape[1:]),
                  pl.BlockSpec(w_ukt.shape, lambda s, pt: (0, 0)),
                  pl.BlockSpec(memory_space=pl.ANY), pl.BlockSpec(memory_space=pl.ANY)],
        out_specs=seq_block((rows_q, kv_lora)),
        scratch_shapes=[pltpu.VMEM((2, tk, kv_lora), F32), pltpu.VMEM((2, tk, QK_ROPE), F32),
                        pltpu.VMEM((w_ukt.shape[0] + rows_q, kv_lora), BF16),
                        pltpu.VMEM((rows_q, 1), F32), pltpu.VMEM((rows_q, 1), F32),
                        pltpu.VMEM((rows_q, kv_lora), F32),
                        pltpu.SemaphoreType.DMA((2,)), pltpu.SemaphoreType.DMA((2,))])
    return pl.pallas_call(
        functools.partial(_paged_attn_kernel, n_chunks=n_chunks, chunk_pages=chunk_pages,
                          pages_per_seq=pages_per_seq, n_new=n_new),
        grid_spec=grid_spec,
        out_shape=jax.ShapeDtypeStruct((n_seq, rows_q, kv_lora), BF16),
        compiler_params=_params("arbitrary"),
        name="paged_attn",
    )(page_table.reshape(-1), qabs, qr, ckv_new, kpe_new, w_ukt, cache_ckv, cache_kpe)


def _out_proj_kernel(x_ref, a_ref, b_ref, sga_ref, sgb_ref, p_ref, wuvb_ref, wa_ref, wb_ref, wo_ref, gffn_ref,
                     wg_ref, wup_ref, wdn_ref, gple_ref, wpg_ref, wple_ref, y_ref, *, latent_b):
    b = b_ref[...]
    if latent_b:
        b = _dot(b, wuvb_ref[...]).astype(BF16)
    merged = (sga_ref[...].astype(F32) * _dot(a_ref[...], wa_ref[...])
              + sgb_ref[...].astype(F32) * _dot(b, wb_ref[...]))
    x = x_ref[...] + _dot(merged.astype(BF16), wo_ref[...])

    def norm(t, g_ref):
        return (t * lax.rsqrt(jnp.mean(t * t, axis=-1, keepdims=True) + EPS) * g_ref[...]).astype(BF16)

    hf = norm(x, gffn_ref)
    act = jax.nn.silu(_dot(hf, wg_ref[...])) * _dot(hf, wup_ref[...])
    x = x + _dot(act.astype(BF16), wdn_ref[...])
    gate = jax.nn.sigmoid(_dot(norm(x, gple_ref), wpg_ref[...]))
    y_ref[...] = x + gate * _dot(p_ref[...].astype(BF16), wple_ref[...])


def _out_proj(x, a, b, sga, sgb, p, w, tm, latent_b):
    n, d = x.shape
    assert n % tm == 0
    row = lambda width: pl.BlockSpec((tm, width), lambda i: (i, 0))
    consts = [w[k] for k in ("w_uv_bd", "w_a", "w_b", "w_out", "g_ffn", "w_gate", "w_up", "w_down", "g_ple",
                             "w_ple_gate", "w_ple")]
    return pl.pallas_call(
        functools.partial(_out_proj_kernel, latent_b=latent_b),
        grid=(n // tm,),
        in_specs=[row(d), row(a.shape[1]), row(b.shape[1]), row(d), row(d), row(p.shape[1])]
        + [_const_spec(c.shape) for c in consts],
        out_specs=row(d),
        out_shape=jax.ShapeDtypeStruct((n, d), F32),
        compiler_params=_params("parallel"),
        name="out_proj_latent" if latent_b else "out_proj",
    )(x, a, b, sga, sgb, p, *consts)


def _head_blocked(w_nope, w_rope):
    k = w_nope.shape[0] if w_nope is not None else w_rope.shape[0]
    nope = w_nope if w_nope is not None else jnp.zeros((k, N_HEADS, QK_NOPE), F32)
    rope = w_rope if w_rope is not None else jnp.zeros((k, N_HEADS, QK_ROPE), F32)
    pad = jnp.zeros((k, N_HEADS, HEAD_BLOCK - QK_HEAD), F32)
    return jnp.concatenate([nope, rope, pad], axis=-1).reshape(k, N_HEADS * HEAD_BLOCK)


def _rot_cols(w):
    half = QK_ROPE // 2
    return jnp.concatenate([-w[..., half:], w[..., :half]], axis=-1)


def _head_gain(g_nope, g_rope_half):
    g = jnp.concatenate([g_nope, g_rope_half, g_rope_half, jnp.zeros((HEAD_BLOCK - QK_HEAD,), F32)])
    return jnp.tile(g, N_HEADS)[None, :]


def _block_diag(blocks):
    g, r, c = blocks.shape
    eye = jnp.eye(g, dtype=blocks.dtype)
    return jnp.einsum("grc,gh->grhc", blocks, eye).reshape(g * r, g * c)


def _prepare(norm_attn_g, w_in, q_norm_g, w_uq, kv_norm_g, w_uk, w_uv, q_norm_nope_g, q_norm_rope_g,
             k_norm_nope_g, k_norm_rope_g, disc, ssm_c_re, ssm_c_im, ssm_d, w_glu, w_branch_a, w_branch_b,
             w_out, norm_ffn_g, w_gate, w_up, w_down, norm_ple_g, w_ple_gate, w_ple):
    d_model = w_in.shape[0]
    ssm_w = ssm_d.shape[0]
    q_lora, kv_lora = w_uq.shape[0], w_uk.shape[0]
    o = np.cumsum([0, ssm_w, q_lora, kv_lora, QK_ROPE, d_model, d_model])
    bf = lambda a: a.astype(BF16)
    w = {}
    w["g_attn"] = norm_attn_g[None, :]
    w["w_u"] = bf(w_in[:, o[0]:o[1]])
    w["w_q"] = bf(w_in[:, o[1]:o[2]])
    w["w_kv"] = bf(w_in[:, o[2]:o[3]])
    w_kr = w_in[:, o[3]:o[4]]
    place = lambda r: jnp.pad(r, ((0, 0), (QK_NOPE, HEAD_BLOCK - QK_HEAD)))
    w["w_kr"] = bf(jnp.concatenate([place(w_kr), place(_rot_cols(w_kr))], axis=1))
    w["w_ga"] = bf(w_in[:, o[4]:o[5]])
    w["w_gb"] = bf(w_in[:, o[5]:o[6]])
    w["g_q"] = q_norm_g[None, :]
    uq_nope, uq_rope = w_uq[..., :QK_NOPE], w_uq[..., QK_NOPE:]
    w["w_uq"] = bf(jnp.concatenate([_head_blocked(uq_nope, uq_rope), _head_blocked(None, _rot_cols(uq_rope))], axis=1))
    w["g_kv"] = kv_norm_g[None, :]
    w["w_uk"] = bf(_head_blocked(w_uk, None))
    w["w_uv"] = bf(w_uv.reshape(kv_lora, N_HEADS * V_HEAD))
    w["g_qh"] = _head_gain(q_norm_nope_g, q_norm_rope_g)
    w["g_kh"] = _head_gain(k_norm_nope_g, k_norm_rope_g)
    e = np.zeros((N_HEADS * HEAD_BLOCK, LANES), np.float32)
    for h in range(N_HEADS):
        e[h * HEAD_BLOCK:h * HEAD_BLOCK + QK_HEAD, h] = 1.0
    w["e"] = jnp.asarray(e, BF16)
    w["et"] = jnp.asarray(e.T, BF16)
    ukt = jnp.transpose(w_uk, (1, 2, 0))
    ukt = jnp.pad(ukt, ((0, 0), (0, HEAD_BLOCK - QK_NOPE), (0, 0)))
    w["w_abs"] = bf(_block_diag(ukt))
    sel = np.zeros((N_HEADS * HEAD_BLOCK, N_HEADS * HEAD_BLOCK), np.float32)
    for h in range(N_HEADS):
        for dd in range(QK_ROPE):
            sel[h * HEAD_BLOCK + QK_NOPE + dd, h * HEAD_BLOCK + dd] = 1.0
    w["sel_rope"] = jnp.asarray(sel, BF16)
    w["w_ukt"] = bf(jnp.transpose(w_uk, (1, 2, 0)).reshape(N_HEADS * QK_NOPE, kv_lora))
    w["w_uv_bd"] = bf(_block_diag(jnp.transpose(w_uv, (1, 0, 2))))
    lb_re, lb_im, bb_re, bb_im = disc
    w["a_re"] = lb_re.reshape(1, -1)
    w["a_im"] = lb_im.reshape(1, -1)
    tr = lambda a: jnp.transpose(a, (0, 2, 1))
    w["w_bu"] = bf(jnp.concatenate([_block_diag(tr(bb_re)), _block_diag(tr(bb_im))], axis=1))
    w["w_y"] = bf(jnp.concatenate([_block_diag(tr(ssm_c_re)), -_block_diag(tr(ssm_c_im))], axis=0))
    w["ssm_d"] = ssm_d[None, :]
    w["w_glu"] = bf(w_glu)
    w["w_a"], w["w_b"], w["w_out"] = bf(w_branch_a), bf(w_branch_b), bf(w_out)
    w["g_ffn"], w["g_ple"] = norm_ffn_g[None, :], norm_ple_g[None, :]
    w["w_gate"], w["w_up"], w["w_down"] = bf(w_gate), bf(w_up), bf(w_down)
    w["w_ple_gate"], w["w_ple"] = bf(w_ple_gate), bf(w_ple)
    return w


def _rope_tables(pos):
    half = QK_ROPE // 2
    inv = ROPE_THETA ** (-jnp.arange(half, dtype=F32) / half)
    ang = pos.astype(F32)[:, None] * inv[None, :]
    cos, sin = jnp.cos(ang), jnp.sin(ang)
    n = pos.shape[0]
    ones, zeros = jnp.ones((n, QK_NOPE), F32), jnp.zeros((n, QK_NOPE), F32)
    pad = jnp.zeros((n, HEAD_BLOCK - QK_HEAD), F32)
    return (jnp.concatenate([ones, cos, cos, pad], axis=1), jnp.concatenate([zeros, sin, sin, pad], axis=1))


def _pick(n, prefs):
    for t in prefs:
        if n % t == 0:
            return t
    return n


def kernel(x_prompt, x_sample, p_prompt, p_sample, cache_ckv, cache_kpe, state_ssm_re, state_ssm_im, page_table, norm_attn_g, w_in, q_norm_g, w_uq, kv_norm_g, w_uk, w_uv, q_norm_nope_g, q_norm_rope_g, k_norm_nope_g, k_norm_rope_g, ssm_lam_re, ssm_lam_im, ssm_log_dt, ssm_b_re, ssm_b_im, ssm_c_re, ssm_c_im, ssm_d, w_glu, w_branch_a, w_branch_b, w_out, norm_ffn_g, w_gate, w_up, w_down, norm_ple_g, w_ple_gate, w_ple):
    depth = w_in.shape[0]
    batch, seq, d_model = x_prompt.shape
    n_seq, n_new, _ = x_sample.shape
    past = page_table.shape[1] * PAGE_SIZE
    groups, n_state = ssm_lam_re.shape[1:]
    ns = groups * n_state

    cos_p, sin_p = _rope_tables(jnp.arange(seq))
    cos_p, sin_p = jnp.tile(cos_p, (batch, 1)), jnp.tile(sin_p, (batch, 1))
    cos_s, sin_s = _rope_tables(past + jnp.arange(n_new))
    cos_s, sin_s = jnp.tile(cos_s, (n_seq, 1)), jnp.tile(sin_s, (n_seq, 1))

    n_seg = SUBLANES
    assert seq % n_seg == 0
    seg_len = seq // n_seg
    chains_p = batch * n_seg
    steps_p = _pick(seg_len, (16, 8, 4, 2))
    steps_s = _pick(n_new, (4, 2))
    tm_p = _pick(batch * seq, (512, 256, 128))
    tm_s = _pick(n_seq * n_new, (512, 256, 128))
    tq = _pick(seq, (256, 128))
    chunk_pages = _pick(page_table.shape[1], (16, 8, 4, 2))

    xp = x_prompt.reshape(batch * seq, d_model)
    xs = x_sample.reshape(n_seq * n_new, d_model)
    outs = {k: [] for k in ("ckv_p", "kpe_p", "hre_p", "him_p", "ckv_s", "kpe_s", "hre_s", "him_s")}
    for i in range(depth):
        disc = _discretize(ssm_lam_re[i], ssm_lam_im[i], ssm_log_dt[i], ssm_b_re[i], ssm_b_im[i])
        w = _prepare(norm_attn_g[i], w_in[i], q_norm_g[i], w_uq[i], kv_norm_g[i], w_uk[i], w_uv[i],
                     q_norm_nope_g[i], q_norm_rope_g[i], k_norm_nope_g[i], k_norm_rope_g[i], disc,
                     ssm_c_re[i], ssm_c_im[i], ssm_d[i], w_glu[i], w_branch_a[i], w_branch_b[i], w_out[i],
                     norm_ffn_g[i], w_gate[i], w_up[i], w_down[i], norm_ple_g[i], w_ple_gate[i], w_ple[i])
        ssm_w = w["w_u"].shape[1]

        u, q, k, v, ckv, kpe, sga, sgb = _in_proj(xp, cos_p, sin_p, w, tm_p)
        u_rows = u.reshape(batch, n_seg, seg_len, ssm_w).transpose(2, 0, 1, 3).reshape(batch * seq, ssm_w)
        zero_state = jnp.zeros((chains_p, 2 * ns), F32)
        h_end = _ssm(u_rows, zero_state, w, chains_p, steps_p, emit_y=False)
        h_start = _ssm_combine(h_end, w, seg_len, n_seg)
        a_rows, h_fin = _ssm(u_rows, h_start, w, chains_p, steps_p, emit_y=True)
        a_out = a_rows.reshape(seg_len, batch, n_seg, ssm_w).transpose(1, 2, 0, 3).reshape(batch * seq, ssm_w)
        b_out = _flash(q, k, v, batch, seq, tq)
        xp = _out_proj(xp, a_out, b_out, sga, sgb, p_prompt[i].reshape(batch * seq, -1), w, tm_p, latent_b=False)
        h_fin = h_fin.reshape(batch, n_seg, 2 * ns)[:, -1]
        outs["ckv_p"].append(ckv.reshape(batch, seq, -1))
        outs["kpe_p"].append(kpe.reshape(batch, seq, -1))
        outs["hre_p"].append(h_fin[:, :ns].reshape(batch, groups, n_state))
        outs["him_p"].append(h_fin[:, ns:].reshape(batch, groups, n_state))

        u, q, _, _, ckv, kpe, sga, sgb = _in_proj(xs, cos_s, sin_s, w, tm_s)
        u_rows = u.reshape(n_seq, n_new, ssm_w).transpose(1, 0, 2).reshape(n_seq * n_new, ssm_w)
        h0 = jnp.concatenate([state_ssm_re[i].reshape(n_seq, ns), state_ssm_im[i].reshape(n_seq, ns)], axis=1)
        a_rows, h_fin = _ssm(u_rows, h0, w, n_seq, steps_s, emit_y=True)
        a_out = a_rows.reshape(n_new, n_seq, ssm_w).transpose(1, 0, 2).reshape(n_seq * n_new, ssm_w)
        qabs, qr = _absorb(q, w)
        rows_q = n_new * N_HEADS
        kv_lora = ckv.shape[1]
        pad_new = PAGE_SIZE - n_new
        ckv_new = jnp.pad(ckv.reshape(n_seq, n_new, kv_lora), ((0, 0), (0, pad_new), (0, 0)))
        kpe_new = jnp.pad(kpe.reshape(n_seq, n_new, QK_ROPE), ((0, 0), (0, pad_new), (0, 0)))
        o_lat = _paged_attn(page_table, qabs.reshape(n_seq, rows_q, kv_lora), qr.reshape(n_seq, rows_q, HEAD_BLOCK),
                            ckv_new, kpe_new, w["w_ukt"], cache_ckv[i], cache_kpe[i], chunk_pages)
        b_lat = o_lat.reshape(n_seq * n_new, N_HEADS * kv_lora)
        xs = _out_proj(xs, a_out, b_lat, sga, sgb, p_sample[i].reshape(n_seq * n_new, -1), w, tm_s, latent_b=True)
        outs["ckv_s"].append(ckv.reshape(n_seq, n_new, -1))
        outs["kpe_s"].append(kpe.reshape(n_seq, n_new, -1))
        outs["hre_s"].append(h_fin[:, :ns].reshape(n_seq, groups, n_state))
        outs["him_s"].append(h_fin[:, ns:].reshape(n_seq, groups, n_state))

    st = lambda key: jnp.stack(outs[key])
    return (xp.reshape(batch, seq, d_model), xs.reshape(n_seq, n_new, d_model),
            st("ckv_p"), st("kpe_p"), st("hre_p"), st("him_p"),
            st("ckv_s"), st("kpe_s"), st("hre_s"), st("him_s"))
```

```python
import functools
import math

import numpy as np
import jax
import jax.numpy as jnp
from jax import lax
from jax.experimental import pallas as pl
from jax.experimental.pallas import tpu as pltpu

N_HEADS = 8
QK_NOPE = 64
QK_ROPE = 32
QK_HEAD = QK_NOPE + QK_ROPE
V_HEAD = 64
VT_ROWS = 80
HEAD_BLOCK = 128
ROPE_THETA = 10000.0
ATTN_SCALE = QK_HEAD ** -0.5
LOG2E = math.log2(math.e)
SSM_GROUP = 16
SSM_STATE = 64
PAGE_SIZE = 128
KV_BLOCK = 256
EPS = 1e-6

LANES = 128
SUBLANES = 8
VMEM_LIMIT_BYTES = 56 * 1024 * 1024

BF16 = jnp.bfloat16
F32 = jnp.float32
NEG_INF = float("-inf")


def _dot(a, b):
    return jnp.dot(a, b, preferred_element_type=F32)


def _dot_nt(a, b):
    return lax.dot_general(a, b, (((1,), (1,)), ((), ())), preferred_element_type=F32)


def _split_dot(x, w):
    hi = x.astype(BF16)
    lo = (x - hi.astype(F32)).astype(BF16)
    return _dot(hi, w) + _dot(lo, w)


def _const_spec(shape):
    nd = len(shape)
    return pl.BlockSpec(shape, lambda *_: (0,) * nd, pipeline_mode=pl.Buffered(1))


def _params(*sem):
    return pltpu.CompilerParams(dimension_semantics=sem, vmem_limit_bytes=VMEM_LIMIT_BYTES)


def _discretize_kernel(lr_ref, li_ref, ldt_ref, br_ref, bi_ref, lbr_ref, lbi_ref, bbr_ref, bbi_ref):
    lr, li = lr_ref[...], li_ref[...]
    dt = jnp.exp(ldt_ref[...])
    mag = jnp.exp(lr * dt)
    lb_re, lb_im = mag * jnp.cos(li * dt), mag * jnp.sin(li * dt)
    num_re, num_im = lb_re - 1.0, lb_im
    den = lr * lr + li * li
    f_re = (num_re * lr + num_im * li) / den
    f_im = (num_im * lr - num_re * li) / den
    br, bi = br_ref[...], bi_ref[...]
    lbr_ref[...] = lb_re
    lbi_ref[...] = lb_im
    bbr_ref[...] = f_re * br - f_im * bi
    bbi_ref[...] = f_re * bi + f_im * br


def _discretize(lam_re, lam_im, log_dt, b_re, b_im):
    g, n, c = b_re.shape
    rep = lambda a: jnp.repeat(a, c, axis=-1)
    args = (rep(lam_re), rep(lam_im), jnp.broadcast_to(log_dt[:, None], (g, n * c)),
            b_re.reshape(g, n * c), b_im.reshape(g, n * c))
    sds = jax.ShapeDtypeStruct((g, n * c), F32)
    lbr, lbi, bbr, bbi = pl.pallas_call(_discretize_kernel, out_shape=(sds,) * 4, name="discretize")(*args)
    lb_re = lbr.reshape(g, n, c)[..., 0]
    lb_im = lbi.reshape(g, n, c)[..., 0]
    return lb_re, lb_im, bbr.reshape(g, n, c), bbi.reshape(g, n, c)


def _in_proj_kernel(x_ref, cos_ref, sin_ref, gattn_ref, wu_ref, wq_ref, wkv_ref, wkr_ref, wga_ref, wgb_ref,
                    gq_ref, wuq_ref, gkv_ref, wuk_ref, wuv_ref, gqh_ref, gkh_ref, e_ref, et_ref,
                    u_ref, q_ref, ckv_ref, kpe_ref, sga_ref, sgb_ref, *kv_refs):
    x = x_ref[...]
    h = x * lax.rsqrt(jnp.mean(x * x, axis=-1, keepdims=True) + EPS) * gattn_ref[...]
    hb = h.astype(BF16)
    cos, sin = cos_ref[...], sin_ref[...]
    cos8 = jnp.concatenate([cos] * N_HEADS, axis=1)
    sin8 = jnp.concatenate([sin] * N_HEADS, axis=1)

    u_ref[...] = _dot(hb, wu_ref[...])
    sga_ref[...] = jax.nn.sigmoid(_dot(hb, wga_ref[...])).astype(sga_ref.dtype)
    sgb_ref[...] = jax.nn.sigmoid(_dot(hb, wgb_ref[...])).astype(sgb_ref.dtype)

    def head_norm(t, gain):
        ssq = _split_dot(t * t, e_ref[...])
        r = lax.rsqrt(ssq * (1.0 / QK_HEAD) + EPS)
        return t * _split_dot(r, et_ref[...]) * gain

    q_lat = _dot(hb, wq_ref[...])
    qc = q_lat * lax.rsqrt(jnp.mean(q_lat * q_lat, axis=-1, keepdims=True) + EPS) * gq_ref[...]
    qq = _dot(qc.astype(BF16), wuq_ref[...])
    hw = N_HEADS * HEAD_BLOCK
    q_pre = qq[:, :hw] * cos8 + qq[:, hw:] * sin8
    q_ref[...] = (head_norm(q_pre, gqh_ref[...]) * (ATTN_SCALE * LOG2E)).astype(q_ref.dtype)

    kv_lat = _dot(hb, wkv_ref[...])
    ckv = kv_lat * lax.rsqrt(jnp.mean(kv_lat * kv_lat, axis=-1, keepdims=True) + EPS) * gkv_ref[...]
    ckv_ref[...] = ckv
    kr = _dot(hb, wkr_ref[...])
    kpe_wide = kr[:, :HEAD_BLOCK] * cos + kr[:, HEAD_BLOCK:] * sin
    kpe_ref[...] = kpe_wide[:, QK_NOPE:QK_NOPE + QK_ROPE]

    if not kv_refs:
        return
    k_ref, vt_ref = kv_refs
    cb = ckv.astype(BF16)
    k_pre = _dot(cb, wuk_ref[...]) + jnp.concatenate([kpe_wide] * N_HEADS, axis=1)
    k_ref[...] = head_norm(k_pre, gkh_ref[...]).astype(k_ref.dtype)
    vt = _dot_nt(wuv_ref[...], cb)
    ones_row = lax.broadcasted_iota(jnp.int32, vt.shape, 0) % VT_ROWS == V_HEAD
    vt = jnp.where(ones_row, 1.0, vt).astype(vt_ref.dtype)
    for c in range(vt_ref.shape[0]):
        vt_ref[c] = vt[:, c * KV_BLOCK:(c + 1) * KV_BLOCK]


def _in_proj(x, cos_t, sin_t, w, tm, with_kv):
    n, d = x.shape
    assert n % tm == 0
    row = lambda width: pl.BlockSpec((tm, width), lambda i: (i, 0))
    consts = [w[k] for k in ("g_attn", "w_u", "w_q", "w_kv", "w_kr", "w_ga", "w_gb", "g_q", "w_uq", "g_kv",
                             "w_uk", "w_uv_t", "g_qh", "g_kh", "e", "et")]
    hw = N_HEADS * HEAD_BLOCK
    out_shape = (jax.ShapeDtypeStruct((n, w["w_u"].shape[1]), F32),
                 jax.ShapeDtypeStruct((n, hw), BF16),
                 jax.ShapeDtypeStruct((n, w["w_kv"].shape[1]), F32),
                 jax.ShapeDtypeStruct((n, QK_ROPE), F32),
                 jax.ShapeDtypeStruct((n, d), BF16),
                 jax.ShapeDtypeStruct((n, d), BF16))
    if with_kv:
        assert tm % KV_BLOCK == 0
        out_shape += (jax.ShapeDtypeStruct((n, hw), BF16),
                      jax.ShapeDtypeStruct((n // KV_BLOCK, N_HEADS * VT_ROWS, KV_BLOCK), BF16))
    return pl.pallas_call(
        _in_proj_kernel,
        grid=(n // tm,),
        in_specs=[row(d), row(LANES), row(LANES)] + [_const_spec(c.shape) for c in consts],
        out_specs=tuple(pl.BlockSpec((tm // KV_BLOCK,) + s.shape[1:], lambda i: (i, 0, 0)) if len(s.shape) == 3
                        else row(s.shape[1]) for s in out_shape),
        out_shape=out_shape,
        compiler_params=_params("parallel"),
        name="in_proj",
    )(x, cos_t, sin_t, *consts)


def _ssm_kernel(u_ref, h0_ref, ar_ref, ai_ref, wbu_ref, wy_ref, d_ref, wglu_ref,
                *rest, chains, steps, emit_y):
    if emit_y:
        a_ref, ht_ref, bu_ref, hc_ref = rest
    else:
        ht_ref, bu_ref, hc_ref = rest
    i = pl.program_id(0)
    ns = ar_ref.shape[1]

    @pl.when(i == 0)
    def _():
        hc_ref[...] = h0_ref[...]

    u = u_ref[...]
    bu_ref[...] = _dot(u.astype(BF16), wbu_ref[...])

    def step(j, carry):
        rows = pl.ds(pl.multiple_of(j * chains, SUBLANES), chains)
        ar, ai = ar_ref[...], ai_ref[...]
        hr, hi = hc_ref[:, :ns], hc_ref[:, ns:]
        nr = ar * hr - ai * hi + bu_ref[rows, :ns]
        ni = ar * hi + ai * hr + bu_ref[rows, ns:]
        hc_ref[:, :ns] = nr
        hc_ref[:, ns:] = ni
        if emit_y:
            bu_ref[rows, :ns] = nr
            bu_ref[rows, ns:] = ni
        return carry

    lax.fori_loop(0, steps, step, 0)
    ht_ref[...] = hc_ref[...]
    if emit_y:
        y = _dot(bu_ref[...].astype(BF16), wy_ref[...]) + d_ref[...] * u
        z = jax.nn.gelu(y)
        a_ref[...] = (z * jax.nn.sigmoid(_dot(z.astype(BF16), wglu_ref[...]))).astype(a_ref.dtype)


def _ssm(u_rows, h0, w, chains, steps, emit_y):
    n, width = u_rows.shape
    rows = chains * steps
    assert n % rows == 0
    ns2 = h0.shape[1]
    consts = [w[k] for k in ("a_re", "a_im", "w_bu", "w_y", "ssm_d", "w_glu")]
    ht_sds = jax.ShapeDtypeStruct((chains, ns2), F32)
    ht_spec = pl.BlockSpec((chains, ns2), lambda i: (0, 0))
    if emit_y:
        out_shape = (jax.ShapeDtypeStruct((n, width), BF16), ht_sds)
        out_specs = (pl.BlockSpec((rows, width), lambda i: (i, 0)), ht_spec)
    else:
        out_shape, out_specs = ht_sds, ht_spec
    return pl.pallas_call(
        functools.partial(_ssm_kernel, chains=chains, steps=steps, emit_y=emit_y),
        grid=(n // rows,),
        in_specs=[pl.BlockSpec((rows, width), lambda i: (i, 0)), _const_spec(h0.shape)]
        + [_const_spec(c.shape) for c in consts],
        out_specs=out_specs,
        out_shape=out_shape,
        scratch_shapes=[pltpu.VMEM((rows, ns2), F32), pltpu.VMEM((chains, ns2), F32)],
        compiler_params=_params("arbitrary"),
        name="ssm_scan" if emit_y else "ssm_end_state",
    )(u_rows, h0, *consts)


def _ssm_combine_kernel(hend_ref, ar_ref, ai_ref, h0_ref, *, seg_len, n_seg):
    ns = ar_ref.shape[1]
    pr, pi = ar_ref[...], ai_ref[...]
    qr, qi = None, None
    e = seg_len
    while e:
        if e & 1:
            qr, qi = (pr, pi) if qr is None else (qr * pr - qi * pi, qr * pi + qi * pr)
        e >>= 1
        if e:
            pr, pi = pr * pr - pi * pi, 2.0 * pr * pi
    er, ei = hend_ref[:, :ns], hend_ref[:, ns:]
    seg = lax.broadcasted_iota(jnp.int32, er.shape, 0) % n_seg
    hr, hi = jnp.zeros_like(er), jnp.zeros_like(ei)
    for s in range(1, n_seg):
        xr = qr * hr - qi * hi + er
        xi = qr * hi + qi * hr + ei
        hr = jnp.where(seg == s, pltpu.roll(xr, 1, 0), hr)
        hi = jnp.where(seg == s, pltpu.roll(xi, 1, 0), hi)
    h0_ref[:, :ns] = hr
    h0_ref[:, ns:] = hi


def _ssm_combine(hend, w, seg_len, n_seg):
    return pl.pallas_call(
        functools.partial(_ssm_combine_kernel, seg_len=seg_len, n_seg=n_seg),
        out_shape=jax.ShapeDtypeStruct(hend.shape, F32),
        name="ssm_combine",
    )(hend, w["a_re"], w["a_im"])


def _flash_kernel(q_ref, k_ref, vt_ref, o_ref, s_buf, p_buf, acc_ref, m_ref, alpha_ref):
    i = pl.program_id(2)
    tq = q_ref.shape[0]
    heads = range(2)
    q = [q_ref[:, hh * HEAD_BLOCK:(hh + 1) * HEAD_BLOCK] for hh in heads]

    def scores(j, slot):
        rows = pl.ds(pl.multiple_of(j * KV_BLOCK, KV_BLOCK), KV_BLOCK)
        for hh in heads:
            s_buf[slot, hh] = _dot_nt(k_ref[rows, hh * HEAD_BLOCK:(hh + 1) * HEAD_BLOCK], q[hh])

    def softmax(slot, hh, masked):
        s = s_buf[slot, hh]
        if masked:
            ki = lax.broadcasted_iota(jnp.int32, s.shape, 0)
            qi = lax.broadcasted_iota(jnp.int32, s.shape, 1)
            s = jnp.where(ki <= qi, s, NEG_INF)
        m = m_ref[hh]
        m_new = jnp.maximum(m, jnp.max(s, axis=0, keepdims=True))
        m_ref[hh] = m_new
        return jnp.exp2(s - m_new).astype(BF16), jnp.exp2(m - m_new)

    def values(j, hh, p, alpha):
        acc_ref[hh] = alpha * acc_ref[hh] + _dot(vt_ref[j, hh * VT_ROWS:(hh + 1) * VT_ROWS, :], p)

    def step(t, slot):
        scores(t + 1, 1 - slot)
        for hh in heads:
            values(jnp.maximum(t - 1, 0), hh, p_buf[1 - slot, hh], alpha_ref[hh])
        for hh in heads:
            p_buf[slot, hh], alpha_ref[hh] = softmax(slot, hh, False)

    scores(0, 0)
    p_buf[1] = jnp.zeros(p_buf.shape[1:], BF16)
    alpha_ref[...] = jnp.ones(alpha_ref.shape, F32)
    m_ref[...] = jnp.full(m_ref.shape, NEG_INF, F32)
    acc_ref[...] = jnp.zeros(acc_ref.shape, F32)

    def two_steps(u, carry):
        step(2 * u, 0)
        step(2 * u + 1, 1)
        return carry

    lax.fori_loop(0, i // 2, two_steps, 0)

    @pl.when(i % 2 == 1)
    def _():
        step(i - 1, 0)

    last = i % 2
    for hh in heads:
        values(jnp.maximum(i - 1, 0), hh, p_buf[1 - last, hh], alpha_ref[hh])
    for hh in heads:
        values(i, hh, *softmax(last, hh, True))
    out = [acc_ref[hh, :V_HEAD] / acc_ref[hh, V_HEAD:V_HEAD + 1] for hh in heads]
    o_ref[...] = jnp.concatenate(out, axis=0).T.astype(o_ref.dtype)


def _flash(q, k, vt, batch, seq):
    tq = KV_BLOCK
    nq = seq // tq
    pairs = N_HEADS // 2
    return pl.pallas_call(
        _flash_kernel,
        grid=(batch, pairs, nq),
        in_specs=[pl.BlockSpec((tq, 2 * HEAD_BLOCK), lambda b, p, i: (b * nq + i, p)),
                  pl.BlockSpec((seq, 2 * HEAD_BLOCK), lambda b, p, i: (b, p)),
                  pl.BlockSpec((nq, 2 * VT_ROWS, KV_BLOCK), lambda b, p, i: (b, p, 0))],
        out_specs=pl.BlockSpec((tq, 2 * V_HEAD), lambda b, p, i: (b * nq + i, p)),
        out_shape=jax.ShapeDtypeStruct((batch * seq, N_HEADS * V_HEAD), BF16),
        scratch_shapes=[pltpu.VMEM((2, 2, KV_BLOCK, tq), F32), pltpu.VMEM((2, 2, KV_BLOCK, tq), BF16),
                        pltpu.VMEM((2, VT_ROWS, tq), F32), pltpu.VMEM((2, 1, tq), F32),
                        pltpu.VMEM((2, 1, tq), F32)],
        compiler_params=_params("parallel", "parallel", "arbitrary"),
        name="flash",
    )(q, k, vt)


def _absorb_kernel(q_ref, gk_ref, wabs_ref, sel_ref, qabs_ref, qr_ref):
    qg = (q_ref[...].astype(F32) * gk_ref[...]).astype(BF16)
    qabs_ref[...] = _dot(qg, wabs_ref[...]).astype(qabs_ref.dtype)
    qr_ref[...] = _dot(qg, sel_ref[...]).astype(qr_ref.dtype)


def _absorb(q, w):
    n = q.shape[0]
    return pl.pallas_call(
        _absorb_kernel,
        out_shape=(jax.ShapeDtypeStruct((n, w["w_abs"].shape[1]), BF16),
                   jax.ShapeDtypeStruct((n, w["sel_rope"].shape[1]), BF16)),
        compiler_params=_params(),
        name="absorb",
    )(q, w["g_kh"], w["w_abs"], w["sel_rope"])


def _paged_attn_kernel(pt_ref, qabs_ref, qr_ref, ckvn_ref, kpen_ref, wukt_ref, ckv_hbm, kpe_hbm,
                       o_ref, ckv_buf, kpe_buf, lhs_ref, m_ref, l_ref, acc_ref, sem_c, sem_k,
                       *, layer, n_chunks, chunk_pages, pages_per_seq, n_new):
    s = pl.program_id(0)
    total = pl.num_programs(0)
    c = s % n_chunks
    slot = s % 2
    kv_lora = wukt_ref.shape[1]
    n_k = wukt_ref.shape[0]
    rows_q = qabs_ref.shape[1]

    def page_copies(step, slt):
        seq, chunk = step // n_chunks, step % n_chunks
        base = seq * pages_per_seq + chunk * chunk_pages
        out = []
        for p in range(chunk_pages):
            page = pt_ref[base + p]
            dst = pl.ds(p * PAGE_SIZE, PAGE_SIZE)
            out.append(pltpu.make_async_copy(ckv_hbm.at[layer, page], ckv_buf.at[slt, dst, :], sem_c.at[slt]))
            out.append(pltpu.make_async_copy(kpe_hbm.at[layer, page], kpe_buf.at[slt, dst, :], sem_k.at[slt]))
        return out

    @pl.when(s == 0)
    def _():
        lhs_ref[:n_k, :] = wukt_ref[...]
        for cp in page_copies(s, slot):
            cp.start()

    @pl.when(s + 1 < total)
    def _():
        for cp in page_copies(s + 1, 1 - slot):
            cp.start()

    @pl.when(c == 0)
    def _():
        lhs_ref[n_k:, :] = qabs_ref[0]
        m_ref[...] = jnp.full(m_ref.shape, NEG_INF, F32)
        l_ref[...] = jnp.zeros(l_ref.shape, F32)
        acc_ref[...] = jnp.zeros(acc_ref.shape, F32)

    for cp in page_copies(s, slot):
        cp.wait()

    qr = qr_ref[0][:, :QK_ROPE]
    ones = jnp.ones((SUBLANES, QK_ROPE), BF16)

    def attend(ckv, kpe, mask):
        tk = ckv.shape[0]
        cb = ckv.astype(BF16)
        both = _dot_nt(lhs_ref[...], cb)
        kt = both[:n_k]
        ssq = jnp.sum((kt * kt).reshape(N_HEADS, QK_NOPE, tk), axis=1)
        kp2 = kpe * kpe
        kp2_hi = kp2.astype(BF16)
        kp2_lo = (kp2 - kp2_hi.astype(F32)).astype(BF16)
        ssq = ssq + _dot_nt(ones, kp2_hi) + _dot_nt(ones, kp2_lo)
        r = lax.rsqrt(ssq * (1.0 / QK_HEAD) + EPS)
        sc = both[n_k:] + _dot_nt(qr, kpe.astype(BF16))
        sc = (sc.reshape(n_new, N_HEADS, tk) * r[None]).reshape(rows_q, tk)
        if mask is not None:
            sc = jnp.where(mask, sc, NEG_INF)
        m_old = m_ref[...]
        m_new = jnp.maximum(m_old, jnp.max(sc, axis=1, keepdims=True))
        p = jnp.exp2(sc - m_new)
        alpha = jnp.exp2(m_old - m_new)
        l_ref[...] = alpha * l_ref[...] + jnp.sum(p, axis=1, keepdims=True)
        acc_ref[...] = alpha * acc_ref[...] + _dot(p.astype(BF16), cb)
        m_ref[...] = m_new

    attend(ckv_buf[slot], kpe_buf[slot], None)

    @pl.when(c == n_chunks - 1)
    def _():
        tkn = ckvn_ref.shape[1]
        key = lax.broadcasted_iota(jnp.int32, (rows_q, tkn), 1)
        tok = lax.broadcasted_iota(jnp.int32, (rows_q, tkn), 0) // N_HEADS
        attend(ckvn_ref[0], kpen_ref[0], key <= tok)
        o_ref[0] = (acc_ref[...] / l_ref[...]).astype(o_ref.dtype)


def _paged_attn(page_table, qabs, qr, ckv_new, kpe_new, w_ukt, cache_ckv, cache_kpe, layer, chunk_pages):
    n_seq, pages_per_seq = page_table.shape
    assert pages_per_seq % chunk_pages == 0
    n_chunks = pages_per_seq // chunk_pages
    rows_q, kv_lora = qabs.shape[1:]
    n_new = rows_q // N_HEADS
    tk = chunk_pages * PAGE_SIZE
    seq_block = lambda shape: pl.BlockSpec((1,) + shape, lambda s, pt: (s // n_chunks, 0, 0))
    grid_spec = pltpu.PrefetchScalarGridSpec(
        num_scalar_prefetch=1,
        grid=(n_seq * n_chunks,),
        in_specs=[seq_block(qabs.shape[1:]), seq_block(qr.shape[1:]), seq_block(ckv_new.shape[1:]),
                  seq_block(kpe_new.shape[1:]),
                  pl.BlockSpec(w_ukt.shape, lambda s, pt: (0, 0)),
                  pl.BlockSpec(memory_space=pl.ANY), pl.BlockSpec(memory_space=pl.ANY)],
        out_specs=seq_block((rows_q, kv_lora)),
        scratch_shapes=[pltpu.VMEM((2, tk, kv_lora), F32), pltpu.VMEM((2, tk, QK_ROPE), F32),
                        pltpu.VMEM((w_ukt.shape[0] + rows_q, kv_lora), BF16),
                        pltpu.VMEM((rows_q, 1), F32), pltpu.VMEM((rows_q, 1), F32),
                        pltpu.VMEM((rows_q, kv_lora), F32),
                        pltpu.SemaphoreType.DMA((2,)), pltpu.SemaphoreType.DMA((2,))])
    return pl.pallas_call(
        functools.partial(_paged_attn_kernel, layer=layer, n_chunks=n_chunks, chunk_pages=chunk_pages,
                          pages_per_seq=pages_per_seq, n_new=n_new),
        grid_spec=grid_spec,
        out_shape=jax.ShapeDtypeStruct((n_seq, rows_q, kv_lora), BF16),
        compiler_params=_params("arbitrary"),
        name="paged_attn",
    )(page_table.reshape(-1), qabs, qr, ckv_new, kpe_new, w_ukt, cache_ckv, cache_kpe)


def _out_proj_kernel(x_ref, a_ref, b_ref, sga_ref, sgb_ref, p_ref, wuvb_ref, wa_ref, wb_ref, wo_ref, gffn_ref,
                     wg_ref, wup_ref, wdn_ref, gple_ref, wpg_ref, wple_ref, y_ref, *, latent_b):
    b = b_ref[...]
    if latent_b:
        b = _dot(b, wuvb_ref[...]).astype(BF16)
    merged = (sga_ref[...].astype(F32) * _dot(a_ref[...], wa_ref[...])
              + sgb_ref[...].astype(F32) * _dot(b, wb_ref[...]))
    x = x_ref[...] + _dot(merged.astype(BF16), wo_ref[...])

    def norm(t, g_ref):
        return (t * lax.rsqrt(jnp.mean(t * t, axis=-1, keepdims=True) + EPS) * g_ref[...]).astype(BF16)

    hf = norm(x, gffn_ref)
    act = jax.nn.silu(_dot(hf, wg_ref[...])) * _dot(hf, wup_ref[...])
    x = x + _dot(act.astype(BF16), wdn_ref[...])
    gate = jax.nn.sigmoid(_dot(norm(x, gple_ref), wpg_ref[...]))
    y_ref[...] = x + gate * _dot(p_ref[...].astype(BF16), wple_ref[...])


def _out_proj(x, a, b, sga, sgb, p, w, tm, latent_b):
    n, d = x.shape
    assert n % tm == 0
    row = lambda width: pl.BlockSpec((tm, width), lambda i: (i, 0))
    consts = [w[k] for k in ("w_uv_bd", "w_a", "w_b", "w_out", "g_ffn", "w_gate", "w_up", "w_down", "g_ple",
                             "w_ple_gate", "w_ple")]
    return pl.pallas_call(
        functools.partial(_out_proj_kernel, latent_b=latent_b),
        grid=(n // tm,),
        in_specs=[row(d), row(a.shape[1]), row(b.shape[1]), row(d), row(d), row(p.shape[1])]
        + [_const_spec(c.shape) for c in consts],
        out_specs=row(d),
        out_shape=jax.ShapeDtypeStruct((n, d), F32),
        compiler_params=_params("parallel"),
        name="out_proj_latent" if latent_b else "out_proj",
    )(x, a, b, sga, sgb, p, *consts)


def _head_blocked(w_nope, w_rope):
    k = w_nope.shape[0] if w_nope is not None else w_rope.shape[0]
    nope = w_nope if w_nope is not None else jnp.zeros((k, N_HEADS, QK_NOPE), F32)
    rope = w_rope if w_rope is not None else jnp.zeros((k, N_HEADS, QK_ROPE), F32)
    pad = jnp.zeros((k, N_HEADS, HEAD_BLOCK - QK_HEAD), F32)
    return jnp.concatenate([nope, rope, pad], axis=-1).reshape(k, N_HEADS * HEAD_BLOCK)


def _rot_cols(w):
    half = QK_ROPE // 2
    return jnp.concatenate([-w[..., half:], w[..., :half]], axis=-1)


def _head_gain(g_nope, g_rope_half):
    g = jnp.concatenate([g_nope, g_rope_half, g_rope_half, jnp.zeros((HEAD_BLOCK - QK_HEAD,), F32)])
    return jnp.tile(g, N_HEADS)[None, :]


def _block_diag(blocks):
    g, r, c = blocks.shape
    eye = jnp.eye(g, dtype=blocks.dtype)
    return jnp.einsum("grc,gh->grhc", blocks, eye).reshape(g * r, g * c)


def _prepare(norm_attn_g, w_in, q_norm_g, w_uq, kv_norm_g, w_uk, w_uv, q_norm_nope_g, q_norm_rope_g,
             k_norm_nope_g, k_norm_rope_g, disc, ssm_c_re, ssm_c_im, ssm_d, w_glu, w_branch_a, w_branch_b,
             w_out, norm_ffn_g, w_gate, w_up, w_down, norm_ple_g, w_ple_gate, w_ple):
    d_model = w_in.shape[0]
    ssm_w = ssm_d.shape[0]
    q_lora, kv_lora = w_uq.shape[0], w_uk.shape[0]
    o = np.cumsum([0, ssm_w, q_lora, kv_lora, QK_ROPE, d_model, d_model])
    bf = lambda a: a.astype(BF16)
    w = {}
    w["g_attn"] = norm_attn_g[None, :]
    w["w_u"] = bf(w_in[:, o[0]:o[1]])
    w["w_q"] = bf(w_in[:, o[1]:o[2]])
    w["w_kv"] = bf(w_in[:, o[2]:o[3]])
    w_kr = w_in[:, o[3]:o[4]]
    place = lambda r: jnp.pad(r, ((0, 0), (QK_NOPE, HEAD_BLOCK - QK_HEAD)))
    w["w_kr"] = bf(jnp.concatenate([place(w_kr), place(_rot_cols(w_kr))], axis=1))
    w["w_ga"] = bf(w_in[:, o[4]:o[5]])
    w["w_gb"] = bf(w_in[:, o[5]:o[6]])
    w["g_q"] = q_norm_g[None, :]
    uq_nope, uq_rope = w_uq[..., :QK_NOPE], w_uq[..., QK_NOPE:]
    w["w_uq"] = bf(jnp.concatenate([_head_blocked(uq_nope, uq_rope), _head_blocked(None, _rot_cols(uq_rope))], axis=1))
    w["g_kv"] = kv_norm_g[None, :]
    w["w_uk"] = bf(_head_blocked(w_uk, None))
    uv_t = jnp.pad(jnp.transpose(w_uv, (1, 2, 0)), ((0, 0), (0, VT_ROWS - V_HEAD), (0, 0)))
    w["w_uv_t"] = bf(uv_t.reshape(N_HEADS * VT_ROWS, kv_lora))
    w["g_qh"] = _head_gain(q_norm_nope_g, q_norm_rope_g)
    w["g_kh"] = _head_gain(k_norm_nope_g, k_norm_rope_g)
    e = np.zeros((N_HEADS * HEAD_BLOCK, LANES), np.float32)
    for h in range(N_HEADS):
        e[h * HEAD_BLOCK:h * HEAD_BLOCK + QK_HEAD, h] = 1.0
    w["e"] = jnp.asarray(e, BF16)
    w["et"] = jnp.asarray(e.T, BF16)
    ukt = jnp.transpose(w_uk, (1, 2, 0))
    ukt = jnp.pad(ukt, ((0, 0), (0, HEAD_BLOCK - QK_NOPE), (0, 0)))
    w["w_abs"] = bf(_block_diag(ukt))
    sel = np.zeros((N_HEADS * HEAD_BLOCK, N_HEADS * HEAD_BLOCK), np.float32)
    for h in range(N_HEADS):
        for dd in range(QK_ROPE):
            sel[h * HEAD_BLOCK + QK_NOPE + dd, h * HEAD_BLOCK + dd] = 1.0
    w["sel_rope"] = jnp.asarray(sel, BF16)
    w["w_ukt"] = bf(jnp.transpose(w_uk, (1, 2, 0)).reshape(N_HEADS * QK_NOPE, kv_lora))
    w["w_uv_bd"] = bf(_block_diag(jnp.transpose(w_uv, (1, 0, 2))))
    lb_re, lb_im, bb_re, bb_im = disc
    w["a_re"] = lb_re.reshape(1, -1)
    w["a_im"] = lb_im.reshape(1, -1)
    tr = lambda a: jnp.transpose(a, (0, 2, 1))
    w["w_bu"] = bf(jnp.concatenate([_block_diag(tr(bb_re)), _block_diag(tr(bb_im))], axis=1))
    w["w_y"] = bf(jnp.concatenate([_block_diag(tr(ssm_c_re)), -_block_diag(tr(ssm_c_im))], axis=0))
    w["ssm_d"] = ssm_d[None, :]
    w["w_glu"] = bf(w_glu)
    w["w_a"], w["w_b"], w["w_out"] = bf(w_branch_a), bf(w_branch_b), bf(w_out)
    w["g_ffn"], w["g_ple"] = norm_ffn_g[None, :], norm_ple_g[None, :]
    w["w_gate"], w["w_up"], w["w_down"] = bf(w_gate), bf(w_up), bf(w_down)
    w["w_ple_gate"], w["w_ple"] = bf(w_ple_gate), bf(w_ple)
    return w


def _rope_tables(pos):
    half = QK_ROPE // 2
    inv = ROPE_THETA ** (-jnp.arange(half, dtype=F32) / half)
    ang = pos.astype(F32)[:, None] * inv[None, :]
    cos, sin = jnp.cos(ang), jnp.sin(ang)
    n = pos.shape[0]
    ones, zeros = jnp.ones((n, QK_NOPE), F32), jnp.zeros((n, QK_NOPE), F32)
    pad = jnp.zeros((n, HEAD_BLOCK - QK_HEAD), F32)
    return (jnp.concatenate([ones, cos, cos, pad], axis=1), jnp.concatenate([zeros, sin, sin, pad], axis=1))


def _pick(n, prefs):
    for t in prefs:
        if n % t == 0:
            return t
    return n


def kernel(x_prompt, x_sample, p_prompt, p_sample, cache_ckv, cache_kpe, state_ssm_re, state_ssm_im, page_table, norm_attn_g, w_in, q_norm_g, w_uq, kv_norm_g, w_uk, w_uv, q_norm_nope_g, q_norm_rope_g, k_norm_nope_g, k_norm_rope_g, ssm_lam_re, ssm_lam_im, ssm_log_dt, ssm_b_re, ssm_b_im, ssm_c_re, ssm_c_im, ssm_d, w_glu, w_branch_a, w_branch_b, w_out, norm_ffn_g, w_gate, w_up, w_down, norm_ple_g, w_ple_gate, w_ple):
    depth = w_in.shape[0]
    batch, seq, d_model = x_prompt.shape
    n_seq, n_new, _ = x_sample.shape
    past = page_table.shape[1] * PAGE_SIZE
    groups, n_state = ssm_lam_re.shape[1:]
    ns = groups * n_state

    cos_p, sin_p = _rope_tables(jnp.arange(seq))
    cos_p, sin_p = jnp.tile(cos_p, (batch, 1)), jnp.tile(sin_p, (batch, 1))
    cos_s, sin_s = _rope_tables(past + jnp.arange(n_new))
    cos_s, sin_s = jnp.tile(cos_s, (n_seq, 1)), jnp.tile(sin_s, (n_seq, 1))

    n_seg = SUBLANES
    assert seq % n_seg == 0
    seg_len = seq // n_seg
    chains_p = batch * n_seg
    steps_p = _pick(seg_len, (16, 8, 4, 2))
    steps_s = _pick(n_new, (4, 2))
    tm_p = _pick(batch * seq, (512, 256, 128))
    tm_s = _pick(n_seq * n_new, (512, 256, 128))
    assert seq % KV_BLOCK == 0
    chunk_pages = _pick(page_table.shape[1], (16, 8, 4, 2))

    xp = x_prompt.reshape(batch * seq, d_model)
    xs = x_sample.reshape(n_seq * n_new, d_model)
    outs = {k: [] for k in ("ckv_p", "kpe_p", "hre_p", "him_p", "ckv_s", "kpe_s", "hre_s", "him_s")}
    for i in range(depth):
        disc = _discretize(ssm_lam_re[i], ssm_lam_im[i], ssm_log_dt[i], ssm_b_re[i], ssm_b_im[i])
        w = _prepare(norm_attn_g[i], w_in[i], q_norm_g[i], w_uq[i], kv_norm_g[i], w_uk[i], w_uv[i],
                     q_norm_nope_g[i], q_norm_rope_g[i], k_norm_nope_g[i], k_norm_rope_g[i], disc,
                     ssm_c_re[i], ssm_c_im[i], ssm_d[i], w_glu[i], w_branch_a[i], w_branch_b[i], w_out[i],
                     norm_ffn_g[i], w_gate[i], w_up[i], w_down[i], norm_ple_g[i], w_ple_gate[i], w_ple[i])
        ssm_w = w["w_u"].shape[1]

        u, q, ckv, kpe, sga, sgb, k, vt = _in_proj(xp, cos_p, sin_p, w, tm_p, with_kv=True)
        u_rows = u.reshape(batch, n_seg, seg_len, ssm_w).transpose(2, 0, 1, 3).reshape(batch * seq, ssm_w)
        zero_state = jnp.zeros((chains_p, 2 * ns), F32)
        h_end = _ssm(u_rows, zero_state, w, chains_p, steps_p, emit_y=False)
        h_start = _ssm_combine(h_end, w, seg_len, n_seg)
        a_rows, h_fin = _ssm(u_rows, h_start, w, chains_p, steps_p, emit_y=True)
        a_out = a_rows.reshape(seg_len, batch, n_seg, ssm_w).transpose(1, 2, 0, 3).reshape(batch * seq, ssm_w)
        b_out = _flash(q, k, vt, batch, seq)
        xp = _out_proj(xp, a_out, b_out, sga, sgb, p_prompt[i].reshape(batch * seq, -1), w, tm_p, latent_b=False)
        h_fin = h_fin.reshape(batch, n_seg, 2 * ns)[:, -1]
        outs["ckv_p"].append(ckv.reshape(batch, seq, -1))
        outs["kpe_p"].append(kpe.reshape(batch, seq, -1))
        outs["hre_p"].append(h_fin[:, :ns].reshape(batch, groups, n_state))
        outs["him_p"].append(h_fin[:, ns:].reshape(batch, groups, n_state))

        u, q, ckv, kpe, sga, sgb = _in_proj(xs, cos_s, sin_s, w, tm_s, with_kv=False)
        u_rows = u.reshape(n_seq, n_new, ssm_w).transpose(1, 0, 2).reshape(n_seq * n_new, ssm_w)
        h0 = jnp.concatenate([state_ssm_re[i].reshape(n_seq, ns), state_ssm_im[i].reshape(n_seq, ns)], axis=1)
        a_rows, h_fin = _ssm(u_rows, h0, w, n_seq, steps_s, emit_y=True)
        a_out = a_rows.reshape(n_new, n_seq, ssm_w).transpose(1, 0, 2).reshape(n_seq * n_new, ssm_w)
        qabs, qr = _absorb(q, w)
        rows_q = n_new * N_HEADS
        kv_lora = ckv.shape[1]
        pad_new = PAGE_SIZE - n_new
        ckv_new = jnp.pad(ckv.reshape(n_seq, n_new, kv_lora), ((0, 0), (0, pad_new), (0, 0)))
        kpe_new = jnp.pad(kpe.reshape(n_seq, n_new, QK_ROPE), ((0, 0), (0, pad_new), (0, 0)))
        o_lat = _paged_attn(page_table, qabs.reshape(n_seq, rows_q, kv_lora), qr.reshape(n_seq, rows_q, HEAD_BLOCK),
                            ckv_new, kpe_new, w["w_ukt"], cache_ckv, cache_kpe, i, chunk_pages)
        b_lat = o_lat.reshape(n_seq * n_new, N_HEADS * kv_lora)
        xs = _out_proj(xs, a_out, b_lat, sga, sgb, p_sample[i].reshape(n_seq * n_new, -1), w, tm_s, latent_b=True)
        outs["ckv_s"].append(ckv.reshape(n_seq, n_new, -1))
        outs["kpe_s"].append(kpe.reshape(n_seq, n_new, -1))
        outs["hre_s"].append(h_fin[:, :ns].reshape(n_seq, groups, n_state))
        outs["him_s"].append(h_fin[:, ns:].reshape(n_seq, groups, n_state))

    st = lambda key: jnp.stack(outs[key])
    return (xp.reshape(batch, seq, d_model), xs.reshape(n_seq, n_new, d_model),
            st("ckv_p"), st("kpe_p"), st("hre_p"), st("him_p"),
            st("ckv_s"), st("kpe_s"), st("hre_s"), st("him_s"))
```

```python
import functools
import math

import numpy as np
import jax
import jax.numpy as jnp
from jax import lax
from jax.experimental import pallas as pl
from jax.experimental.pallas import tpu as pltpu

N_HEADS = 8
QK_NOPE = 64
QK_ROPE = 32
QK_HEAD = QK_NOPE + QK_ROPE
V_HEAD = 64
VT_ROWS = 80
HEAD_BLOCK = 128
ROPE_THETA = 10000.0
ATTN_SCALE = QK_HEAD ** -0.5
LOG2E = math.log2(math.e)
SSM_GROUP = 16
SSM_STATE = 64
PAGE_SIZE = 128
KV_BLOCK = 256
PART_PAGES = 8
EPS = 1e-6

LANES = 128
SUBLANES = 8
VMEM_LIMIT_BYTES = 56 * 1024 * 1024

BF16 = jnp.bfloat16
F32 = jnp.float32
NEG_INF = float("-inf")


def _dot(a, b):
    return jnp.dot(a, b, preferred_element_type=F32)


def _dot_nt(a, b):
    return lax.dot_general(a, b, (((1,), (1,)), ((), ())), preferred_element_type=F32)


def _split_dot(x, w):
    hi = x.astype(BF16)
    lo = (x - hi.astype(F32)).astype(BF16)
    return _dot(hi, w) + _dot(lo, w)


def _const_spec(shape):
    nd = len(shape)
    return pl.BlockSpec(shape, lambda *_: (0,) * nd, pipeline_mode=pl.Buffered(1))


def _params(*sem):
    return pltpu.CompilerParams(dimension_semantics=sem, vmem_limit_bytes=VMEM_LIMIT_BYTES)


def _discretize_kernel(lr_ref, li_ref, ldt_ref, br_ref, bi_ref, lbr_ref, lbi_ref, bbr_ref, bbi_ref):
    lr, li = lr_ref[...], li_ref[...]
    dt = jnp.exp(ldt_ref[...])
    mag = jnp.exp(lr * dt)
    lb_re, lb_im = mag * jnp.cos(li * dt), mag * jnp.sin(li * dt)
    num_re, num_im = lb_re - 1.0, lb_im
    den = lr * lr + li * li
    f_re = (num_re * lr + num_im * li) / den
    f_im = (num_im * lr - num_re * li) / den
    br, bi = br_ref[...], bi_ref[...]
    lbr_ref[...] = lb_re
    lbi_ref[...] = lb_im
    bbr_ref[...] = f_re * br - f_im * bi
    bbi_ref[...] = f_re * bi + f_im * br


def _discretize(lam_re, lam_im, log_dt, b_re, b_im):
    g, n, c = b_re.shape
    rep = lambda a: jnp.repeat(a, c, axis=-1)
    args = (rep(lam_re), rep(lam_im), jnp.broadcast_to(log_dt[:, None], (g, n * c)),
            b_re.reshape(g, n * c), b_im.reshape(g, n * c))
    sds = jax.ShapeDtypeStruct((g, n * c), F32)
    lbr, lbi, bbr, bbi = pl.pallas_call(_discretize_kernel, out_shape=(sds,) * 4, name="discretize")(*args)
    lb_re = lbr.reshape(g, n, c)[..., 0]
    lb_im = lbi.reshape(g, n, c)[..., 0]
    return lb_re, lb_im, bbr.reshape(g, n, c), bbi.reshape(g, n, c)


def _in_proj_kernel(x_ref, cos_ref, sin_ref, gattn_ref, wu_ref, wq_ref, wkv_ref, wkr_ref, wga_ref, wgb_ref,
                    gq_ref, wuq_ref, gkv_ref, wuk_ref, wuv_ref, gqh_ref, gkh_ref, e_ref, et_ref,
                    u_ref, q_ref, ckv_ref, kpe_ref, sga_ref, sgb_ref, *kv_refs):
    x = x_ref[...]
    h = x * lax.rsqrt(jnp.mean(x * x, axis=-1, keepdims=True) + EPS) * gattn_ref[...]
    hb = h.astype(BF16)
    cos, sin = cos_ref[...], sin_ref[...]
    cos8 = jnp.concatenate([cos] * N_HEADS, axis=1)
    sin8 = jnp.concatenate([sin] * N_HEADS, axis=1)

    u_ref[...] = _dot(hb, wu_ref[...])
    sga_ref[...] = jax.nn.sigmoid(_dot(hb, wga_ref[...])).astype(sga_ref.dtype)
    sgb_ref[...] = jax.nn.sigmoid(_dot(hb, wgb_ref[...])).astype(sgb_ref.dtype)

    def head_norm(t, gain):
        ssq = _split_dot(t * t, e_ref[...])
        r = lax.rsqrt(ssq * (1.0 / QK_HEAD) + EPS)
        return t * _split_dot(r, et_ref[...]) * gain

    q_lat = _dot(hb, wq_ref[...])
    qc = q_lat * lax.rsqrt(jnp.mean(q_lat * q_lat, axis=-1, keepdims=True) + EPS) * gq_ref[...]
    qq = _dot(qc.astype(BF16), wuq_ref[...])
    hw = N_HEADS * HEAD_BLOCK
    q_pre = qq[:, :hw] * cos8 + qq[:, hw:] * sin8
    q_ref[...] = (head_norm(q_pre, gqh_ref[...]) * (ATTN_SCALE * LOG2E)).astype(q_ref.dtype)

    kv_lat = _dot(hb, wkv_ref[...])
    ckv = kv_lat * lax.rsqrt(jnp.mean(kv_lat * kv_lat, axis=-1, keepdims=True) + EPS) * gkv_ref[...]
    ckv_ref[...] = ckv
    kr = _dot(hb, wkr_ref[...])
    kpe_wide = kr[:, :HEAD_BLOCK] * cos + kr[:, HEAD_BLOCK:] * sin
    kpe_ref[...] = kpe_wide[:, QK_NOPE:QK_NOPE + QK_ROPE]

    if not kv_refs:
        return
    k_ref, vt_ref = kv_refs
    cb = ckv.astype(BF16)
    k_pre = _dot(cb, wuk_ref[...]) + jnp.concatenate([kpe_wide] * N_HEADS, axis=1)
    k_ref[...] = head_norm(k_pre, gkh_ref[...]).astype(k_ref.dtype)
    vt = _dot_nt(wuv_ref[...], cb)
    ones_row = lax.broadcasted_iota(jnp.int32, vt.shape, 0) % VT_ROWS == V_HEAD
    vt = jnp.where(ones_row, 1.0, vt).astype(vt_ref.dtype)
    for c in range(vt_ref.shape[0]):
        vt_ref[c] = vt[:, c * KV_BLOCK:(c + 1) * KV_BLOCK]


def _in_proj(x, cos_t, sin_t, w, tm, with_kv):
    n, d = x.shape
    assert n % tm == 0
    row = lambda width: pl.BlockSpec((tm, width), lambda i: (i, 0))
    consts = [w[k] for k in ("g_attn", "w_u", "w_q", "w_kv", "w_kr", "w_ga", "w_gb", "g_q", "w_uq", "g_kv",
                             "w_uk", "w_uv_t", "g_qh", "g_kh", "e", "et")]
    hw = N_HEADS * HEAD_BLOCK
    out_shape = (jax.ShapeDtypeStruct((n, w["w_u"].shape[1]), F32),
                 jax.ShapeDtypeStruct((n, hw), BF16),
                 jax.ShapeDtypeStruct((n, w["w_kv"].shape[1]), F32),
                 jax.ShapeDtypeStruct((n, QK_ROPE), F32),
                 jax.ShapeDtypeStruct((n, d), BF16),
                 jax.ShapeDtypeStruct((n, d), BF16))
    if with_kv:
        assert tm % KV_BLOCK == 0
        out_shape += (jax.ShapeDtypeStruct((n, hw), BF16),
                      jax.ShapeDtypeStruct((n // KV_BLOCK, N_HEADS * VT_ROWS, KV_BLOCK), BF16))
    return pl.pallas_call(
        _in_proj_kernel,
        grid=(n // tm,),
        in_specs=[row(d), row(LANES), row(LANES)] + [_const_spec(c.shape) for c in consts],
        out_specs=tuple(pl.BlockSpec((tm // KV_BLOCK,) + s.shape[1:], lambda i: (i, 0, 0)) if len(s.shape) == 3
                        else row(s.shape[1]) for s in out_shape),
        out_shape=out_shape,
        compiler_params=_params("parallel"),
        name="in_proj",
    )(x, cos_t, sin_t, *consts)


def _ssm_kernel(u_ref, h0_ref, ar_ref, ai_ref, wbu_ref, wy_ref, d_ref, wglu_ref,
                *rest, chains, steps, emit_y):
    if emit_y:
        a_ref, ht_ref, bu_ref, hc_ref = rest
    else:
        ht_ref, bu_ref, hc_ref = rest
    i = pl.program_id(0)
    ns = ar_ref.shape[1]

    @pl.when(i == 0)
    def _():
        hc_ref[...] = h0_ref[...]

    u = u_ref[...]
    bu_ref[...] = _dot(u.astype(BF16), wbu_ref[...])

    def step(j, carry):
        rows = pl.ds(pl.multiple_of(j * chains, SUBLANES), chains)
        ar, ai = ar_ref[...], ai_ref[...]
        hr, hi = hc_ref[:, :ns], hc_ref[:, ns:]
        nr = ar * hr - ai * hi + bu_ref[rows, :ns]
        ni = ar * hi + ai * hr + bu_ref[rows, ns:]
        hc_ref[:, :ns] = nr
        hc_ref[:, ns:] = ni
        if emit_y:
            bu_ref[rows, :ns] = nr
            bu_ref[rows, ns:] = ni
        return carry

    lax.fori_loop(0, steps, step, 0)
    ht_ref[...] = hc_ref[...]
    if emit_y:
        y = _dot(bu_ref[...].astype(BF16), wy_ref[...]) + d_ref[...] * u
        z = jax.nn.gelu(y)
        a_ref[...] = (z * jax.nn.sigmoid(_dot(z.astype(BF16), wglu_ref[...]))).astype(a_ref.dtype)


def _ssm(u_rows, h0, w, chains, steps, emit_y):
    n, width = u_rows.shape
    rows = chains * steps
    assert n % rows == 0
    ns2 = h0.shape[1]
    consts = [w[k] for k in ("a_re", "a_im", "w_bu", "w_y", "ssm_d", "w_glu")]
    ht_sds = jax.ShapeDtypeStruct((chains, ns2), F32)
    ht_spec = pl.BlockSpec((chains, ns2), lambda i: (0, 0))
    if emit_y:
        out_shape = (jax.ShapeDtypeStruct((n, width), BF16), ht_sds)
        out_specs = (pl.BlockSpec((rows, width), lambda i: (i, 0)), ht_spec)
    else:
        out_shape, out_specs = ht_sds, ht_spec
    return pl.pallas_call(
        functools.partial(_ssm_kernel, chains=chains, steps=steps, emit_y=emit_y),
        grid=(n // rows,),
        in_specs=[pl.BlockSpec((rows, width), lambda i: (i, 0)), _const_spec(h0.shape)]
        + [_const_spec(c.shape) for c in consts],
        out_specs=out_specs,
        out_shape=out_shape,
        scratch_shapes=[pltpu.VMEM((rows, ns2), F32), pltpu.VMEM((chains, ns2), F32)],
        compiler_params=_params("arbitrary"),
        name="ssm_scan" if emit_y else "ssm_end_state",
    )(u_rows, h0, *consts)


def _ssm_combine_kernel(hend_ref, ar_ref, ai_ref, h0_ref, *, seg_len, n_seg):
    ns = ar_ref.shape[1]
    pr, pi = ar_ref[...], ai_ref[...]
    qr, qi = None, None
    e = seg_len
    while e:
        if e & 1:
            qr, qi = (pr, pi) if qr is None else (qr * pr - qi * pi, qr * pi + qi * pr)
        e >>= 1
        if e:
            pr, pi = pr * pr - pi * pi, 2.0 * pr * pi
    er, ei = hend_ref[:, :ns], hend_ref[:, ns:]
    seg = lax.broadcasted_iota(jnp.int32, er.shape, 0) % n_seg
    hr, hi = jnp.zeros_like(er), jnp.zeros_like(ei)
    for s in range(1, n_seg):
        xr = qr * hr - qi * hi + er
        xi = qr * hi + qi * hr + ei
        hr = jnp.where(seg == s, pltpu.roll(xr, 1, 0), hr)
        hi = jnp.where(seg == s, pltpu.roll(xi, 1, 0), hi)
    h0_ref[:, :ns] = hr
    h0_ref[:, ns:] = hi


def _ssm_combine(hend, w, seg_len, n_seg):
    return pl.pallas_call(
        functools.partial(_ssm_combine_kernel, seg_len=seg_len, n_seg=n_seg),
        out_shape=jax.ShapeDtypeStruct(hend.shape, F32),
        name="ssm_combine",
    )(hend, w["a_re"], w["a_im"])


def _flash_kernel(q_ref, k_ref, vt_ref, o_ref, s_buf, p_buf, acc_ref, m_ref, alpha_ref):
    i = pl.program_id(2)
    tq = q_ref.shape[0]
    heads = range(2)
    q = [q_ref[:, hh * HEAD_BLOCK:(hh + 1) * HEAD_BLOCK] for hh in heads]

    def scores(j, slot):
        rows = pl.ds(pl.multiple_of(j * KV_BLOCK, KV_BLOCK), KV_BLOCK)
        for hh in heads:
            s_buf[slot, hh] = _dot_nt(k_ref[rows, hh * HEAD_BLOCK:(hh + 1) * HEAD_BLOCK], q[hh])

    def softmax(slot, hh, masked):
        s = s_buf[slot, hh]
        if masked:
            ki = lax.broadcasted_iota(jnp.int32, s.shape, 0)
            qi = lax.broadcasted_iota(jnp.int32, s.shape, 1)
            s = jnp.where(ki <= qi, s, NEG_INF)
        m = m_ref[hh]
        m_new = jnp.maximum(m, jnp.max(s, axis=0, keepdims=True))
        m_ref[hh] = m_new
        return jnp.exp2(s - m_new).astype(BF16), jnp.exp2(m - m_new)

    def values(j, hh, p, alpha):
        acc_ref[hh] = alpha * acc_ref[hh] + _dot(vt_ref[j, hh * VT_ROWS:(hh + 1) * VT_ROWS, :], p)

    def step(t, slot):
        scores(t + 1, 1 - slot)
        for hh in heads:
            values(jnp.maximum(t - 1, 0), hh, p_buf[1 - slot, hh], alpha_ref[hh])
        for hh in heads:
            p_buf[slot, hh], alpha_ref[hh] = softmax(slot, hh, False)

    scores(0, 0)
    p_buf[1] = jnp.zeros(p_buf.shape[1:], BF16)
    alpha_ref[...] = jnp.ones(alpha_ref.shape, F32)
    m_ref[...] = jnp.full(m_ref.shape, NEG_INF, F32)
    acc_ref[...] = jnp.zeros(acc_ref.shape, F32)

    def two_steps(u, carry):
        step(2 * u, 0)
        step(2 * u + 1, 1)
        return carry

    lax.fori_loop(0, i // 2, two_steps, 0)

    @pl.when(i % 2 == 1)
    def _():
        step(i - 1, 0)

    last = i % 2
    for hh in heads:
        values(jnp.maximum(i - 1, 0), hh, p_buf[1 - last, hh], alpha_ref[hh])
    for hh in heads:
        values(i, hh, *softmax(last, hh, True))
    out = [acc_ref[hh, :V_HEAD] / acc_ref[hh, V_HEAD:V_HEAD + 1] for hh in heads]
    o_ref[...] = jnp.concatenate(out, axis=0).T.astype(o_ref.dtype)


def _flash(q, k, vt, batch, seq):
    tq = KV_BLOCK
    nq = seq // tq
    pairs = N_HEADS // 2
    return pl.pallas_call(
        _flash_kernel,
        grid=(batch, pairs, nq),
        in_specs=[pl.BlockSpec((tq, 2 * HEAD_BLOCK), lambda b, p, i: (b * nq + i, p)),
                  pl.BlockSpec((seq, 2 * HEAD_BLOCK), lambda b, p, i: (b, p)),
                  pl.BlockSpec((nq, 2 * VT_ROWS, KV_BLOCK), lambda b, p, i: (b, p, 0))],
        out_specs=pl.BlockSpec((tq, 2 * V_HEAD), lambda b, p, i: (b * nq + i, p)),
        out_shape=jax.ShapeDtypeStruct((batch * seq, N_HEADS * V_HEAD), BF16),
        scratch_shapes=[pltpu.VMEM((2, 2, KV_BLOCK, tq), F32), pltpu.VMEM((2, 2, KV_BLOCK, tq), BF16),
                        pltpu.VMEM((2, VT_ROWS, tq), F32), pltpu.VMEM((2, 1, tq), F32),
                        pltpu.VMEM((2, 1, tq), F32)],
        compiler_params=_params("parallel", "parallel", "arbitrary"),
        name="flash",
    )(q, k, vt)


def _absorb_kernel(q_ref, gk_ref, wabs_ref, sel_ref, qabs_ref, qr_ref):
    qg = (q_ref[...].astype(F32) * gk_ref[...]).astype(BF16)
    qabs_ref[...] = _dot(qg, wabs_ref[...]).astype(qabs_ref.dtype)
    qr_ref[...] = _dot(qg, sel_ref[...]).astype(qr_ref.dtype)


def _absorb(q, w):
    n = q.shape[0]
    return pl.pallas_call(
        _absorb_kernel,
        out_shape=(jax.ShapeDtypeStruct((n, w["w_abs"].shape[1]), BF16),
                   jax.ShapeDtypeStruct((n, w["sel_rope"].shape[1]), BF16)),
        compiler_params=_params(),
        name="absorb",
    )(q, w["g_kh"], w["w_abs"], w["sel_rope"])


def _paged_attn_kernel(pt_ref, qabs_ref, qr_ref, qabs_nx_ref, qr_nx_ref, ckvn_ref, kpetn_ref, wukt_ref,
                       ckv_hbm, kpet_hbm, o_ref, ckv_buf, kpe_buf, cb_buf, sc_buf, lhs_ref, m_ref, l_ref, acc_ref,
                       sem_c, sem_k, *, layer, n_chunks, chunk_pages, n_new):
    b = pl.program_id(0)
    total = pl.num_programs(0) * n_chunks
    n_k = wukt_ref.shape[0]
    rows_q = qabs_ref.shape[1]
    tk = chunk_pages * PAGE_SIZE
    parts = chunk_pages // PART_PAGES
    qr = (qr_ref[0][:, :QK_ROPE], qr_nx_ref[0][:, :QK_ROPE])

    def page_copies(g, slt):
        out = []
        for p in range(chunk_pages):
            page = pt_ref[g * chunk_pages + p]
            dst = pl.ds(p * PAGE_SIZE, PAGE_SIZE)
            out.append(pltpu.make_async_copy(ckv_hbm.at[layer, page], ckv_buf.at[slt, dst, :], sem_c.at[slt]))
            out.append(pltpu.make_async_copy(kpet_hbm.at[layer, page], kpe_buf.at[slt, p], sem_k.at[slt]))
        return out

    def scaled_scores(which, cb, kpet):
        keys = cb.shape[0]
        both = _dot_nt(lhs_ref[which], cb)
        kt = both[:n_k]
        ssq = jnp.sum((kt * kt).reshape(QK_NOPE, N_HEADS, keys), axis=0)
        ssq = ssq + jnp.sum(kpet * kpet, axis=0, keepdims=True)
        r = lax.rsqrt(ssq * (1.0 / QK_HEAD) + EPS)
        sc = both[n_k:] + _dot(qr[which], kpet.astype(BF16))
        return (sc.reshape(n_new, N_HEADS, keys) * r[None]).reshape(rows_q, keys)

    def stage_a(slt, which, part):
        pages = range(part * PART_PAGES, (part + 1) * PART_PAGES)
        rows = slice(pages[0] * PAGE_SIZE, (pages[-1] + 1) * PAGE_SIZE)
        cb = ckv_buf[slt, rows, :].astype(BF16)
        cb_buf[slt, rows, :] = cb
        kpet = jnp.concatenate([kpe_buf[slt, p] for p in pages], axis=1)
        sc_buf[slt, :, rows] = scaled_scores(which, cb, kpet)

    def stage_b(sc, cb):
        m_old = m_ref[...]
        m_new = jnp.maximum(m_old, jnp.max(sc, axis=1, keepdims=True))
        p = jnp.exp2(sc - m_new)
        alpha = jnp.exp2(m_old - m_new)
        l_ref[...] = alpha * l_ref[...] + jnp.sum(p, axis=1, keepdims=True)
        acc_ref[...] = alpha * acc_ref[...] + _dot(p.astype(BF16), cb)
        m_ref[...] = m_new

    @pl.when(b == 0)
    def _():
        for which in range(2):
            lhs_ref[which, :n_k, :] = wukt_ref[...]

    lhs_ref[0, n_k:, :] = qabs_ref[0]
    lhs_ref[1, n_k:, :] = qabs_nx_ref[0]

    @pl.when(b == 0)
    def _():
        for cp in page_copies(0, 0):
            cp.start()
        for cp in page_copies(1, 1):
            cp.start()
        for cp in page_copies(0, 0):
            cp.wait()
        for part in range(parts):
            stage_a(0, 0, part)

    m_ref[...] = jnp.full(m_ref.shape, NEG_INF, F32)
    l_ref[...] = jnp.zeros(l_ref.shape, F32)
    acc_ref[...] = jnp.zeros(acc_ref.shape, F32)

    for c in range(n_chunks):
        g = b * n_chunks + c
        slot, nxt = c % 2, (c + 1) % 2
        which = 0 if c + 1 < n_chunks else 1

        @pl.when(g + 1 < total)
        def _():
            for cp in page_copies(g + 1, nxt):
                cp.wait()

        @pl.when(g + 2 < total)
        def _():
            for cp in page_copies(g + 2, slot):
                cp.start()

        stage_a(nxt, which, 0)
        stage_b(sc_buf[slot], cb_buf[slot])
        for part in range(1, parts):
            stage_a(nxt, which, part)

    tkn = ckvn_ref.shape[1]
    cbn = ckvn_ref[0].astype(BF16)
    sc = scaled_scores(0, cbn, kpetn_ref[0])
    key = lax.broadcasted_iota(jnp.int32, (rows_q, tkn), 1)
    tok = lax.broadcasted_iota(jnp.int32, (rows_q, tkn), 0) // N_HEADS
    stage_b(jnp.where(key <= tok, sc, NEG_INF), cbn)
    o_ref[0] = (acc_ref[...] / l_ref[...]).astype(o_ref.dtype)


def _paged_attn(page_table, qabs, qr, ckv_new, kpet_new, w_ukt, cache_ckv, cache_kpet, layer, chunk_pages):
    n_seq, pages_per_seq = page_table.shape
    assert pages_per_seq % chunk_pages == 0 and chunk_pages % PART_PAGES == 0
    n_chunks = pages_per_seq // chunk_pages
    assert n_chunks % 2 == 0 and n_seq * n_chunks >= 2
    rows_q, kv_lora = qabs.shape[1:]
    n_new = rows_q // N_HEADS
    tk = chunk_pages * PAGE_SIZE
    seq_block = lambda shape: pl.BlockSpec((1,) + shape, lambda b, pt: (b, 0, 0))
    nxt_block = lambda shape: pl.BlockSpec((1,) + shape, lambda b, pt: (jnp.minimum(b + 1, n_seq - 1), 0, 0))
    grid_spec = pltpu.PrefetchScalarGridSpec(
        num_scalar_prefetch=1,
        grid=(n_seq,),
        in_specs=[seq_block(qabs.shape[1:]), seq_block(qr.shape[1:]),
                  nxt_block(qabs.shape[1:]), nxt_block(qr.shape[1:]),
                  seq_block(ckv_new.shape[1:]), seq_block(kpet_new.shape[1:]),
                  pl.BlockSpec(w_ukt.shape, lambda b, pt: (0, 0)),
                  pl.BlockSpec(memory_space=pl.ANY), pl.BlockSpec(memory_space=pl.ANY)],
        out_specs=seq_block((rows_q, kv_lora)),
        scratch_shapes=[pltpu.VMEM((2, tk, kv_lora), F32), pltpu.VMEM((2, chunk_pages, QK_ROPE, PAGE_SIZE), F32),
                        pltpu.VMEM((2, tk, kv_lora), BF16), pltpu.VMEM((2, rows_q, tk), F32),
                        pltpu.VMEM((2, w_ukt.shape[0] + rows_q, kv_lora), BF16),
                        pltpu.VMEM((rows_q, 1), F32), pltpu.VMEM((rows_q, 1), F32),
                        pltpu.VMEM((rows_q, kv_lora), F32),
                        pltpu.SemaphoreType.DMA((2,)), pltpu.SemaphoreType.DMA((2,))])
    return pl.pallas_call(
        functools.partial(_paged_attn_kernel, layer=layer, n_chunks=n_chunks, chunk_pages=chunk_pages, n_new=n_new),
        grid_spec=grid_spec,
        out_shape=jax.ShapeDtypeStruct((n_seq, rows_q, kv_lora), BF16),
        compiler_params=_params("arbitrary"),
        name="paged_attn",
    )(page_table.reshape(-1), qabs, qr, qabs, qr, ckv_new, kpet_new, w_ukt, cache_ckv, cache_kpet)


def _out_proj_kernel(x_ref, a_ref, b_ref, sga_ref, sgb_ref, p_ref, wuvb_ref, wa_ref, wb_ref, wo_ref, gffn_ref,
                     wg_ref, wup_ref, wdn_ref, gple_ref, wpg_ref, wple_ref, y_ref, *, latent_b):
    b = b_ref[...]
    if latent_b:
        b = _dot(b, wuvb_ref[...]).astype(BF16)
    merged = (sga_ref[...].astype(F32) * _dot(a_ref[...], wa_ref[...])
              + sgb_ref[...].astype(F32) * _dot(b, wb_ref[...]))
    x = x_ref[...] + _dot(merged.astype(BF16), wo_ref[...])

    def norm(t, g_ref):
        return (t * lax.rsqrt(jnp.mean(t * t, axis=-1, keepdims=True) + EPS) * g_ref[...]).astype(BF16)

    hf = norm(x, gffn_ref)
    act = jax.nn.silu(_dot(hf, wg_ref[...])) * _dot(hf, wup_ref[...])
    x = x + _dot(act.astype(BF16), wdn_ref[...])
    gate = jax.nn.sigmoid(_dot(norm(x, gple_ref), wpg_ref[...]))
    y_ref[...] = x + gate * _dot(p_ref[...].astype(BF16), wple_ref[...])


def _out_proj(x, a, b, sga, sgb, p, w, tm, latent_b):
    n, d = x.shape
    assert n % tm == 0
    row = lambda width: pl.BlockSpec((tm, width), lambda i: (i, 0))
    consts = [w[k] for k in ("w_uv_bd", "w_a", "w_b", "w_out", "g_ffn", "w_gate", "w_up", "w_down", "g_ple",
                             "w_ple_gate", "w_ple")]
    return pl.pallas_call(
        functools.partial(_out_proj_kernel, latent_b=latent_b),
        grid=(n // tm,),
        in_specs=[row(d), row(a.shape[1]), row(b.shape[1]), row(d), row(d), row(p.shape[1])]
        + [_const_spec(c.shape) for c in consts],
        out_specs=row(d),
        out_shape=jax.ShapeDtypeStruct((n, d), F32),
        compiler_params=_params("parallel"),
        name="out_proj_latent" if latent_b else "out_proj",
    )(x, a, b, sga, sgb, p, *consts)


def _head_blocked(w_nope, w_rope):
    k = w_nope.shape[0] if w_nope is not None else w_rope.shape[0]
    nope = w_nope if w_nope is not None else jnp.zeros((k, N_HEADS, QK_NOPE), F32)
    rope = w_rope if w_rope is not None else jnp.zeros((k, N_HEADS, QK_ROPE), F32)
    pad = jnp.zeros((k, N_HEADS, HEAD_BLOCK - QK_HEAD), F32)
    return jnp.concatenate([nope, rope, pad], axis=-1).reshape(k, N_HEADS * HEAD_BLOCK)


def _rot_cols(w):
    half = QK_ROPE // 2
    return jnp.concatenate([-w[..., half:], w[..., :half]], axis=-1)


def _head_gain(g_nope, g_rope_half):
    g = jnp.concatenate([g_nope, g_rope_half, g_rope_half, jnp.zeros((HEAD_BLOCK - QK_HEAD,), F32)])
    return jnp.tile(g, N_HEADS)[None, :]


def _block_diag(blocks):
    g, r, c = blocks.shape
    eye = jnp.eye(g, dtype=blocks.dtype)
    return jnp.einsum("grc,gh->grhc", blocks, eye).reshape(g * r, g * c)


def _prepare(norm_attn_g, w_in, q_norm_g, w_uq, kv_norm_g, w_uk, w_uv, q_norm_nope_g, q_norm_rope_g,
             k_norm_nope_g, k_norm_rope_g, disc, ssm_c_re, ssm_c_im, ssm_d, w_glu, w_branch_a, w_branch_b,
             w_out, norm_ffn_g, w_gate, w_up, w_down, norm_ple_g, w_ple_gate, w_ple):
    d_model = w_in.shape[0]
    ssm_w = ssm_d.shape[0]
    q_lora, kv_lora = w_uq.shape[0], w_uk.shape[0]
    o = np.cumsum([0, ssm_w, q_lora, kv_lora, QK_ROPE, d_model, d_model])
    bf = lambda a: a.astype(BF16)
    w = {}
    w["g_attn"] = norm_attn_g[None, :]
    w["w_u"] = bf(w_in[:, o[0]:o[1]])
    w["w_q"] = bf(w_in[:, o[1]:o[2]])
    w["w_kv"] = bf(w_in[:, o[2]:o[3]])
    w_kr = w_in[:, o[3]:o[4]]
    place = lambda r: jnp.pad(r, ((0, 0), (QK_NOPE, HEAD_BLOCK - QK_HEAD)))
    w["w_kr"] = bf(jnp.concatenate([place(w_kr), place(_rot_cols(w_kr))], axis=1))
    w["w_ga"] = bf(w_in[:, o[4]:o[5]])
    w["w_gb"] = bf(w_in[:, o[5]:o[6]])
    w["g_q"] = q_norm_g[None, :]
    uq_nope, uq_rope = w_uq[..., :QK_NOPE], w_uq[..., QK_NOPE:]
    w["w_uq"] = bf(jnp.concatenate([_head_blocked(uq_nope, uq_rope), _head_blocked(None, _rot_cols(uq_rope))], axis=1))
    w["g_kv"] = kv_norm_g[None, :]
    w["w_uk"] = bf(_head_blocked(w_uk, None))
    uv_t = jnp.pad(jnp.transpose(w_uv, (1, 2, 0)), ((0, 0), (0, VT_ROWS - V_HEAD), (0, 0)))
    w["w_uv_t"] = bf(uv_t.reshape(N_HEADS * VT_ROWS, kv_lora))
    w["g_qh"] = _head_gain(q_norm_nope_g, q_norm_rope_g)
    w["g_kh"] = _head_gain(k_norm_nope_g, k_norm_rope_g)
    e = np.zeros((N_HEADS * HEAD_BLOCK, LANES), np.float32)
    for h in range(N_HEADS):
        e[h * HEAD_BLOCK:h * HEAD_BLOCK + QK_HEAD, h] = 1.0
    w["e"] = jnp.asarray(e, BF16)
    w["et"] = jnp.asarray(e.T, BF16)
    ukt = jnp.transpose(w_uk, (1, 2, 0))
    ukt = jnp.pad(ukt, ((0, 0), (0, HEAD_BLOCK - QK_NOPE), (0, 0)))
    w["w_abs"] = bf(_block_diag(ukt))
    sel = np.zeros((N_HEADS * HEAD_BLOCK, N_HEADS * HEAD_BLOCK), np.float32)
    for h in range(N_HEADS):
        for dd in range(QK_ROPE):
            sel[h * HEAD_BLOCK + QK_NOPE + dd, h * HEAD_BLOCK + dd] = 1.0
    w["sel_rope"] = jnp.asarray(sel, BF16)
    w["w_ukt"] = bf(jnp.transpose(w_uk, (2, 1, 0)).reshape(QK_NOPE * N_HEADS, kv_lora))
    w["w_uv_bd"] = bf(_block_diag(jnp.transpose(w_uv, (1, 0, 2))))
    lb_re, lb_im, bb_re, bb_im = disc
    w["a_re"] = lb_re.reshape(1, -1)
    w["a_im"] = lb_im.reshape(1, -1)
    tr = lambda a: jnp.transpose(a, (0, 2, 1))
    w["w_bu"] = bf(jnp.concatenate([_block_diag(tr(bb_re)), _block_diag(tr(bb_im))], axis=1))
    w["w_y"] = bf(jnp.concatenate([_block_diag(tr(ssm_c_re)), -_block_diag(tr(ssm_c_im))], axis=0))
    w["ssm_d"] = ssm_d[None, :]
    w["w_glu"] = bf(w_glu)
    w["w_a"], w["w_b"], w["w_out"] = bf(w_branch_a), bf(w_branch_b), bf(w_out)
    w["g_ffn"], w["g_ple"] = norm_ffn_g[None, :], norm_ple_g[None, :]
    w["w_gate"], w["w_up"], w["w_down"] = bf(w_gate), bf(w_up), bf(w_down)
    w["w_ple_gate"], w["w_ple"] = bf(w_ple_gate), bf(w_ple)
    return w


def _rope_tables(pos):
    half = QK_ROPE // 2
    inv = ROPE_THETA ** (-jnp.arange(half, dtype=F32) / half)
    ang = pos.astype(F32)[:, None] * inv[None, :]
    cos, sin = jnp.cos(ang), jnp.sin(ang)
    n = pos.shape[0]
    ones, zeros = jnp.ones((n, QK_NOPE), F32), jnp.zeros((n, QK_NOPE), F32)
    pad = jnp.zeros((n, HEAD_BLOCK - QK_HEAD), F32)
    return (jnp.concatenate([ones, cos, cos, pad], axis=1), jnp.concatenate([zeros, sin, sin, pad], axis=1))


def _pick(n, prefs):
    for t in prefs:
        if n % t == 0:
            return t
    return n


def kernel(x_prompt, x_sample, p_prompt, p_sample, cache_ckv, cache_kpe, state_ssm_re, state_ssm_im, page_table, norm_attn_g, w_in, q_norm_g, w_uq, kv_norm_g, w_uk, w_uv, q_norm_nope_g, q_norm_rope_g, k_norm_nope_g, k_norm_rope_g, ssm_lam_re, ssm_lam_im, ssm_log_dt, ssm_b_re, ssm_b_im, ssm_c_re, ssm_c_im, ssm_d, w_glu, w_branch_a, w_branch_b, w_out, norm_ffn_g, w_gate, w_up, w_down, norm_ple_g, w_ple_gate, w_ple):
    depth = w_in.shape[0]
    batch, seq, d_model = x_prompt.shape
    n_seq, n_new, _ = x_sample.shape
    past = page_table.shape[1] * PAGE_SIZE
    groups, n_state = ssm_lam_re.shape[1:]
    ns = groups * n_state

    cos_p, sin_p = _rope_tables(jnp.arange(seq))
    cos_p, sin_p = jnp.tile(cos_p, (batch, 1)), jnp.tile(sin_p, (batch, 1))
    cos_s, sin_s = _rope_tables(past + jnp.arange(n_new))
    cos_s, sin_s = jnp.tile(cos_s, (n_seq, 1)), jnp.tile(sin_s, (n_seq, 1))

    n_seg = SUBLANES
    assert seq % n_seg == 0
    seg_len = seq // n_seg
    chains_p = batch * n_seg
    steps_p = _pick(seg_len, (16, 8, 4, 2))
    steps_s = _pick(n_new, (4, 2))
    tm_p = _pick(batch * seq, (512, 256, 128))
    tm_s = _pick(n_seq * n_new, (512, 256, 128))
    assert seq % KV_BLOCK == 0
    chunk_pages = _pick(page_table.shape[1] // 2, (32, 16, 8))

    xp = x_prompt.reshape(batch * seq, d_model)
    xs = x_sample.reshape(n_seq * n_new, d_model)
    outs = {k: [] for k in ("ckv_p", "kpe_p", "hre_p", "him_p", "ckv_s", "kpe_s", "hre_s", "him_s")}
    for i in range(depth):
        disc = _discretize(ssm_lam_re[i], ssm_lam_im[i], ssm_log_dt[i], ssm_b_re[i], ssm_b_im[i])
        w = _prepare(norm_attn_g[i], w_in[i], q_norm_g[i], w_uq[i], kv_norm_g[i], w_uk[i], w_uv[i],
                     q_norm_nope_g[i], q_norm_rope_g[i], k_norm_nope_g[i], k_norm_rope_g[i], disc,
                     ssm_c_re[i], ssm_c_im[i], ssm_d[i], w_glu[i], w_branch_a[i], w_branch_b[i], w_out[i],
                     norm_ffn_g[i], w_gate[i], w_up[i], w_down[i], norm_ple_g[i], w_ple_gate[i], w_ple[i])
        ssm_w = w["w_u"].shape[1]

        u, q, ckv, kpe, sga, sgb, k, vt = _in_proj(xp, cos_p, sin_p, w, tm_p, with_kv=True)
        u_rows = u.reshape(batch, n_seg, seg_len, ssm_w).transpose(2, 0, 1, 3).reshape(batch * seq, ssm_w)
        zero_state = jnp.zeros((chains_p, 2 * ns), F32)
        h_end = _ssm(u_rows, zero_state, w, chains_p, steps_p, emit_y=False)
        h_start = _ssm_combine(h_end, w, seg_len, n_seg)
        a_rows, h_fin = _ssm(u_rows, h_start, w, chains_p, steps_p, emit_y=True)
        a_out = a_rows.reshape(seg_len, batch, n_seg, ssm_w).transpose(1, 2, 0, 3).reshape(batch * seq, ssm_w)
        b_out = _flash(q, k, vt, batch, seq)
        xp = _out_proj(xp, a_out, b_out, sga, sgb, p_prompt[i].reshape(batch * seq, -1), w, tm_p, latent_b=False)
        h_fin = h_fin.reshape(batch, n_seg, 2 * ns)[:, -1]
        outs["ckv_p"].append(ckv.reshape(batch, seq, -1))
        outs["kpe_p"].append(kpe.reshape(batch, seq, -1))
        outs["hre_p"].append(h_fin[:, :ns].reshape(batch, groups, n_state))
        outs["him_p"].append(h_fin[:, ns:].reshape(batch, groups, n_state))

        u, q, ckv, kpe, sga, sgb = _in_proj(xs, cos_s, sin_s, w, tm_s, with_kv=False)
        u_rows = u.reshape(n_seq, n_new, ssm_w).transpose(1, 0, 2).reshape(n_seq * n_new, ssm_w)
        h0 = jnp.concatenate([state_ssm_re[i].reshape(n_seq, ns), state_ssm_im[i].reshape(n_seq, ns)], axis=1)
        a_rows, h_fin = _ssm(u_rows, h0, w, n_seq, steps_s, emit_y=True)
        a_out = a_rows.reshape(n_new, n_seq, ssm_w).transpose(1, 0, 2).reshape(n_seq * n_new, ssm_w)
        qabs, qr = _absorb(q, w)
        rows_q = n_new * N_HEADS
        kv_lora = ckv.shape[1]
        pad_new = PAGE_SIZE - n_new
        ckv_new = jnp.pad(ckv.reshape(n_seq, n_new, kv_lora), ((0, 0), (0, pad_new), (0, 0)))
        kpet_new = jnp.pad(kpe.reshape(n_seq, n_new, QK_ROPE).transpose(0, 2, 1), ((0, 0), (0, 0), (0, pad_new)))
        o_lat = _paged_attn(page_table, qabs.reshape(n_seq, rows_q, kv_lora), qr.reshape(n_seq, rows_q, HEAD_BLOCK),
                            ckv_new, kpet_new, w["w_ukt"], cache_ckv, jnp.swapaxes(cache_kpe, 2, 3), i, chunk_pages)
        b_lat = o_lat.reshape(n_seq * n_new, N_HEADS * kv_lora)
        xs = _out_proj(xs, a_out, b_lat, sga, sgb, p_sample[i].reshape(n_seq * n_new, -1), w, tm_s, latent_b=True)
        outs["ckv_s"].append(ckv.reshape(n_seq, n_new, -1))
        outs["kpe_s"].append(kpe.reshape(n_seq, n_new, -1))
        outs["hre_s"].append(h_fin[:, :ns].reshape(n_seq, groups, n_state))
        outs["him_s"].append(h_fin[:, ns:].reshape(n_seq, groups, n_state))

    st = lambda key: jnp.stack(outs[key])
    return (xp.reshape(batch, seq, d_model), xs.reshape(n_seq, n_new, d_model),
            st("ckv_p"), st("kpe_p"), st("hre_p"), st("him_p"),
            st("ckv_s"), st("kpe_s"), st("hre_s"), st("him_s"))
```

```python
import functools
import math

import numpy as np
import jax
import jax.numpy as jnp
from jax import lax
from jax.experimental import pallas as pl
from jax.experimental.pallas import tpu as pltpu

N_HEADS = 8
QK_NOPE = 64
QK_ROPE = 32
QK_HEAD = QK_NOPE + QK_ROPE
V_HEAD = 64
VT_ROWS = 80
HEAD_BLOCK = 128
ROPE_THETA = 10000.0
ATTN_SCALE = QK_HEAD ** -0.5
LOG2E = math.log2(math.e)
SSM_GROUP = 16
SSM_STATE = 64
PAGE_SIZE = 128
KV_BLOCK = 256
PART_PAGES = 8
SSM_SPLIT = 2
EPS = 1e-6

LANES = 128
SUBLANES = 8
VMEM_LIMIT_BYTES = 56 * 1024 * 1024

BF16 = jnp.bfloat16
F32 = jnp.float32
NEG_INF = float("-inf")


def _dot(a, b):
    return jnp.dot(a, b, preferred_element_type=F32)


def _dot_nt(a, b):
    return lax.dot_general(a, b, (((1,), (1,)), ((), ())), preferred_element_type=F32)


def _const_spec(shape):
    nd = len(shape)
    return pl.BlockSpec(shape, lambda *_: (0,) * nd, pipeline_mode=pl.Buffered(1))


def _params(*sem):
    return pltpu.CompilerParams(dimension_semantics=sem, vmem_limit_bytes=VMEM_LIMIT_BYTES)


def _discretize_kernel(lr_ref, li_ref, ldt_ref, br_ref, bi_ref, lbr_ref, lbi_ref, bbr_ref, bbi_ref):
    lr, li = lr_ref[...], li_ref[...]
    dt = jnp.exp(ldt_ref[...])
    mag = jnp.exp(lr * dt)
    lb_re, lb_im = mag * jnp.cos(li * dt), mag * jnp.sin(li * dt)
    num_re, num_im = lb_re - 1.0, lb_im
    den = lr * lr + li * li
    f_re = (num_re * lr + num_im * li) / den
    f_im = (num_im * lr - num_re * li) / den
    br, bi = br_ref[...], bi_ref[...]
    lbr_ref[...] = lb_re
    lbi_ref[...] = lb_im
    bbr_ref[...] = f_re * br - f_im * bi
    bbi_ref[...] = f_re * bi + f_im * br


def _discretize(lam_re, lam_im, log_dt, b_re, b_im):
    g, n, c = b_re.shape
    rep = lambda a: jnp.repeat(a, c, axis=-1)
    args = (rep(lam_re), rep(lam_im), jnp.broadcast_to(log_dt[:, None], (g, n * c)),
            b_re.reshape(g, n * c), b_im.reshape(g, n * c))
    sds = jax.ShapeDtypeStruct((g, n * c), F32)
    lbr, lbi, bbr, bbi = pl.pallas_call(_discretize_kernel, out_shape=(sds,) * 4, name="discretize")(*args)
    lb_re = lbr.reshape(g, n, c)[..., 0]
    lb_im = lbi.reshape(g, n, c)[..., 0]
    return lb_re, lb_im, bbr.reshape(g, n, c), bbi.reshape(g, n, c)


def _in_proj_kernel(x_ref, cos_ref, sin_ref, gattn_ref, wu_ref, wq_ref, wkv_ref, wkr_ref, wga_ref, wgb_ref,
                    gq_ref, wuq_ref, gkv_ref, wuk_ref, wuv_ref, gqh_ref, gkh_ref,
                    u_ref, q_ref, ckv_ref, kpe_ref, sga_ref, sgb_ref, *kv_refs):
    x = x_ref[...]
    h = x * lax.rsqrt(jnp.mean(x * x, axis=-1, keepdims=True) + EPS) * gattn_ref[...]
    hb = h.astype(BF16)
    cos, sin = cos_ref[...], sin_ref[...]
    cos8 = jnp.concatenate([cos] * N_HEADS, axis=1)
    sin8 = jnp.concatenate([sin] * N_HEADS, axis=1)

    u_ref[...] = _dot(hb, wu_ref[...])
    sga_ref[...] = jax.nn.sigmoid(_dot(hb, wga_ref[...])).astype(sga_ref.dtype)
    sgb_ref[...] = jax.nn.sigmoid(_dot(hb, wgb_ref[...])).astype(sgb_ref.dtype)

    def head_norm(t, gain):
        blocks = []
        for hd in range(N_HEADS):
            blk = t[:, hd * HEAD_BLOCK:(hd + 1) * HEAD_BLOCK]
            ssq = jnp.sum(blk * blk, axis=-1, keepdims=True)
            blocks.append(blk * lax.rsqrt(ssq * (1.0 / QK_HEAD) + EPS))
        return jnp.concatenate(blocks, axis=1) * gain

    q_lat = _dot(hb, wq_ref[...])
    qc = q_lat * lax.rsqrt(jnp.mean(q_lat * q_lat, axis=-1, keepdims=True) + EPS) * gq_ref[...]
    qq = _dot(qc.astype(BF16), wuq_ref[...])
    hw = N_HEADS * HEAD_BLOCK
    q_pre = qq[:, :hw] * cos8 + qq[:, hw:] * sin8
    q_ref[...] = (head_norm(q_pre, gqh_ref[...]) * (ATTN_SCALE * LOG2E)).astype(q_ref.dtype)

    kv_lat = _dot(hb, wkv_ref[...])
    ckv = kv_lat * lax.rsqrt(jnp.mean(kv_lat * kv_lat, axis=-1, keepdims=True) + EPS) * gkv_ref[...]
    ckv_ref[...] = ckv
    kr = _dot(hb, wkr_ref[...])
    kpe_wide = kr[:, :HEAD_BLOCK] * cos + kr[:, HEAD_BLOCK:] * sin
    kpe_ref[...] = kpe_wide[:, QK_NOPE:QK_NOPE + QK_ROPE]

    if not kv_refs:
        return
    k_ref, vt_ref = kv_refs
    cb = ckv.astype(BF16)
    k_pre = _dot(cb, wuk_ref[...]) + jnp.concatenate([kpe_wide] * N_HEADS, axis=1)
    k_ref[...] = head_norm(k_pre, gkh_ref[...]).astype(k_ref.dtype)
    vt = _dot_nt(wuv_ref[...], cb)
    ones_row = lax.broadcasted_iota(jnp.int32, vt.shape, 0) % VT_ROWS == V_HEAD
    vt = jnp.where(ones_row, 1.0, vt).astype(vt_ref.dtype)
    for c in range(vt_ref.shape[0]):
        vt_ref[c] = vt[:, c * KV_BLOCK:(c + 1) * KV_BLOCK]


def _in_proj(x, cos_t, sin_t, w, tm, with_kv):
    n, d = x.shape
    assert n % tm == 0
    row = lambda width: pl.BlockSpec((tm, width), lambda i: (i, 0))
    consts = [w[k] for k in ("g_attn", "w_u", "w_q", "w_kv", "w_kr", "w_ga", "w_gb", "g_q", "w_uq", "g_kv",
                             "w_uk", "w_uv_t", "g_qh", "g_kh")]
    hw = N_HEADS * HEAD_BLOCK
    out_shape = (jax.ShapeDtypeStruct((n, w["w_u"].shape[1]), F32),
                 jax.ShapeDtypeStruct((n, hw), BF16),
                 jax.ShapeDtypeStruct((n, w["w_kv"].shape[1]), F32),
                 jax.ShapeDtypeStruct((n, QK_ROPE), F32),
                 jax.ShapeDtypeStruct((n, d), BF16),
                 jax.ShapeDtypeStruct((n, d), BF16))
    if with_kv:
        assert tm % KV_BLOCK == 0
        out_shape += (jax.ShapeDtypeStruct((n, hw), BF16),
                      jax.ShapeDtypeStruct((n // KV_BLOCK, N_HEADS * VT_ROWS, KV_BLOCK), BF16))
    return pl.pallas_call(
        _in_proj_kernel,
        grid=(n // tm,),
        in_specs=[row(d), row(LANES), row(LANES)] + [_const_spec(c.shape) for c in consts],
        out_specs=tuple(pl.BlockSpec((tm // KV_BLOCK,) + s.shape[1:], lambda i: (i, 0, 0)) if len(s.shape) == 3
                        else row(s.shape[1]) for s in out_shape),
        out_shape=out_shape,
        compiler_params=_params("parallel"),
        name="in_proj",
    )(x, cos_t, sin_t, *consts)


def _ssm_kernel(u_ref, h0_ref, ar_ref, ai_ref, wbu_ref, wy_ref, d_ref, wglu_ref,
                *rest, chains, steps, emit_y):
    if emit_y:
        a_ref, ht_ref, bu_ref, hc_ref = rest
    else:
        ht_ref, bu_ref, hc_ref = rest
    i = pl.program_id(0)
    nh = ar_ref.shape[1] // SSM_SPLIT
    cw = u_ref.shape[1] // SSM_SPLIT

    @pl.when(i == 0)
    def _():
        hc_ref[...] = h0_ref[...]

    u = u_ref[...]
    ub = u.astype(BF16)
    for p in range(SSM_SPLIT):
        bu_ref[:, p * 2 * nh:(p + 1) * 2 * nh] = _dot(ub[:, p * cw:(p + 1) * cw], wbu_ref[p])

    def advance(prev_ref, prev_rows, rows):
        for p in range(SSM_SPLIT):
            re, im = slice(p * 2 * nh, p * 2 * nh + nh), slice(p * 2 * nh + nh, (p + 1) * 2 * nh)
            ar, ai = ar_ref[:, p * nh:(p + 1) * nh], ai_ref[:, p * nh:(p + 1) * nh]
            hr, hi = prev_ref[prev_rows, re], prev_ref[prev_rows, im]
            nr = ar * hr - ai * hi + bu_ref[rows, re]
            ni = ar * hi + ai * hr + bu_ref[rows, im]
            bu_ref[rows, re] = nr
            bu_ref[rows, im] = ni

    row_block = lambda j: pl.ds(pl.multiple_of(j * chains, SUBLANES), chains)
    advance(hc_ref, slice(None), row_block(0))

    def step(j, carry):
        advance(bu_ref, row_block(j - 1), row_block(j))
        return carry

    lax.fori_loop(1, steps, step, 0)
    last = bu_ref[(steps - 1) * chains:, :]
    hc_ref[...] = last
    ht_ref[...] = last
    if emit_y:
        hb = bu_ref[...].astype(BF16)
        y = jnp.concatenate([_dot(hb[:, p * 2 * nh:(p + 1) * 2 * nh], wy_ref[p]) for p in range(SSM_SPLIT)], axis=1)
        z = jax.nn.gelu(y + d_ref[...] * u)
        a_ref[...] = (z * jax.nn.sigmoid(_dot(z.astype(BF16), wglu_ref[...]))).astype(a_ref.dtype)


def _state_to_parts(h):
    c, n2 = h.shape
    return h.reshape(c, 2, SSM_SPLIT, n2 // (2 * SSM_SPLIT)).transpose(0, 2, 1, 3).reshape(c, n2)


def _state_from_parts(h):
    c, n2 = h.shape
    return h.reshape(c, SSM_SPLIT, 2, n2 // (2 * SSM_SPLIT)).transpose(0, 2, 1, 3).reshape(c, n2)


def _ssm(u_rows, h0, w, chains, steps, emit_y):
    n, width = u_rows.shape
    rows = chains * steps
    assert n % rows == 0
    ns2 = h0.shape[1]
    consts = [w[k] for k in ("a_re", "a_im", "w_bu", "w_y", "ssm_d", "w_glu")]
    ht_sds = jax.ShapeDtypeStruct((chains, ns2), F32)
    ht_spec = pl.BlockSpec((chains, ns2), lambda i: (0, 0))
    if emit_y:
        out_shape = (jax.ShapeDtypeStruct((n, width), BF16), ht_sds)
        out_specs = (pl.BlockSpec((rows, width), lambda i: (i, 0)), ht_spec)
    else:
        out_shape, out_specs = ht_sds, ht_spec
    return pl.pallas_call(
        functools.partial(_ssm_kernel, chains=chains, steps=steps, emit_y=emit_y),
        grid=(n // rows,),
        in_specs=[pl.BlockSpec((rows, width), lambda i: (i, 0)), _const_spec(h0.shape)]
        + [_const_spec(c.shape) for c in consts],
        out_specs=out_specs,
        out_shape=out_shape,
        scratch_shapes=[pltpu.VMEM((rows, ns2), F32), pltpu.VMEM((chains, ns2), F32)],
        compiler_params=_params("arbitrary"),
        name="ssm_scan" if emit_y else "ssm_end_state",
    )(u_rows, h0, *consts)


def _ssm_combine_kernel(hend_ref, ar_ref, ai_ref, h0_ref, *, seg_len, n_seg):
    ns = ar_ref.shape[1]
    pr, pi = ar_ref[...], ai_ref[...]
    qr, qi = None, None
    e = seg_len
    while e:
        if e & 1:
            qr, qi = (pr, pi) if qr is None else (qr * pr - qi * pi, qr * pi + qi * pr)
        e >>= 1
        if e:
            pr, pi = pr * pr - pi * pi, 2.0 * pr * pi
    er, ei = hend_ref[:, :ns], hend_ref[:, ns:]
    seg = lax.broadcasted_iota(jnp.int32, er.shape, 0) % n_seg
    hr, hi = jnp.zeros_like(er), jnp.zeros_like(ei)
    for s in range(1, n_seg):
        xr = qr * hr - qi * hi + er
        xi = qr * hi + qi * hr + ei
        hr = jnp.where(seg == s, pltpu.roll(xr, 1, 0), hr)
        hi = jnp.where(seg == s, pltpu.roll(xi, 1, 0), hi)
    h0_ref[:, :ns] = hr
    h0_ref[:, ns:] = hi


def _ssm_combine(hend, w, seg_len, n_seg):
    return pl.pallas_call(
        functools.partial(_ssm_combine_kernel, seg_len=seg_len, n_seg=n_seg),
        out_shape=jax.ShapeDtypeStruct(hend.shape, F32),
        name="ssm_combine",
    )(hend, w["a_re"], w["a_im"])


def _flash_kernel(q_ref, k_ref, vt_ref, o_ref, s_buf, p_buf, acc_ref, m_ref, alpha_ref):
    i = pl.program_id(2)
    n_blocks = k_ref.shape[0] // KV_BLOCK
    tq = q_ref.shape[0]
    heads = range(2)
    q = [q_ref[:, hh * HEAD_BLOCK:(hh + 1) * HEAD_BLOCK] for hh in heads]
    key_minus_query = (lax.broadcasted_iota(jnp.int32, (KV_BLOCK, tq), 0)
                       - lax.broadcasted_iota(jnp.int32, (KV_BLOCK, tq), 1))

    def scores(t, slot):
        j = jnp.minimum(t, n_blocks - 1)
        rows = pl.ds(pl.multiple_of(j * KV_BLOCK, KV_BLOCK), KV_BLOCK)
        for hh in heads:
            s_buf[slot, hh] = _dot_nt(k_ref[rows, hh * HEAD_BLOCK:(hh + 1) * HEAD_BLOCK], q[hh])

    def softmax(t, slot, hh):
        s = jnp.where(key_minus_query <= (i - t) * KV_BLOCK, s_buf[slot, hh], NEG_INF)
        m = m_ref[hh]
        m_new = jnp.maximum(m, jnp.max(s, axis=0, keepdims=True))
        m_ref[hh] = m_new
        p_buf[slot, hh] = jnp.exp2(s - m_new).astype(BF16)
        alpha_ref[hh] = jnp.exp2(m - m_new)

    def values(t, slot, hh):
        j = jnp.clip(t, 0, n_blocks - 1)
        acc_ref[hh] = (alpha_ref[hh] * acc_ref[hh]
                       + _dot(vt_ref[j, hh * VT_ROWS:(hh + 1) * VT_ROWS, :], p_buf[slot, hh]))

    def step(t, slot):
        scores(t + 1, 1 - slot)
        for hh in heads:
            values(t - 1, 1 - slot, hh)
        for hh in heads:
            softmax(t, slot, hh)

    scores(0, 0)
    p_buf[1] = jnp.zeros(p_buf.shape[1:], BF16)
    alpha_ref[...] = jnp.ones(alpha_ref.shape, F32)
    m_ref[...] = jnp.full(m_ref.shape, NEG_INF, F32)
    acc_ref[...] = jnp.zeros(acc_ref.shape, F32)

    def two_steps(u, carry):
        step(2 * u, 0)
        step(2 * u + 1, 1)
        return carry

    lax.fori_loop(0, (i + 3) // 2, two_steps, 0)
    out = [acc_ref[hh, :V_HEAD] / acc_ref[hh, V_HEAD:V_HEAD + 1] for hh in heads]
    o_ref[...] = jnp.concatenate(out, axis=0).T.astype(o_ref.dtype)


def _flash(q, k, vt, batch, seq):
    tq = KV_BLOCK
    nq = seq // tq
    pairs = N_HEADS // 2
    return pl.pallas_call(
        _flash_kernel,
        grid=(batch, pairs, nq),
        in_specs=[pl.BlockSpec((tq, 2 * HEAD_BLOCK), lambda b, p, i: (b * nq + i, p)),
                  pl.BlockSpec((seq, 2 * HEAD_BLOCK), lambda b, p, i: (b, p)),
                  pl.BlockSpec((nq, 2 * VT_ROWS, KV_BLOCK), lambda b, p, i: (b, p, 0))],
        out_specs=pl.BlockSpec((tq, 2 * V_HEAD), lambda b, p, i: (b * nq + i, p)),
        out_shape=jax.ShapeDtypeStruct((batch * seq, N_HEADS * V_HEAD), BF16),
        scratch_shapes=[pltpu.VMEM((2, 2, KV_BLOCK, tq), F32), pltpu.VMEM((2, 2, KV_BLOCK, tq), BF16),
                        pltpu.VMEM((2, VT_ROWS, tq), F32), pltpu.VMEM((2, 1, tq), F32),
                        pltpu.VMEM((2, 1, tq), F32)],
        compiler_params=_params("parallel", "parallel", "arbitrary"),
        name="flash",
    )(q, k, vt)


def _absorb_kernel(q_ref, gk_ref, wabs_ref, sel_ref, qabs_ref, qr_ref):
    qg = (q_ref[...].astype(F32) * gk_ref[...]).astype(BF16)
    qabs_ref[...] = _dot(qg, wabs_ref[...]).astype(qabs_ref.dtype)
    qr_ref[...] = _dot(qg, sel_ref[...]).astype(qr_ref.dtype)


def _absorb(q, w):
    n = q.shape[0]
    return pl.pallas_call(
        _absorb_kernel,
        out_shape=(jax.ShapeDtypeStruct((n, w["w_abs"].shape[1]), BF16),
                   jax.ShapeDtypeStruct((n, w["sel_rope"].shape[1]), BF16)),
        compiler_params=_params(),
        name="absorb",
    )(q, w["g_kh"], w["w_abs"], w["sel_rope"])


def _paged_attn_kernel(pt_ref, qabs_ref, qr_ref, qabs_nx_ref, qr_nx_ref, ckvn_ref, kpetn_ref, wukt_ref,
                       ckv_hbm, kpet_hbm, o_ref, ckv_buf, kpe_buf, cb_buf, sc_buf, lhs_ref, m_ref, l_ref, acc_ref,
                       sem_c, sem_k, *, layer, n_chunks, chunk_pages, n_new):
    b = pl.program_id(0)
    total = pl.num_programs(0) * n_chunks
    n_k = wukt_ref.shape[0]
    rows_q = qabs_ref.shape[1]
    tk = chunk_pages * PAGE_SIZE
    parts = chunk_pages // PART_PAGES
    qr = (qr_ref[0][:, :QK_ROPE], qr_nx_ref[0][:, :QK_ROPE])

    def page_copies(g, slt):
        out = []
        for p in range(chunk_pages):
            page = pt_ref[g * chunk_pages + p]
            dst = pl.ds(p * PAGE_SIZE, PAGE_SIZE)
            out.append(pltpu.make_async_copy(ckv_hbm.at[layer, page], ckv_buf.at[slt, dst, :], sem_c.at[slt]))
            out.append(pltpu.make_async_copy(kpet_hbm.at[layer, page], kpe_buf.at[slt, p], sem_k.at[slt]))
        return out

    def scaled_scores(which, cb, kpet):
        keys = cb.shape[0]
        both = _dot_nt(lhs_ref[which], cb)
        kt = both[:n_k]
        ssq = jnp.sum((kt * kt).reshape(QK_NOPE, N_HEADS, keys), axis=0)
        ssq = ssq + jnp.sum(kpet * kpet, axis=0, keepdims=True)
        r = lax.rsqrt(ssq * (1.0 / QK_HEAD) + EPS)
        sc = both[n_k:] + _dot(qr[which], kpet.astype(BF16))
        return (sc.reshape(n_new, N_HEADS, keys) * r[None]).reshape(rows_q, keys)

    def stage_a(slt, which, part):
        pages = range(part * PART_PAGES, (part + 1) * PART_PAGES)
        rows = slice(pages[0] * PAGE_SIZE, (pages[-1] + 1) * PAGE_SIZE)
        cb = ckv_buf[slt, rows, :].astype(BF16)
        cb_buf[slt, rows, :] = cb
        kpet = jnp.concatenate([kpe_buf[slt, p] for p in pages], axis=1)
        sc_buf[slt, :, rows] = scaled_scores(which, cb, kpet)

    def stage_b(sc, cb):
        m_old = m_ref[...]
        m_new = jnp.maximum(m_old, jnp.max(sc, axis=1, keepdims=True))
        p = jnp.exp2(sc - m_new)
        alpha = jnp.exp2(m_old - m_new)
        l_ref[...] = alpha * l_ref[...] + jnp.sum(p, axis=1, keepdims=True)
        acc_ref[...] = alpha * acc_ref[...] + _dot(p.astype(BF16), cb)
        m_ref[...] = m_new

    @pl.when(b == 0)
    def _():
        for which in range(2):
            lhs_ref[which, :n_k, :] = wukt_ref[...]

    lhs_ref[0, n_k:, :] = qabs_ref[0]
    lhs_ref[1, n_k:, :] = qabs_nx_ref[0]

    @pl.when(b == 0)
    def _():
        for cp in page_copies(0, 0):
            cp.start()
        for cp in page_copies(1, 1):
            cp.start()
        for cp in page_copies(0, 0):
            cp.wait()
        for part in range(parts):
            stage_a(0, 0, part)

    m_ref[...] = jnp.full(m_ref.shape, NEG_INF, F32)
    l_ref[...] = jnp.zeros(l_ref.shape, F32)
    acc_ref[...] = jnp.zeros(acc_ref.shape, F32)

    for c in range(n_chunks):
        g = b * n_chunks + c
        slot, nxt = c % 2, (c + 1) % 2
        which = 0 if c + 1 < n_chunks else 1

        @pl.when(g + 1 < total)
        def _():
            for cp in page_copies(g + 1, nxt):
                cp.wait()

        @pl.when(g + 2 < total)
        def _():
            for cp in page_copies(g + 2, slot):
                cp.start()

        stage_a(nxt, which, 0)
        stage_b(sc_buf[slot], cb_buf[slot])
        for part in range(1, parts):
            stage_a(nxt, which, part)

    tkn = ckvn_ref.shape[1]
    cbn = ckvn_ref[0].astype(BF16)
    sc = scaled_scores(0, cbn, kpetn_ref[0])
    key = lax.broadcasted_iota(jnp.int32, (rows_q, tkn), 1)
    tok = lax.broadcasted_iota(jnp.int32, (rows_q, tkn), 0) // N_HEADS
    stage_b(jnp.where(key <= tok, sc, NEG_INF), cbn)
    o_ref[0] = (acc_ref[...] / l_ref[...]).astype(o_ref.dtype)


def _paged_attn(page_table, qabs, qr, ckv_new, kpet_new, w_ukt, cache_ckv, cache_kpet, layer, chunk_pages):
    n_seq, pages_per_seq = page_table.shape
    assert pages_per_seq % chunk_pages == 0 and chunk_pages % PART_PAGES == 0
    n_chunks = pages_per_seq // chunk_pages
    assert n_chunks % 2 == 0 and n_seq * n_chunks >= 2
    rows_q, kv_lora = qabs.shape[1:]
    n_new = rows_q // N_HEADS
    tk = chunk_pages * PAGE_SIZE
    seq_block = lambda shape: pl.BlockSpec((1,) + shape, lambda b, pt: (b, 0, 0))
    nxt_block = lambda shape: pl.BlockSpec((1,) + shape, lambda b, pt: (jnp.minimum(b + 1, n_seq - 1), 0, 0))
    grid_spec = pltpu.PrefetchScalarGridSpec(
        num_scalar_prefetch=1,
        grid=(n_seq,),
        in_specs=[seq_block(qabs.shape[1:]), seq_block(qr.shape[1:]),
                  nxt_block(qabs.shape[1:]), nxt_block(qr.shape[1:]),
                  seq_block(ckv_new.shape[1:]), seq_block(kpet_new.shape[1:]),
                  pl.BlockSpec(w_ukt.shape, lambda b, pt: (0, 0)),
                  pl.BlockSpec(memory_space=pl.ANY), pl.BlockSpec(memory_space=pl.ANY)],
        out_specs=seq_block((rows_q, kv_lora)),
        scratch_shapes=[pltpu.VMEM((2, tk, kv_lora), F32), pltpu.VMEM((2, chunk_pages, QK_ROPE, PAGE_SIZE), F32),
                        pltpu.VMEM((2, tk, kv_lora), BF16), pltpu.VMEM((2, rows_q, tk), F32),
                        pltpu.VMEM((2, w_ukt.shape[0] + rows_q, kv_lora), BF16),
                        pltpu.VMEM((rows_q, 1), F32), pltpu.VMEM((rows_q, 1), F32),
                        pltpu.VMEM((rows_q, kv_lora), F32),
                        pltpu.SemaphoreType.DMA((2,)), pltpu.SemaphoreType.DMA((2,))])
    return pl.pallas_call(
        functools.partial(_paged_attn_kernel, layer=layer, n_chunks=n_chunks, chunk_pages=chunk_pages, n_new=n_new),
        grid_spec=grid_spec,
        out_shape=jax.ShapeDtypeStruct((n_seq, rows_q, kv_lora), BF16),
        compiler_params=_params("arbitrary"),
        name="paged_attn",
    )(page_table.reshape(-1), qabs, qr, qabs, qr, ckv_new, kpet_new, w_ukt, cache_ckv, cache_kpet)


def _out_proj_kernel(x_ref, a_ref, b_ref, sga_ref, sgb_ref, p_ref, wuvb_ref, wa_ref, wb_ref, wo_ref, gffn_ref,
                     wg_ref, wup_ref, wdn_ref, gple_ref, wpg_ref, wple_ref, y_ref, *, latent_b):
    b = b_ref[...]
    if latent_b:
        b = _dot(b, wuvb_ref[...]).astype(BF16)
    merged = (sga_ref[...].astype(F32) * _dot(a_ref[...], wa_ref[...])
              + sgb_ref[...].astype(F32) * _dot(b, wb_ref[...]))
    x = x_ref[...] + _dot(merged.astype(BF16), wo_ref[...])

    def norm(t, g_ref):
        return (t * lax.rsqrt(jnp.mean(t * t, axis=-1, keepdims=True) + EPS) * g_ref[...]).astype(BF16)

    hf = norm(x, gffn_ref)
    act = jax.nn.silu(_dot(hf, wg_ref[...])) * _dot(hf, wup_ref[...])
    x = x + _dot(act.astype(BF16), wdn_ref[...])
    gate = jax.nn.sigmoid(_dot(norm(x, gple_ref), wpg_ref[...]))
    y_ref[...] = x + gate * _dot(p_ref[...].astype(BF16), wple_ref[...])


def _out_proj(x, a, b, sga, sgb, p, w, tm, latent_b):
    n, d = x.shape
    assert n % tm == 0
    row = lambda width: pl.BlockSpec((tm, width), lambda i: (i, 0))
    consts = [w[k] for k in ("w_uv_bd", "w_a", "w_b", "w_out", "g_ffn", "w_gate", "w_up", "w_down", "g_ple",
                             "w_ple_gate", "w_ple")]
    return pl.pallas_call(
        functools.partial(_out_proj_kernel, latent_b=latent_b),
        grid=(n // tm,),
        in_specs=[row(d), row(a.shape[1]), row(b.shape[1]), row(d), row(d), row(p.shape[1])]
        + [_const_spec(c.shape) for c in consts],
        out_specs=row(d),
        out_shape=jax.ShapeDtypeStruct((n, d), F32),
        compiler_params=_params("parallel"),
        name="out_proj_latent" if latent_b else "out_proj",
    )(x, a, b, sga, sgb, p, *consts)


def _head_blocked(w_nope, w_rope):
    k = w_nope.shape[0] if w_nope is not None else w_rope.shape[0]
    nope = w_nope if w_nope is not None else jnp.zeros((k, N_HEADS, QK_NOPE), F32)
    rope = w_rope if w_rope is not None else jnp.zeros((k, N_HEADS, QK_ROPE), F32)
    pad = jnp.zeros((k, N_HEADS, HEAD_BLOCK - QK_HEAD), F32)
    return jnp.concatenate([nope, rope, pad], axis=-1).reshape(k, N_HEADS * HEAD_BLOCK)


def _rot_cols(w):
    half = QK_ROPE // 2
    return jnp.concatenate([-w[..., half:], w[..., :half]], axis=-1)


def _head_gain(g_nope, g_rope_half):
    g = jnp.concatenate([g_nope, g_rope_half, g_rope_half, jnp.zeros((HEAD_BLOCK - QK_HEAD,), F32)])
    return jnp.tile(g, N_HEADS)[None, :]


def _block_diag(blocks):
    g, r, c = blocks.shape
    eye = jnp.eye(g, dtype=blocks.dtype)
    return jnp.einsum("grc,gh->grhc", blocks, eye).reshape(g * r, g * c)


def _prepare(norm_attn_g, w_in, q_norm_g, w_uq, kv_norm_g, w_uk, w_uv, q_norm_nope_g, q_norm_rope_g,
             k_norm_nope_g, k_norm_rope_g, disc, ssm_c_re, ssm_c_im, ssm_d, w_glu, w_branch_a, w_branch_b,
             w_out, norm_ffn_g, w_gate, w_up, w_down, norm_ple_g, w_ple_gate, w_ple):
    d_model = w_in.shape[0]
    ssm_w = ssm_d.shape[0]
    q_lora, kv_lora = w_uq.shape[0], w_uk.shape[0]
    o = np.cumsum([0, ssm_w, q_lora, kv_lora, QK_ROPE, d_model, d_model])
    bf = lambda a: a.astype(BF16)
    w = {}
    w["g_attn"] = norm_attn_g[None, :]
    w["w_u"] = bf(w_in[:, o[0]:o[1]])
    w["w_q"] = bf(w_in[:, o[1]:o[2]])
    w["w_kv"] = bf(w_in[:, o[2]:o[3]])
    w_kr = w_in[:, o[3]:o[4]]
    place = lambda r: jnp.pad(r, ((0, 0), (QK_NOPE, HEAD_BLOCK - QK_HEAD)))
    w["w_kr"] = bf(jnp.concatenate([place(w_kr), place(_rot_cols(w_kr))], axis=1))
    w["w_ga"] = bf(w_in[:, o[4]:o[5]])
    w["w_gb"] = bf(w_in[:, o[5]:o[6]])
    w["g_q"] = q_norm_g[None, :]
    uq_nope, uq_rope = w_uq[..., :QK_NOPE], w_uq[..., QK_NOPE:]
    w["w_uq"] = bf(jnp.concatenate([_head_blocked(uq_nope, uq_rope), _head_blocked(None, _rot_cols(uq_rope))], axis=1))
    w["g_kv"] = kv_norm_g[None, :]
    w["w_uk"] = bf(_head_blocked(w_uk, None))
    uv_t = jnp.pad(jnp.transpose(w_uv, (1, 2, 0)), ((0, 0), (0, VT_ROWS - V_HEAD), (0, 0)))
    w["w_uv_t"] = bf(uv_t.reshape(N_HEADS * VT_ROWS, kv_lora))
    w["g_qh"] = _head_gain(q_norm_nope_g, q_norm_rope_g)
    w["g_kh"] = _head_gain(k_norm_nope_g, k_norm_rope_g)
    ukt = jnp.transpose(w_uk, (1, 2, 0))
    ukt = jnp.pad(ukt, ((0, 0), (0, HEAD_BLOCK - QK_NOPE), (0, 0)))
    w["w_abs"] = bf(_block_diag(ukt))
    sel = np.zeros((N_HEADS * HEAD_BLOCK, N_HEADS * HEAD_BLOCK), np.float32)
    for h in range(N_HEADS):
        for dd in range(QK_ROPE):
            sel[h * HEAD_BLOCK + QK_NOPE + dd, h * HEAD_BLOCK + dd] = 1.0
    w["sel_rope"] = jnp.asarray(sel, BF16)
    w["w_ukt"] = bf(jnp.transpose(w_uk, (2, 1, 0)).reshape(QK_NOPE * N_HEADS, kv_lora))
    w["w_uv_bd"] = bf(_block_diag(jnp.transpose(w_uv, (1, 0, 2))))
    lb_re, lb_im, bb_re, bb_im = disc
    w["a_re"] = lb_re.reshape(1, -1)
    w["a_im"] = lb_im.reshape(1, -1)
    tr = lambda a: jnp.transpose(a, (0, 2, 1))
    gp = bb_re.shape[0] // SSM_SPLIT
    part = lambda a, p: _block_diag(tr(a[p * gp:(p + 1) * gp]))
    w["w_bu"] = bf(jnp.stack([jnp.concatenate([part(bb_re, p), part(bb_im, p)], axis=1) for p in range(SSM_SPLIT)]))
    w["w_y"] = bf(jnp.stack([jnp.concatenate([part(ssm_c_re, p), -part(ssm_c_im, p)], axis=0)
                             for p in range(SSM_SPLIT)]))
    w["ssm_d"] = ssm_d[None, :]
    w["w_glu"] = bf(w_glu)
    w["w_a"], w["w_b"], w["w_out"] = bf(w_branch_a), bf(w_branch_b), bf(w_out)
    w["g_ffn"], w["g_ple"] = norm_ffn_g[None, :], norm_ple_g[None, :]
    w["w_gate"], w["w_up"], w["w_down"] = bf(w_gate), bf(w_up), bf(w_down)
    w["w_ple_gate"], w["w_ple"] = bf(w_ple_gate), bf(w_ple)
    return w


def _rope_tables(pos):
    half = QK_ROPE // 2
    inv = ROPE_THETA ** (-jnp.arange(half, dtype=F32) / half)
    ang = pos.astype(F32)[:, None] * inv[None, :]
    cos, sin = jnp.cos(ang), jnp.sin(ang)
    n = pos.shape[0]
    ones, zeros = jnp.ones((n, QK_NOPE), F32), jnp.zeros((n, QK_NOPE), F32)
    pad = jnp.zeros((n, HEAD_BLOCK - QK_HEAD), F32)
    return (jnp.concatenate([ones, cos, cos, pad], axis=1), jnp.concatenate([zeros, sin, sin, pad], axis=1))


def _pick(n, prefs):
    for t in prefs:
        if n % t == 0:
            return t
    return n


def kernel(x_prompt, x_sample, p_prompt, p_sample, cache_ckv, cache_kpe, state_ssm_re, state_ssm_im, page_table, norm_attn_g, w_in, q_norm_g, w_uq, kv_norm_g, w_uk, w_uv, q_norm_nope_g, q_norm_rope_g, k_norm_nope_g, k_norm_rope_g, ssm_lam_re, ssm_lam_im, ssm_log_dt, ssm_b_re, ssm_b_im, ssm_c_re, ssm_c_im, ssm_d, w_glu, w_branch_a, w_branch_b, w_out, norm_ffn_g, w_gate, w_up, w_down, norm_ple_g, w_ple_gate, w_ple):
    depth = w_in.shape[0]
    batch, seq, d_model = x_prompt.shape
    n_seq, n_new, _ = x_sample.shape
    past = page_table.shape[1] * PAGE_SIZE
    groups, n_state = ssm_lam_re.shape[1:]
    ns = groups * n_state

    cos_p, sin_p = _rope_tables(jnp.arange(seq))
    cos_p, sin_p = jnp.tile(cos_p, (batch, 1)), jnp.tile(sin_p, (batch, 1))
    cos_s, sin_s = _rope_tables(past + jnp.arange(n_new))
    cos_s, sin_s = jnp.tile(cos_s, (n_seq, 1)), jnp.tile(sin_s, (n_seq, 1))

    n_seg = SUBLANES
    assert seq % n_seg == 0
    seg_len = seq // n_seg
    chains_p = batch * n_seg
    steps_p = _pick(seg_len, (16, 8, 4, 2))
    steps_s = _pick(n_new, (4, 2))
    tm_p = _pick(batch * seq, (512, 256, 128))
    tm_s = _pick(n_seq * n_new, (512, 256, 128))
    assert seq % KV_BLOCK == 0
    chunk_pages = _pick(page_table.shape[1] // 2, (32, 16, 8))

    xp = x_prompt.reshape(batch * seq, d_model)
    xs = x_sample.reshape(n_seq * n_new, d_model)
    outs = {k: [] for k in ("ckv_p", "kpe_p", "hre_p", "him_p", "ckv_s", "kpe_s", "hre_s", "him_s")}
    for i in range(depth):
        disc = _discretize(ssm_lam_re[i], ssm_lam_im[i], ssm_log_dt[i], ssm_b_re[i], ssm_b_im[i])
        w = _prepare(norm_attn_g[i], w_in[i], q_norm_g[i], w_uq[i], kv_norm_g[i], w_uk[i], w_uv[i],
                     q_norm_nope_g[i], q_norm_rope_g[i], k_norm_nope_g[i], k_norm_rope_g[i], disc,
                     ssm_c_re[i], ssm_c_im[i], ssm_d[i], w_glu[i], w_branch_a[i], w_branch_b[i], w_out[i],
                     norm_ffn_g[i], w_gate[i], w_up[i], w_down[i], norm_ple_g[i], w_ple_gate[i], w_ple[i])
        ssm_w = w["w_u"].shape[1]

        u, q, ckv, kpe, sga, sgb, k, vt = _in_proj(xp, cos_p, sin_p, w, tm_p, with_kv=True)
        u_rows = u.reshape(batch, n_seg, seg_len, ssm_w).transpose(2, 0, 1, 3).reshape(batch * seq, ssm_w)
        zero_state = jnp.zeros((chains_p, 2 * ns), F32)
        h_end = _ssm(u_rows, zero_state, w, chains_p, steps_p, emit_y=False)
        h_start = _ssm_combine(_state_from_parts(h_end), w, seg_len, n_seg)
        a_rows, h_fin = _ssm(u_rows, _state_to_parts(h_start), w, chains_p, steps_p, emit_y=True)
        h_fin = _state_from_parts(h_fin)
        a_out = a_rows.reshape(seg_len, batch, n_seg, ssm_w).transpose(1, 2, 0, 3).reshape(batch * seq, ssm_w)
        b_out = _flash(q, k, vt, batch, seq)
        xp = _out_proj(xp, a_out, b_out, sga, sgb, p_prompt[i].reshape(batch * seq, -1), w, tm_p, latent_b=False)
        h_fin = h_fin.reshape(batch, n_seg, 2 * ns)[:, -1]
        outs["ckv_p"].append(ckv.reshape(batch, seq, -1))
        outs["kpe_p"].append(kpe.reshape(batch, seq, -1))
        outs["hre_p"].append(h_fin[:, :ns].reshape(batch, groups, n_state))
        outs["him_p"].append(h_fin[:, ns:].reshape(batch, groups, n_state))

        u, q, ckv, kpe, sga, sgb = _in_proj(xs, cos_s, sin_s, w, tm_s, with_kv=False)
        u_rows = u.reshape(n_seq, n_new, ssm_w).transpose(1, 0, 2).reshape(n_seq * n_new, ssm_w)
        h0 = jnp.concatenate([state_ssm_re[i].reshape(n_seq, ns), state_ssm_im[i].reshape(n_seq, ns)], axis=1)
        a_rows, h_fin = _ssm(u_rows, _state_to_parts(h0), w, n_seq, steps_s, emit_y=True)
        h_fin = _state_from_parts(h_fin)
        a_out = a_rows.reshape(n_new, n_seq, ssm_w).transpose(1, 0, 2).reshape(n_seq * n_new, ssm_w)
        qabs, qr = _absorb(q, w)
        rows_q = n_new * N_HEADS
        kv_lora = ckv.shape[1]
        pad_new = PAGE_SIZE - n_new
        ckv_new = jnp.pad(ckv.reshape(n_seq, n_new, kv_lora), ((0, 0), (0, pad_new), (0, 0)))
        kpet_new = jnp.pad(kpe.reshape(n_seq, n_new, QK_ROPE).transpose(0, 2, 1), ((0, 0), (0, 0), (0, pad_new)))
        o_lat = _paged_attn(page_table, qabs.reshape(n_seq, rows_q, kv_lora), qr.reshape(n_seq, rows_q, HEAD_BLOCK),
                            ckv_new, kpet_new, w["w_ukt"], cache_ckv, jnp.swapaxes(cache_kpe, 2, 3), i, chunk_pages)
        b_lat = o_lat.reshape(n_seq * n_new, N_HEADS * kv_lora)
        xs = _out_proj(xs, a_out, b_lat, sga, sgb, p_sample[i].reshape(n_seq * n_new, -1), w, tm_s, latent_b=True)
        outs["ckv_s"].append(ckv.reshape(n_seq, n_new, -1))
        outs["kpe_s"].append(kpe.reshape(n_seq, n_new, -1))
        outs["hre_s"].append(h_fin[:, :ns].reshape(n_seq, groups, n_state))
        outs["him_s"].append(h_fin[:, ns:].reshape(n_seq, groups, n_state))

    st = lambda key: jnp.stack(outs[key])
    return (xp.reshape(batch, seq, d_model), xs.reshape(n_seq, n_new, d_model),
            st("ckv_p"), st("kpe_p"), st("hre_p"), st("him_p"),
            st("ckv_s"), st("kpe_s"), st("hre_s"), st("him_s"))
```

```python
import functools
import math

import numpy as np
import jax
import jax.numpy as jnp
from jax import lax
from jax.experimental import pallas as pl
from jax.experimental.pallas import tpu as pltpu

N_HEADS = 8
QK_NOPE = 64
QK_ROPE = 32
QK_HEAD = QK_NOPE + QK_ROPE
V_HEAD = 64
VT_ROWS = 80
HEAD_BLOCK = 128
ROPE_THETA = 10000.0
ATTN_SCALE = QK_HEAD ** -0.5
LOG2E = math.log2(math.e)
SSM_GROUP = 16
SSM_STATE = 64
PAGE_SIZE = 128
KV_BLOCK = 256
PART_PAGES = 8
SSM_SPLIT = 2
EPS = 1e-6

LANES = 128
SUBLANES = 8
VMEM_LIMIT_BYTES = 56 * 1024 * 1024

BF16 = jnp.bfloat16
F32 = jnp.float32
NEG_INF = float("-inf")


def _dot(a, b):
    return jnp.dot(a, b, preferred_element_type=F32)


def _dot_nt(a, b):
    return lax.dot_general(a, b, (((1,), (1,)), ((), ())), preferred_element_type=F32)


def _const_spec(shape):
    nd = len(shape)
    return pl.BlockSpec(shape, lambda *_: (0,) * nd, pipeline_mode=pl.Buffered(1))


def _params(*sem):
    return pltpu.CompilerParams(dimension_semantics=sem, vmem_limit_bytes=VMEM_LIMIT_BYTES)


def _discretize_kernel(lr_ref, li_ref, ldt_ref, br_ref, bi_ref, lbr_ref, lbi_ref, bbr_ref, bbi_ref):
    lr, li = lr_ref[...], li_ref[...]
    dt = jnp.exp(ldt_ref[...])
    mag = jnp.exp(lr * dt)
    lb_re, lb_im = mag * jnp.cos(li * dt), mag * jnp.sin(li * dt)
    num_re, num_im = lb_re - 1.0, lb_im
    den = lr * lr + li * li
    f_re = (num_re * lr + num_im * li) / den
    f_im = (num_im * lr - num_re * li) / den
    br, bi = br_ref[...], bi_ref[...]
    lbr_ref[...] = lb_re
    lbi_ref[...] = lb_im
    bbr_ref[...] = f_re * br - f_im * bi
    bbi_ref[...] = f_re * bi + f_im * br


def _discretize(lam_re, lam_im, log_dt, b_re, b_im):
    g, n, c = b_re.shape
    rep = lambda a: jnp.repeat(a, c, axis=-1)
    args = (rep(lam_re), rep(lam_im), jnp.broadcast_to(log_dt[:, None], (g, n * c)),
            b_re.reshape(g, n * c), b_im.reshape(g, n * c))
    sds = jax.ShapeDtypeStruct((g, n * c), F32)
    lbr, lbi, bbr, bbi = pl.pallas_call(_discretize_kernel, out_shape=(sds,) * 4, name="discretize")(*args)
    lb_re = lbr.reshape(g, n, c)[..., 0]
    lb_im = lbi.reshape(g, n, c)[..., 0]
    return lb_re, lb_im, bbr.reshape(g, n, c), bbi.reshape(g, n, c)


def _in_proj_kernel(x_ref, cos_ref, sin_ref, gattn_ref, wu_ref, wq_ref, wkv_ref, wkr_ref, wga_ref, wgb_ref,
                    gq_ref, wuq_ref, gkv_ref, wuk_ref, wuv_ref, gqh_ref, gkh_ref,
                    u_ref, q_ref, ckv_ref, kpe_ref, sga_ref, sgb_ref, *kv_refs):
    x = x_ref[...]
    h = x * lax.rsqrt(jnp.mean(x * x, axis=-1, keepdims=True) + EPS) * gattn_ref[...]
    hb = h.astype(BF16)
    cos, sin = cos_ref[...], sin_ref[...]
    cos8 = jnp.concatenate([cos] * N_HEADS, axis=1)
    sin8 = jnp.concatenate([sin] * N_HEADS, axis=1)

    u_ref[...] = _dot(hb, wu_ref[...])
    sga_ref[...] = jax.nn.sigmoid(_dot(hb, wga_ref[...])).astype(sga_ref.dtype)
    sgb_ref[...] = jax.nn.sigmoid(_dot(hb, wgb_ref[...])).astype(sgb_ref.dtype)

    def head_norm(t, gain):
        blocks = []
        for hd in range(N_HEADS):
            blk = t[:, hd * HEAD_BLOCK:(hd + 1) * HEAD_BLOCK]
            ssq = jnp.sum(blk * blk, axis=-1, keepdims=True)
            blocks.append(blk * lax.rsqrt(ssq * (1.0 / QK_HEAD) + EPS))
        return jnp.concatenate(blocks, axis=1) * gain

    q_lat = _dot(hb, wq_ref[...])
    qc = q_lat * lax.rsqrt(jnp.mean(q_lat * q_lat, axis=-1, keepdims=True) + EPS) * gq_ref[...]
    qq = _dot(qc.astype(BF16), wuq_ref[...])
    hw = N_HEADS * HEAD_BLOCK
    q_pre = qq[:, :hw] * cos8 + qq[:, hw:] * sin8
    q_ref[...] = (head_norm(q_pre, gqh_ref[...]) * (ATTN_SCALE * LOG2E)).astype(q_ref.dtype)

    kv_lat = _dot(hb, wkv_ref[...])
    ckv = kv_lat * lax.rsqrt(jnp.mean(kv_lat * kv_lat, axis=-1, keepdims=True) + EPS) * gkv_ref[...]
    ckv_ref[...] = ckv
    kr = _dot(hb, wkr_ref[...])
    kpe_wide = kr[:, :HEAD_BLOCK] * cos + kr[:, HEAD_BLOCK:] * sin
    kpe_ref[...] = kpe_wide[:, QK_NOPE:QK_NOPE + QK_ROPE]

    if not kv_refs:
        return
    k_ref, vt_ref = kv_refs
    cb = ckv.astype(BF16)
    k_pre = _dot(cb, wuk_ref[...]) + jnp.concatenate([kpe_wide] * N_HEADS, axis=1)
    k_ref[...] = head_norm(k_pre, gkh_ref[...]).astype(k_ref.dtype)
    vt = _dot_nt(wuv_ref[...], cb)
    ones_row = lax.broadcasted_iota(jnp.int32, vt.shape, 0) % VT_ROWS == V_HEAD
    vt = jnp.where(ones_row, 1.0, vt).astype(vt_ref.dtype)
    for c in range(vt_ref.shape[0]):
        vt_ref[c] = vt[:, c * KV_BLOCK:(c + 1) * KV_BLOCK]


def _in_proj(x, cos_t, sin_t, w, tm, with_kv):
    n, d = x.shape
    assert n % tm == 0
    row = lambda width: pl.BlockSpec((tm, width), lambda i: (i, 0))
    consts = [w[k] for k in ("g_attn", "w_u", "w_q", "w_kv", "w_kr", "w_ga", "w_gb", "g_q", "w_uq", "g_kv",
                             "w_uk", "w_uv_t", "g_qh", "g_kh")]
    hw = N_HEADS * HEAD_BLOCK
    out_shape = (jax.ShapeDtypeStruct((n, w["w_u"].shape[1]), F32),
                 jax.ShapeDtypeStruct((n, hw), BF16),
                 jax.ShapeDtypeStruct((n, w["w_kv"].shape[1]), F32),
                 jax.ShapeDtypeStruct((n, QK_ROPE), F32),
                 jax.ShapeDtypeStruct((n, d), BF16),
                 jax.ShapeDtypeStruct((n, d), BF16))
    if with_kv:
        assert tm % KV_BLOCK == 0
        out_shape += (jax.ShapeDtypeStruct((n, hw), BF16),
                      jax.ShapeDtypeStruct((n // KV_BLOCK, N_HEADS * VT_ROWS, KV_BLOCK), BF16))
    return pl.pallas_call(
        _in_proj_kernel,
        grid=(n // tm,),
        in_specs=[row(d), row(LANES), row(LANES)] + [_const_spec(c.shape) for c in consts],
        out_specs=tuple(pl.BlockSpec((tm // KV_BLOCK,) + s.shape[1:], lambda i: (i, 0, 0)) if len(s.shape) == 3
                        else row(s.shape[1]) for s in out_shape),
        out_shape=out_shape,
        compiler_params=_params("parallel"),
        name="in_proj",
    )(x, cos_t, sin_t, *consts)


def _ssm_kernel(u_ref, h0_ref, ar_ref, ai_ref, wbu_ref, wy_ref, d_ref, wglu_ref,
                *rest, chains, steps, emit_y):
    if emit_y:
        a_ref, ht_ref, bu_ref, hc_ref = rest
    else:
        ht_ref, bu_ref, hc_ref = rest
    i = pl.program_id(0)
    nh = ar_ref.shape[1] // SSM_SPLIT
    cw = u_ref.shape[1] // SSM_SPLIT

    @pl.when(i == 0)
    def _():
        hc_ref[...] = h0_ref[...]

    u = u_ref[...]
    ub = u.astype(BF16)
    for p in range(SSM_SPLIT):
        bu_ref[:, p * 2 * nh:(p + 1) * 2 * nh] = _dot(ub[:, p * cw:(p + 1) * cw], wbu_ref[p])

    def advance(prev_ref, prev_rows, rows):
        for p in range(SSM_SPLIT):
            re, im = slice(p * 2 * nh, p * 2 * nh + nh), slice(p * 2 * nh + nh, (p + 1) * 2 * nh)
            ar, ai = ar_ref[:, p * nh:(p + 1) * nh], ai_ref[:, p * nh:(p + 1) * nh]
            hr, hi = prev_ref[prev_rows, re], prev_ref[prev_rows, im]
            nr = ar * hr - ai * hi + bu_ref[rows, re]
            ni = ar * hi + ai * hr + bu_ref[rows, im]
            bu_ref[rows, re] = nr
            bu_ref[rows, im] = ni

    row_block = lambda j: pl.ds(pl.multiple_of(j * chains, SUBLANES), chains)
    advance(hc_ref, slice(None), row_block(0))

    def step(j, carry):
        advance(bu_ref, row_block(j - 1), row_block(j))
        return carry

    lax.fori_loop(1, steps, step, 0)
    last = bu_ref[(steps - 1) * chains:, :]
    hc_ref[...] = last
    ht_ref[...] = last
    if emit_y:
        hb = bu_ref[...].astype(BF16)
        y = jnp.concatenate([_dot(hb[:, p * 2 * nh:(p + 1) * 2 * nh], wy_ref[p]) for p in range(SSM_SPLIT)], axis=1)
        z = jax.nn.gelu(y + d_ref[...] * u)
        a_ref[...] = (z * jax.nn.sigmoid(_dot(z.astype(BF16), wglu_ref[...]))).astype(a_ref.dtype)


def _state_to_parts(h):
    c, n2 = h.shape
    return h.reshape(c, 2, SSM_SPLIT, n2 // (2 * SSM_SPLIT)).transpose(0, 2, 1, 3).reshape(c, n2)


def _state_from_parts(h):
    c, n2 = h.shape
    return h.reshape(c, SSM_SPLIT, 2, n2 // (2 * SSM_SPLIT)).transpose(0, 2, 1, 3).reshape(c, n2)


def _ssm(u_rows, h0, w, chains, steps, emit_y):
    n, width = u_rows.shape
    rows = chains * steps
    assert n % rows == 0
    ns2 = h0.shape[1]
    consts = [w[k] for k in ("a_re", "a_im", "w_bu", "w_y", "ssm_d", "w_glu")]
    ht_sds = jax.ShapeDtypeStruct((chains, ns2), F32)
    ht_spec = pl.BlockSpec((chains, ns2), lambda i: (0, 0))
    if emit_y:
        out_shape = (jax.ShapeDtypeStruct((n, width), BF16), ht_sds)
        out_specs = (pl.BlockSpec((rows, width), lambda i: (i, 0)), ht_spec)
    else:
        out_shape, out_specs = ht_sds, ht_spec
    return pl.pallas_call(
        functools.partial(_ssm_kernel, chains=chains, steps=steps, emit_y=emit_y),
        grid=(n // rows,),
        in_specs=[pl.BlockSpec((rows, width), lambda i: (i, 0)), _const_spec(h0.shape)]
        + [_const_spec(c.shape) for c in consts],
        out_specs=out_specs,
        out_shape=out_shape,
        scratch_shapes=[pltpu.VMEM((rows, ns2), F32), pltpu.VMEM((chains, ns2), F32)],
        compiler_params=_params("arbitrary"),
        name="ssm_scan" if emit_y else "ssm_end_state",
    )(u_rows, h0, *consts)


def _ssm_combine_kernel(hend_ref, ar_ref, ai_ref, h0_ref, *, seg_len, n_seg):
    ns = ar_ref.shape[1]
    pr, pi = ar_ref[...], ai_ref[...]
    qr, qi = None, None
    e = seg_len
    while e:
        if e & 1:
            qr, qi = (pr, pi) if qr is None else (qr * pr - qi * pi, qr * pi + qi * pr)
        e >>= 1
        if e:
            pr, pi = pr * pr - pi * pi, 2.0 * pr * pi
    er, ei = hend_ref[:, :ns], hend_ref[:, ns:]
    seg = lax.broadcasted_iota(jnp.int32, er.shape, 0) % n_seg
    hr, hi = jnp.zeros_like(er), jnp.zeros_like(ei)
    for s in range(1, n_seg):
        xr = qr * hr - qi * hi + er
        xi = qr * hi + qi * hr + ei
        hr = jnp.where(seg == s, pltpu.roll(xr, 1, 0), hr)
        hi = jnp.where(seg == s, pltpu.roll(xi, 1, 0), hi)
    h0_ref[:, :ns] = hr
    h0_ref[:, ns:] = hi


def _ssm_combine(hend, w, seg_len, n_seg):
    return pl.pallas_call(
        functools.partial(_ssm_combine_kernel, seg_len=seg_len, n_seg=n_seg),
        out_shape=jax.ShapeDtypeStruct(hend.shape, F32),
        name="ssm_combine",
    )(hend, w["a_re"], w["a_im"])


def _flash_kernel(q_ref, k_ref, vt_ref, o_ref, s_buf, p_buf, acc_ref, m_ref, alpha_ref):
    i = pl.program_id(2)
    tq = q_ref.shape[0]
    heads = range(2)
    q = [q_ref[:, hh * HEAD_BLOCK:(hh + 1) * HEAD_BLOCK] for hh in heads]

    def scores(t, slot):
        rows = pl.ds(pl.multiple_of(t * KV_BLOCK, KV_BLOCK), KV_BLOCK)
        for hh in heads:
            s_buf[slot, hh] = _dot_nt(k_ref[rows, hh * HEAD_BLOCK:(hh + 1) * HEAD_BLOCK], q[hh])

    def softmax(slot, hh, first_visible):
        s = s_buf[slot, hh]
        if first_visible is not None:
            ki = lax.broadcasted_iota(jnp.int32, s.shape, 0)
            qi = lax.broadcasted_iota(jnp.int32, s.shape, 1)
            s = jnp.where(ki + first_visible <= qi, s, NEG_INF)
        m = m_ref[hh]
        m_new = jnp.maximum(m, jnp.max(s, axis=0, keepdims=True))
        m_ref[hh] = m_new
        p_buf[slot, hh] = jnp.exp2(s - m_new).astype(BF16)
        alpha_ref[hh] = jnp.exp2(m - m_new)

    def values(t, slot, hh):
        acc_ref[hh] = (alpha_ref[hh] * acc_ref[hh]
                       + _dot(vt_ref[t, hh * VT_ROWS:(hh + 1) * VT_ROWS, :], p_buf[slot, hh]))

    def step(t, slot, first_visible=None, last=False):
        if not last:
            scores(t + 1, 1 - slot)
        for hh in heads:
            values(jnp.maximum(t - 1, 0), 1 - slot, hh)
        for hh in heads:
            softmax(slot, hh, first_visible)

    scores(0, 0)
    p_buf[1] = jnp.zeros(p_buf.shape[1:], BF16)
    alpha_ref[...] = jnp.ones(alpha_ref.shape, F32)
    m_ref[...] = jnp.full(m_ref.shape, NEG_INF, F32)
    acc_ref[...] = jnp.zeros(acc_ref.shape, F32)

    def two_steps(u, carry):
        step(2 * u, 0)
        step(2 * u + 1, 1)
        return carry

    lax.fori_loop(0, i, two_steps, 0)
    step(2 * i, 0, first_visible=0)
    step(2 * i + 1, 1, first_visible=KV_BLOCK, last=True)
    for hh in heads:
        values(2 * i + 1, 1, hh)
    out = [acc_ref[hh, :V_HEAD] / acc_ref[hh, V_HEAD:V_HEAD + 1] for hh in heads]
    o_ref[...] = jnp.concatenate(out, axis=0).T.astype(o_ref.dtype)


def _flash(q, k, vt, batch, seq):
    tq = 2 * KV_BLOCK
    nq = seq // tq
    pairs = N_HEADS // 2
    return pl.pallas_call(
        _flash_kernel,
        grid=(batch, pairs, nq),
        in_specs=[pl.BlockSpec((tq, 2 * HEAD_BLOCK), lambda b, p, i: (b * nq + i, p)),
                  pl.BlockSpec((seq, 2 * HEAD_BLOCK), lambda b, p, i: (b, p)),
                  pl.BlockSpec((seq // KV_BLOCK, 2 * VT_ROWS, KV_BLOCK), lambda b, p, i: (b, p, 0))],
        out_specs=pl.BlockSpec((tq, 2 * V_HEAD), lambda b, p, i: (b * nq + i, p)),
        out_shape=jax.ShapeDtypeStruct((batch * seq, N_HEADS * V_HEAD), BF16),
        scratch_shapes=[pltpu.VMEM((2, 2, KV_BLOCK, tq), F32), pltpu.VMEM((2, 2, KV_BLOCK, tq), BF16),
                        pltpu.VMEM((2, VT_ROWS, tq), F32), pltpu.VMEM((2, 1, tq), F32),
                        pltpu.VMEM((2, 1, tq), F32)],
        compiler_params=_params("parallel", "parallel", "arbitrary"),
        name="flash",
    )(q, k, vt)


def _absorb_kernel(q_ref, gk_ref, wabs_ref, sel_ref, qabs_ref, qr_ref):
    qg = (q_ref[...].astype(F32) * gk_ref[...]).astype(BF16)
    qabs_ref[...] = _dot(qg, wabs_ref[...]).astype(qabs_ref.dtype)
    qr_ref[...] = _dot(qg, sel_ref[...]).astype(qr_ref.dtype)


def _absorb(q, w):
    n = q.shape[0]
    return pl.pallas_call(
        _absorb_kernel,
        out_shape=(jax.ShapeDtypeStruct((n, w["w_abs"].shape[1]), BF16),
                   jax.ShapeDtypeStruct((n, w["sel_rope"].shape[1]), BF16)),
        compiler_params=_params(),
        name="absorb",
    )(q, w["g_kh"], w["w_abs"], w["sel_rope"])


def _paged_attn_kernel(pt_ref, qabs_ref, qr_ref, qabs_nx_ref, qr_nx_ref, ckvn_ref, kpetn_ref, wukt_ref,
                       ckv_hbm, kpet_hbm, o_ref, ckv_buf, kpe_buf, cb_buf, sc_buf, lhs_ref, m_ref, l_ref, acc_ref,
                       sem_c, sem_k, *, layer, n_chunks, chunk_pages, n_new):
    b = pl.program_id(0)
    total = pl.num_programs(0) * n_chunks
    n_k = wukt_ref.shape[0]
    rows_q = qabs_ref.shape[1]
    tk = chunk_pages * PAGE_SIZE
    parts = chunk_pages // PART_PAGES
    qr = (qr_ref[0][:, :QK_ROPE], qr_nx_ref[0][:, :QK_ROPE])

    def start_chunk(g, slt):
        g = jnp.minimum(g, total - 1)
        for p in range(chunk_pages):
            page = pt_ref[g * chunk_pages + p]
            dst = pl.ds(p * PAGE_SIZE, PAGE_SIZE)
            pltpu.make_async_copy(ckv_hbm.at[layer, page], ckv_buf.at[slt, dst, :], sem_c.at[slt]).start()
            pltpu.make_async_copy(kpet_hbm.at[layer, page], kpe_buf.at[slt, p], sem_k.at[slt]).start()

    def wait_chunk(slt):
        pltpu.make_async_copy(ckv_buf.at[slt], ckv_buf.at[slt], sem_c.at[slt]).wait()
        pltpu.make_async_copy(kpe_buf.at[slt], kpe_buf.at[slt], sem_k.at[slt]).wait()

    def scaled_scores(which, cb, kpet):
        keys = cb.shape[0]
        both = _dot_nt(lhs_ref[which], cb)
        kt = both[:n_k]
        ssq = jnp.sum((kt * kt).reshape(QK_NOPE, N_HEADS, keys), axis=0)
        ssq = ssq + jnp.sum(kpet * kpet, axis=0, keepdims=True)
        r = lax.rsqrt(ssq * (1.0 / QK_HEAD) + EPS)
        sc = both[n_k:] + _dot(qr[which], kpet.astype(BF16))
        return (sc.reshape(n_new, N_HEADS, keys) * r[None]).reshape(rows_q, keys)

    def stage_a(slt, which, part):
        pages = range(part * PART_PAGES, (part + 1) * PART_PAGES)
        rows = slice(pages[0] * PAGE_SIZE, (pages[-1] + 1) * PAGE_SIZE)
        cb = ckv_buf[slt, rows, :].astype(BF16)
        cb_buf[slt, rows, :] = cb
        kpet = jnp.concatenate([kpe_buf[slt, p] for p in pages], axis=1)
        sc_buf[slt, :, rows] = scaled_scores(which, cb, kpet)

    def stage_b(sc, cb):
        m_old = m_ref[...]
        m_new = jnp.maximum(m_old, jnp.max(sc, axis=1, keepdims=True))
        p = jnp.exp2(sc - m_new)
        alpha = jnp.exp2(m_old - m_new)
        l_ref[...] = alpha * l_ref[...] + jnp.sum(p, axis=1, keepdims=True)
        acc_ref[...] = alpha * acc_ref[...] + _dot(p.astype(BF16), cb)
        m_ref[...] = m_new

    @pl.when(b == 0)
    def _():
        for which in range(2):
            lhs_ref[which, :n_k, :] = wukt_ref[...]

    lhs_ref[0, n_k:, :] = qabs_ref[0]
    lhs_ref[1, n_k:, :] = qabs_nx_ref[0]

    @pl.when(b == 0)
    def _():
        start_chunk(0, 0)
        start_chunk(1, 1)
        wait_chunk(0)
        for part in range(parts):
            stage_a(0, 0, part)

    m_ref[...] = jnp.full(m_ref.shape, NEG_INF, F32)
    l_ref[...] = jnp.zeros(l_ref.shape, F32)
    acc_ref[...] = jnp.zeros(acc_ref.shape, F32)

    for c in range(n_chunks):
        g = b * n_chunks + c
        slot, nxt = c % 2, (c + 1) % 2
        which = 0 if c + 1 < n_chunks else 1

        wait_chunk(nxt)
        start_chunk(g + 2, slot)
        stage_a(nxt, which, 0)
        stage_b(sc_buf[slot], cb_buf[slot])
        for part in range(1, parts):
            stage_a(nxt, which, part)

    @pl.when(b == pl.num_programs(0) - 1)
    def _():
        wait_chunk((n_chunks - 1) % 2)

    tkn = ckvn_ref.shape[1]
    cbn = ckvn_ref[0].astype(BF16)
    sc = scaled_scores(0, cbn, kpetn_ref[0])
    key = lax.broadcasted_iota(jnp.int32, (rows_q, tkn), 1)
    tok = lax.broadcasted_iota(jnp.int32, (rows_q, tkn), 0) // N_HEADS
    stage_b(jnp.where(key <= tok, sc, NEG_INF), cbn)
    o_ref[0] = (acc_ref[...] / l_ref[...]).astype(o_ref.dtype)


def _paged_attn(page_table, qabs, qr, ckv_new, kpet_new, w_ukt, cache_ckv, cache_kpet, layer, chunk_pages):
    n_seq, pages_per_seq = page_table.shape
    assert pages_per_seq % chunk_pages == 0 and chunk_pages % PART_PAGES == 0
    n_chunks = pages_per_seq // chunk_pages
    assert n_chunks % 2 == 0 and n_seq * n_chunks >= 2
    rows_q, kv_lora = qabs.shape[1:]
    n_new = rows_q // N_HEADS
    tk = chunk_pages * PAGE_SIZE
    seq_block = lambda shape: pl.BlockSpec((1,) + shape, lambda b, pt: (b, 0, 0))
    nxt_block = lambda shape: pl.BlockSpec((1,) + shape, lambda b, pt: (jnp.minimum(b + 1, n_seq - 1), 0, 0))
    grid_spec = pltpu.PrefetchScalarGridSpec(
        num_scalar_prefetch=1,
        grid=(n_seq,),
        in_specs=[seq_block(qabs.shape[1:]), seq_block(qr.shape[1:]),
                  nxt_block(qabs.shape[1:]), nxt_block(qr.shape[1:]),
                  seq_block(ckv_new.shape[1:]), seq_block(kpet_new.shape[1:]),
                  pl.BlockSpec(w_ukt.shape, lambda b, pt: (0, 0)),
                  pl.BlockSpec(memory_space=pl.ANY), pl.BlockSpec(memory_space=pl.ANY)],
        out_specs=seq_block((rows_q, kv_lora)),
        scratch_shapes=[pltpu.VMEM((2, tk, kv_lora), F32), pltpu.VMEM((2, chunk_pages, QK_ROPE, PAGE_SIZE), F32),
                        pltpu.VMEM((2, tk, kv_lora), BF16), pltpu.VMEM((2, rows_q, tk), F32),
                        pltpu.VMEM((2, w_ukt.shape[0] + rows_q, kv_lora), BF16),
                        pltpu.VMEM((rows_q, 1), F32), pltpu.VMEM((rows_q, 1), F32),
                        pltpu.VMEM((rows_q, kv_lora), F32),
                        pltpu.SemaphoreType.DMA((2,)), pltpu.SemaphoreType.DMA((2,))])
    return pl.pallas_call(
        functools.partial(_paged_attn_kernel, layer=layer, n_chunks=n_chunks, chunk_pages=chunk_pages, n_new=n_new),
        grid_spec=grid_spec,
        out_shape=jax.ShapeDtypeStruct((n_seq, rows_q, kv_lora), BF16),
        compiler_params=_params("arbitrary"),
        name="paged_attn",
    )(page_table.reshape(-1), qabs, qr, qabs, qr, ckv_new, kpet_new, w_ukt, cache_ckv, cache_kpet)


def _out_proj_kernel(x_ref, a_ref, b_ref, sga_ref, sgb_ref, p_ref, wuvb_ref, wa_ref, wb_ref, wo_ref, gffn_ref,
                     wg_ref, wup_ref, wdn_ref, gple_ref, wpg_ref, wple_ref, y_ref, *, latent_b):
    b = b_ref[...]
    if latent_b:
        b = _dot(b, wuvb_ref[...]).astype(BF16)
    merged = (sga_ref[...].astype(F32) * _dot(a_ref[...], wa_ref[...])
              + sgb_ref[...].astype(F32) * _dot(b, wb_ref[...]))
    x = x_ref[...] + _dot(merged.astype(BF16), wo_ref[...])

    def norm(t, g_ref):
        return (t * lax.rsqrt(jnp.mean(t * t, axis=-1, keepdims=True) + EPS) * g_ref[...]).astype(BF16)

    hf = norm(x, gffn_ref)
    act = jax.nn.silu(_dot(hf, wg_ref[...])) * _dot(hf, wup_ref[...])
    x = x + _dot(act.astype(BF16), wdn_ref[...])
    gate = jax.nn.sigmoid(_dot(norm(x, gple_ref), wpg_ref[...]))
    y_ref[...] = x + gate * _dot(p_ref[...].astype(BF16), wple_ref[...])


def _out_proj(x, a, b, sga, sgb, p, w, tm, latent_b):
    n, d = x.shape
    assert n % tm == 0
    row = lambda width: pl.BlockSpec((tm, width), lambda i: (i, 0))
    consts = [w[k] for k in ("w_uv_bd", "w_a", "w_b", "w_out", "g_ffn", "w_gate", "w_up", "w_down", "g_ple",
                             "w_ple_gate", "w_ple")]
    return pl.pallas_call(
        functools.partial(_out_proj_kernel, latent_b=latent_b),
        grid=(n // tm,),
        in_specs=[row(d), row(a.shape[1]), row(b.shape[1]), row(d), row(d), row(p.shape[1])]
        + [_const_spec(c.shape) for c in consts],
        out_specs=row(d),
        out_shape=jax.ShapeDtypeStruct((n, d), F32),
        compiler_params=_params("parallel"),
        name="out_proj_latent" if latent_b else "out_proj",
    )(x, a, b, sga, sgb, p, *consts)


def _head_blocked(w_nope, w_rope):
    k = w_nope.shape[0] if w_nope is not None else w_rope.shape[0]
    nope = w_nope if w_nope is not None else jnp.zeros((k, N_HEADS, QK_NOPE), F32)
    rope = w_rope if w_rope is not None else jnp.zeros((k, N_HEADS, QK_ROPE), F32)
    pad = jnp.zeros((k, N_HEADS, HEAD_BLOCK - QK_HEAD), F32)
    return jnp.concatenate([nope, rope, pad], axis=-1).reshape(k, N_HEADS * HEAD_BLOCK)


def _rot_cols(w):
    half = QK_ROPE // 2
    return jnp.concatenate([-w[..., half:], w[..., :half]], axis=-1)


def _head_gain(g_nope, g_rope_half):
    g = jnp.concatenate([g_nope, g_rope_half, g_rope_half, jnp.zeros((HEAD_BLOCK - QK_HEAD,), F32)])
    return jnp.tile(g, N_HEADS)[None, :]


def _block_diag(blocks):
    g, r, c = blocks.shape
    eye = jnp.eye(g, dtype=blocks.dtype)
    return jnp.einsum("grc,gh->grhc", blocks, eye).reshape(g * r, g * c)


def _prepare(norm_attn_g, w_in, q_norm_g, w_uq, kv_norm_g, w_uk, w_uv, q_norm_nope_g, q_norm_rope_g,
             k_norm_nope_g, k_norm_rope_g, disc, ssm_c_re, ssm_c_im, ssm_d, w_glu, w_branch_a, w_branch_b,
             w_out, norm_ffn_g, w_gate, w_up, w_down, norm_ple_g, w_ple_gate, w_ple):
    d_model = w_in.shape[0]
    ssm_w = ssm_d.shape[0]
    q_lora, kv_lora = w_uq.shape[0], w_uk.shape[0]
    o = np.cumsum([0, ssm_w, q_lora, kv_lora, QK_ROPE, d_model, d_model])
    bf = lambda a: a.astype(BF16)
    w = {}
    w["g_attn"] = norm_attn_g[None, :]
    w["w_u"] = bf(w_in[:, o[0]:o[1]])
    w["w_q"] = bf(w_in[:, o[1]:o[2]])
    w["w_kv"] = bf(w_in[:, o[2]:o[3]])
    w_kr = w_in[:, o[3]:o[4]]
    place = lambda r: jnp.pad(r, ((0, 0), (QK_NOPE, HEAD_BLOCK - QK_HEAD)))
    w["w_kr"] = bf(jnp.concatenate([place(w_kr), place(_rot_cols(w_kr))], axis=1))
    w["w_ga"] = bf(w_in[:, o[4]:o[5]])
    w["w_gb"] = bf(w_in[:, o[5]:o[6]])
    w["g_q"] = q_norm_g[None, :]
    uq_nope, uq_rope = w_uq[..., :QK_NOPE], w_uq[..., QK_NOPE:]
    w["w_uq"] = bf(jnp.concatenate([_head_blocked(uq_nope, uq_rope), _head_blocked(None, _rot_cols(uq_rope))], axis=1))
    w["g_kv"] = kv_norm_g[None, :]
    w["w_uk"] = bf(_head_blocked(w_uk, None))
    uv_t = jnp.pad(jnp.transpose(w_uv, (1, 2, 0)), ((0, 0), (0, VT_ROWS - V_HEAD), (0, 0)))
    w["w_uv_t"] = bf(uv_t.reshape(N_HEADS * VT_ROWS, kv_lora))
    w["g_qh"] = _head_gain(q_norm_nope_g, q_norm_rope_g)
    w["g_kh"] = _head_gain(k_norm_nope_g, k_norm_rope_g)
    ukt = jnp.transpose(w_uk, (1, 2, 0))
    ukt = jnp.pad(ukt, ((0, 0), (0, HEAD_BLOCK - QK_NOPE), (0, 0)))
    w["w_abs"] = bf(_block_diag(ukt))
    sel = np.zeros((N_HEADS * HEAD_BLOCK, N_HEADS * HEAD_BLOCK), np.float32)
    for h in range(N_HEADS):
        for dd in range(QK_ROPE):
            sel[h * HEAD_BLOCK + QK_NOPE + dd, h * HEAD_BLOCK + dd] = 1.0
    w["sel_rope"] = jnp.asarray(sel, BF16)
    w["w_ukt"] = bf(jnp.transpose(w_uk, (2, 1, 0)).reshape(QK_NOPE * N_HEADS, kv_lora))
    w["w_uv_bd"] = bf(_block_diag(jnp.transpose(w_uv, (1, 0, 2))))
    lb_re, lb_im, bb_re, bb_im = disc
    w["a_re"] = lb_re.reshape(1, -1)
    w["a_im"] = lb_im.reshape(1, -1)
    tr = lambda a: jnp.transpose(a, (0, 2, 1))
    gp = bb_re.shape[0] // SSM_SPLIT
    part = lambda a, p: _block_diag(tr(a[p * gp:(p + 1) * gp]))
    w["w_bu"] = bf(jnp.stack([jnp.concatenate([part(bb_re, p), part(bb_im, p)], axis=1) for p in range(SSM_SPLIT)]))
    w["w_y"] = bf(jnp.stack([jnp.concatenate([part(ssm_c_re, p), -part(ssm_c_im, p)], axis=0)
                             for p in range(SSM_SPLIT)]))
    w["ssm_d"] = ssm_d[None, :]
    w["w_glu"] = bf(w_glu)
    w["w_a"], w["w_b"], w["w_out"] = bf(w_branch_a), bf(w_branch_b), bf(w_out)
    w["g_ffn"], w["g_ple"] = norm_ffn_g[None, :], norm_ple_g[None, :]
    w["w_gate"], w["w_up"], w["w_down"] = bf(w_gate), bf(w_up), bf(w_down)
    w["w_ple_gate"], w["w_ple"] = bf(w_ple_gate), bf(w_ple)
    return w


def _rope_tables(pos):
    half = QK_ROPE // 2
    inv = ROPE_THETA ** (-jnp.arange(half, dtype=F32) / half)
    ang = pos.astype(F32)[:, None] * inv[None, :]
    cos, sin = jnp.cos(ang), jnp.sin(ang)
    n = pos.shape[0]
    ones, zeros = jnp.ones((n, QK_NOPE), F32), jnp.zeros((n, QK_NOPE), F32)
    pad = jnp.zeros((n, HEAD_BLOCK - QK_HEAD), F32)
    return (jnp.concatenate([ones, cos, cos, pad], axis=1), jnp.concatenate([zeros, sin, sin, pad], axis=1))


def _pick(n, prefs):
    for t in prefs:
        if n % t == 0:
            return t
    return n


def kernel(x_prompt, x_sample, p_prompt, p_sample, cache_ckv, cache_kpe, state_ssm_re, state_ssm_im, page_table, norm_attn_g, w_in, q_norm_g, w_uq, kv_norm_g, w_uk, w_uv, q_norm_nope_g, q_norm_rope_g, k_norm_nope_g, k_norm_rope_g, ssm_lam_re, ssm_lam_im, ssm_log_dt, ssm_b_re, ssm_b_im, ssm_c_re, ssm_c_im, ssm_d, w_glu, w_branch_a, w_branch_b, w_out, norm_ffn_g, w_gate, w_up, w_down, norm_ple_g, w_ple_gate, w_ple):
    depth = w_in.shape[0]
    batch, seq, d_model = x_prompt.shape
    n_seq, n_new, _ = x_sample.shape
    past = page_table.shape[1] * PAGE_SIZE
    groups, n_state = ssm_lam_re.shape[1:]
    ns = groups * n_state

    cos_p, sin_p = _rope_tables(jnp.arange(seq))
    cos_p, sin_p = jnp.tile(cos_p, (batch, 1)), jnp.tile(sin_p, (batch, 1))
    cos_s, sin_s = _rope_tables(past + jnp.arange(n_new))
    cos_s, sin_s = jnp.tile(cos_s, (n_seq, 1)), jnp.tile(sin_s, (n_seq, 1))

    n_seg = SUBLANES
    assert seq % n_seg == 0
    seg_len = seq // n_seg
    chains_p = batch * n_seg
    steps_p = _pick(seg_len, (16, 8, 4, 2))
    steps_s = _pick(n_new, (4, 2))
    tm_p = _pick(batch * seq, (512, 256, 128))
    tm_s = _pick(n_seq * n_new, (512, 256, 128))
    assert seq % (2 * KV_BLOCK) == 0
    chunk_pages = _pick(page_table.shape[1] // 2, (32, 16, 8))

    xp = x_prompt.reshape(batch * seq, d_model)
    xs = x_sample.reshape(n_seq * n_new, d_model)
    outs = {k: [] for k in ("ckv_p", "kpe_p", "hre_p", "him_p", "ckv_s", "kpe_s", "hre_s", "him_s")}
    for i in range(depth):
        disc = _discretize(ssm_lam_re[i], ssm_lam_im[i], ssm_log_dt[i], ssm_b_re[i], ssm_b_im[i])
        w = _prepare(norm_attn_g[i], w_in[i], q_norm_g[i], w_uq[i], kv_norm_g[i], w_uk[i], w_uv[i],
                     q_norm_nope_g[i], q_norm_rope_g[i], k_norm_nope_g[i], k_norm_rope_g[i], disc,
                     ssm_c_re[i], ssm_c_im[i], ssm_d[i], w_glu[i], w_branch_a[i], w_branch_b[i], w_out[i],
                     norm_ffn_g[i], w_gate[i], w_up[i], w_down[i], norm_ple_g[i], w_ple_gate[i], w_ple[i])
        ssm_w = w["w_u"].shape[1]

        u, q, ckv, kpe, sga, sgb, k, vt = _in_proj(xp, cos_p, sin_p, w, tm_p, with_kv=True)
        u_rows = u.reshape(batch, n_seg, seg_len, ssm_w).transpose(2, 0, 1, 3).reshape(batch * seq, ssm_w)
        zero_state = jnp.zeros((chains_p, 2 * ns), F32)
        h_end = _ssm(u_rows, zero_state, w, chains_p, steps_p, emit_y=False)
        h_start = _ssm_combine(_state_from_parts(h_end), w, seg_len, n_seg)
        a_rows, h_fin = _ssm(u_rows, _state_to_parts(h_start), w, chains_p, steps_p, emit_y=True)
        h_fin = _state_from_parts(h_fin)
        a_out = a_rows.reshape(seg_len, batch, n_seg, ssm_w).transpose(1, 2, 0, 3).reshape(batch * seq, ssm_w)
        b_out = _flash(q, k, vt, batch, seq)
        xp = _out_proj(xp, a_out, b_out, sga, sgb, p_prompt[i].reshape(batch * seq, -1), w, tm_p, latent_b=False)
        h_fin = h_fin.reshape(batch, n_seg, 2 * ns)[:, -1]
        outs["ckv_p"].append(ckv.reshape(batch, seq, -1))
        outs["kpe_p"].append(kpe.reshape(batch, seq, -1))
        outs["hre_p"].append(h_fin[:, :ns].reshape(batch, groups, n_state))
        outs["him_p"].append(h_fin[:, ns:].reshape(batch, groups, n_state))

        u, q, ckv, kpe, sga, sgb = _in_proj(xs, cos_s, sin_s, w, tm_s, with_kv=False)
        u_rows = u.reshape(n_seq, n_new, ssm_w).transpose(1, 0, 2).reshape(n_seq * n_new, ssm_w)
        h0 = jnp.concatenate([state_ssm_re[i].reshape(n_seq, ns), state_ssm_im[i].reshape(n_seq, ns)], axis=1)
        a_rows, h_fin = _ssm(u_rows, _state_to_parts(h0), w, n_seq, steps_s, emit_y=True)
        h_fin = _state_from_parts(h_fin)
        a_out = a_rows.reshape(n_new, n_seq, ssm_w).transpose(1, 0, 2).reshape(n_seq * n_new, ssm_w)
        qabs, qr = _absorb(q, w)
        rows_q = n_new * N_HEADS
        kv_lora = ckv.shape[1]
        pad_new = PAGE_SIZE - n_new
        ckv_new = jnp.pad(ckv.reshape(n_seq, n_new, kv_lora), ((0, 0), (0, pad_new), (0, 0)))
        kpet_new = jnp.pad(kpe.reshape(n_seq, n_new, QK_ROPE).transpose(0, 2, 1), ((0, 0), (0, 0), (0, pad_new)))
        o_lat = _paged_attn(page_table, qabs.reshape(n_seq, rows_q, kv_lora), qr.reshape(n_seq, rows_q, HEAD_BLOCK),
                            ckv_new, kpet_new, w["w_ukt"], cache_ckv, jnp.swapaxes(cache_kpe, 2, 3), i, chunk_pages)
        b_lat = o_lat.reshape(n_seq * n_new, N_HEADS * kv_lora)
        xs = _out_proj(xs, a_out, b_lat, sga, sgb, p_sample[i].reshape(n_seq * n_new, -1), w, tm_s, latent_b=True)
        outs["ckv_s"].append(ckv.reshape(n_seq, n_new, -1))
        outs["kpe_s"].append(kpe.reshape(n_seq, n_new, -1))
        outs["hre_s"].append(h_fin[:, :ns].reshape(n_seq, groups, n_state))
        outs["him_s"].append(h_fin[:, ns:].reshape(n_seq, groups, n_state))

    st = lambda key: jnp.stack(outs[key])
    return (xp.reshape(batch, seq, d_model), xs.reshape(n_seq, n_new, d_model),
            st("ckv_p"), st("kpe_p"), st("hre_p"), st("him_p"),
            st("ckv_s"), st("kpe_s"), st("hre_s"), st("him_s"))
```

```python
import functools
import math

import numpy as np
import jax
import jax.numpy as jnp
from jax import lax
from jax.experimental import pallas as pl
from jax.experimental.pallas import tpu as pltpu

N_HEADS = 8
QK_NOPE = 64
QK_ROPE = 32
QK_HEAD = QK_NOPE + QK_ROPE
V_HEAD = 64
VT_ROWS = 80
HEAD_BLOCK = 128
ROPE_THETA = 10000.0
ATTN_SCALE = QK_HEAD ** -0.5
LOG2E = math.log2(math.e)
SSM_GROUP = 16
SSM_STATE = 64
PAGE_SIZE = 128
KV_BLOCK = 256
FLASH_HEADS = 2
PART_PAGES = 16
SSM_SPLIT = 2
EPS = 1e-6

LANES = 128
SUBLANES = 8
VMEM_LIMIT_BYTES = 56 * 1024 * 1024

BF16 = jnp.bfloat16
F32 = jnp.float32
NEG_INF = float("-inf")


def _dot(a, b):
    return jnp.dot(a, b, preferred_element_type=F32)


def _dot_nt(a, b):
    return lax.dot_general(a, b, (((1,), (1,)), ((), ())), preferred_element_type=F32)


def _const_spec(shape):
    nd = len(shape)
    return pl.BlockSpec(shape, lambda *_: (0,) * nd, pipeline_mode=pl.Buffered(1))


def _params(*sem):
    return pltpu.CompilerParams(dimension_semantics=sem, vmem_limit_bytes=VMEM_LIMIT_BYTES)


def _discretize_kernel(lr_ref, li_ref, ldt_ref, br_ref, bi_ref, lbr_ref, lbi_ref, bbr_ref, bbi_ref):
    lr, li = lr_ref[...], li_ref[...]
    dt = jnp.exp(ldt_ref[...])
    mag = jnp.exp(lr * dt)
    lb_re, lb_im = mag * jnp.cos(li * dt), mag * jnp.sin(li * dt)
    num_re, num_im = lb_re - 1.0, lb_im
    den = lr * lr + li * li
    f_re = (num_re * lr + num_im * li) / den
    f_im = (num_im * lr - num_re * li) / den
    br, bi = br_ref[...], bi_ref[...]
    lbr_ref[...] = lb_re
    lbi_ref[...] = lb_im
    bbr_ref[...] = f_re * br - f_im * bi
    bbi_ref[...] = f_re * bi + f_im * br


def _discretize(lam_re, lam_im, log_dt, b_re, b_im):
    g, n, c = b_re.shape
    rep = lambda a: jnp.repeat(a, c, axis=-1)
    args = (rep(lam_re), rep(lam_im), jnp.broadcast_to(log_dt[:, None], (g, n * c)),
            b_re.reshape(g, n * c), b_im.reshape(g, n * c))
    sds = jax.ShapeDtypeStruct((g, n * c), F32)
    lbr, lbi, bbr, bbi = pl.pallas_call(_discretize_kernel, out_shape=(sds,) * 4, name="discretize")(*args)
    lb_re = lbr.reshape(g, n, c)[..., 0]
    lb_im = lbi.reshape(g, n, c)[..., 0]
    return lb_re, lb_im, bbr.reshape(g, n, c), bbi.reshape(g, n, c)


def _in_proj_kernel(x_ref, cos_ref, sin_ref, gattn_ref, wu_ref, wq_ref, wkv_ref, wkr_ref, wga_ref, wgb_ref,
                    gq_ref, wuq_ref, gkv_ref, wuk_ref, wuv_ref, gqh_ref, gkh_ref,
                    u_ref, q_ref, ckv_ref, kpe_ref, sga_ref, sgb_ref, *kv_refs):
    x = x_ref[...]
    h = x * lax.rsqrt(jnp.mean(x * x, axis=-1, keepdims=True) + EPS) * gattn_ref[...]
    hb = h.astype(BF16)
    cos, sin = cos_ref[...], sin_ref[...]
    cos8 = jnp.concatenate([cos] * N_HEADS, axis=1)
    sin8 = jnp.concatenate([sin] * N_HEADS, axis=1)

    u_ref[...] = _dot(hb, wu_ref[...])
    sga_ref[...] = jax.nn.sigmoid(_dot(hb, wga_ref[...])).astype(sga_ref.dtype)
    sgb_ref[...] = jax.nn.sigmoid(_dot(hb, wgb_ref[...])).astype(sgb_ref.dtype)

    def head_norm(t, gain):
        blocks = []
        for hd in range(N_HEADS):
            blk = t[:, hd * HEAD_BLOCK:(hd + 1) * HEAD_BLOCK]
            ssq = jnp.sum(blk * blk, axis=-1, keepdims=True)
            blocks.append(blk * lax.rsqrt(ssq * (1.0 / QK_HEAD) + EPS))
        return jnp.concatenate(blocks, axis=1) * gain

    q_lat = _dot(hb, wq_ref[...])
    qc = q_lat * lax.rsqrt(jnp.mean(q_lat * q_lat, axis=-1, keepdims=True) + EPS) * gq_ref[...]
    qq = _dot(qc.astype(BF16), wuq_ref[...])
    hw = N_HEADS * HEAD_BLOCK
    q_pre = qq[:, :hw] * cos8 + qq[:, hw:] * sin8
    q_ref[...] = (head_norm(q_pre, gqh_ref[...]) * (ATTN_SCALE * LOG2E)).astype(q_ref.dtype)

    kv_lat = _dot(hb, wkv_ref[...])
    ckv = kv_lat * lax.rsqrt(jnp.mean(kv_lat * kv_lat, axis=-1, keepdims=True) + EPS) * gkv_ref[...]
    ckv_ref[...] = ckv
    kr = _dot(hb, wkr_ref[...])
    kpe_wide = kr[:, :HEAD_BLOCK] * cos + kr[:, HEAD_BLOCK:] * sin
    kpe_ref[...] = kpe_wide[:, QK_NOPE:QK_NOPE + QK_ROPE]

    if not kv_refs:
        return
    k_ref, vt_ref = kv_refs
    cb = ckv.astype(BF16)
    k_pre = _dot(cb, wuk_ref[...]) + jnp.concatenate([kpe_wide] * N_HEADS, axis=1)
    k_ref[...] = head_norm(k_pre, gkh_ref[...]).astype(k_ref.dtype)
    vt = _dot_nt(wuv_ref[...], cb)
    ones_row = lax.broadcasted_iota(jnp.int32, vt.shape, 0) % VT_ROWS == V_HEAD
    vt = jnp.where(ones_row, 1.0, vt).astype(vt_ref.dtype)
    for c in range(vt_ref.shape[0]):
        vt_ref[c] = vt[:, c * KV_BLOCK:(c + 1) * KV_BLOCK]


def _in_proj(x, cos_t, sin_t, w, tm, with_kv):
    n, d = x.shape
    assert n % tm == 0
    row = lambda width: pl.BlockSpec((tm, width), lambda i: (i, 0))
    consts = [w[k] for k in ("g_attn", "w_u", "w_q", "w_kv", "w_kr", "w_ga", "w_gb", "g_q", "w_uq", "g_kv",
                             "w_uk", "w_uv_t", "g_qh", "g_kh")]
    hw = N_HEADS * HEAD_BLOCK
    out_shape = (jax.ShapeDtypeStruct((n, w["w_u"].shape[1]), F32),
                 jax.ShapeDtypeStruct((n, hw), BF16),
                 jax.ShapeDtypeStruct((n, w["w_kv"].shape[1]), F32),
                 jax.ShapeDtypeStruct((n, QK_ROPE), F32),
                 jax.ShapeDtypeStruct((n, d), BF16),
                 jax.ShapeDtypeStruct((n, d), BF16))
    if with_kv:
        assert tm % KV_BLOCK == 0
        out_shape += (jax.ShapeDtypeStruct((n, hw), BF16),
                      jax.ShapeDtypeStruct((n // KV_BLOCK, N_HEADS * VT_ROWS, KV_BLOCK), BF16))
    return pl.pallas_call(
        _in_proj_kernel,
        grid=(n // tm,),
        in_specs=[row(d), row(LANES), row(LANES)] + [_const_spec(c.shape) for c in consts],
        out_specs=tuple(pl.BlockSpec((tm // KV_BLOCK,) + s.shape[1:], lambda i: (i, 0, 0)) if len(s.shape) == 3
                        else row(s.shape[1]) for s in out_shape),
        out_shape=out_shape,
        compiler_params=_params("parallel"),
        name="in_proj",
    )(x, cos_t, sin_t, *consts)


def _ssm_kernel(u_ref, h0_ref, ar_ref, ai_ref, wbu_ref, wy_ref, d_ref, wglu_ref,
                *rest, chains, steps, emit_y):
    if emit_y:
        a_ref, ht_ref, bu_ref, hc_ref = rest
    else:
        ht_ref, bu_ref, hc_ref = rest
    i = pl.program_id(0)
    nh = ar_ref.shape[1] // SSM_SPLIT
    cw = u_ref.shape[1] // SSM_SPLIT

    @pl.when(i == 0)
    def _():
        hc_ref[...] = h0_ref[...]

    u = u_ref[...]
    ub = u.astype(BF16)
    for p in range(SSM_SPLIT):
        bu_ref[:, p * 2 * nh:(p + 1) * 2 * nh] = _dot(ub[:, p * cw:(p + 1) * cw], wbu_ref[p])

    def advance(prev_ref, prev_rows, rows):
        for p in range(SSM_SPLIT):
            re, im = slice(p * 2 * nh, p * 2 * nh + nh), slice(p * 2 * nh + nh, (p + 1) * 2 * nh)
            ar, ai = ar_ref[:, p * nh:(p + 1) * nh], ai_ref[:, p * nh:(p + 1) * nh]
            hr, hi = prev_ref[prev_rows, re], prev_ref[prev_rows, im]
            nr = ar * hr - ai * hi + bu_ref[rows, re]
            ni = ar * hi + ai * hr + bu_ref[rows, im]
            bu_ref[rows, re] = nr
            bu_ref[rows, im] = ni

    row_block = lambda j: pl.ds(pl.multiple_of(j * chains, SUBLANES), chains)
    advance(hc_ref, slice(None), row_block(0))

    def step(j, carry):
        advance(bu_ref, row_block(j - 1), row_block(j))
        return carry

    lax.fori_loop(1, steps, step, 0)
    last = bu_ref[(steps - 1) * chains:, :]
    hc_ref[...] = last
    ht_ref[...] = last
    if emit_y:
        hb = bu_ref[...].astype(BF16)
        y = jnp.concatenate([_dot(hb[:, p * 2 * nh:(p + 1) * 2 * nh], wy_ref[p]) for p in range(SSM_SPLIT)], axis=1)
        z = jax.nn.gelu(y + d_ref[...] * u)
        a_ref[...] = (z * jax.nn.sigmoid(_dot(z.astype(BF16), wglu_ref[...]))).astype(a_ref.dtype)


def _state_to_parts(h):
    c, n2 = h.shape
    return h.reshape(c, 2, SSM_SPLIT, n2 // (2 * SSM_SPLIT)).transpose(0, 2, 1, 3).reshape(c, n2)


def _state_from_parts(h):
    c, n2 = h.shape
    return h.reshape(c, SSM_SPLIT, 2, n2 // (2 * SSM_SPLIT)).transpose(0, 2, 1, 3).reshape(c, n2)


def _ssm(u_rows, h0, w, chains, steps, emit_y):
    n, width = u_rows.shape
    rows = chains * steps
    assert n % rows == 0
    ns2 = h0.shape[1]
    consts = [w[k] for k in ("a_re", "a_im", "w_bu", "w_y", "ssm_d", "w_glu")]
    ht_sds = jax.ShapeDtypeStruct((chains, ns2), F32)
    ht_spec = pl.BlockSpec((chains, ns2), lambda i: (0, 0))
    if emit_y:
        out_shape = (jax.ShapeDtypeStruct((n, width), BF16), ht_sds)
        out_specs = (pl.BlockSpec((rows, width), lambda i: (i, 0)), ht_spec)
    else:
        out_shape, out_specs = ht_sds, ht_spec
    return pl.pallas_call(
        functools.partial(_ssm_kernel, chains=chains, steps=steps, emit_y=emit_y),
        grid=(n // rows,),
        in_specs=[pl.BlockSpec((rows, width), lambda i: (i, 0)), _const_spec(h0.shape)]
        + [_const_spec(c.shape) for c in consts],
        out_specs=out_specs,
        out_shape=out_shape,
        scratch_shapes=[pltpu.VMEM((rows, ns2), F32), pltpu.VMEM((chains, ns2), F32)],
        compiler_params=_params("arbitrary"),
        name="ssm_scan" if emit_y else "ssm_end_state",
    )(u_rows, h0, *consts)


def _ssm_combine_kernel(hend_ref, ar_ref, ai_ref, h0_ref, *, seg_len, n_seg):
    ns = ar_ref.shape[1]
    pr, pi = ar_ref[...], ai_ref[...]
    qr, qi = None, None
    e = seg_len
    while e:
        if e & 1:
            qr, qi = (pr, pi) if qr is None else (qr * pr - qi * pi, qr * pi + qi * pr)
        e >>= 1
        if e:
            pr, pi = pr * pr - pi * pi, 2.0 * pr * pi
    er, ei = hend_ref[:, :ns], hend_ref[:, ns:]
    seg = lax.broadcasted_iota(jnp.int32, er.shape, 0) % n_seg
    hr, hi = jnp.zeros_like(er), jnp.zeros_like(ei)
    for s in range(1, n_seg):
        xr = qr * hr - qi * hi + er
        xi = qr * hi + qi * hr + ei
        hr = jnp.where(seg == s, pltpu.roll(xr, 1, 0), hr)
        hi = jnp.where(seg == s, pltpu.roll(xi, 1, 0), hi)
    h0_ref[:, :ns] = hr
    h0_ref[:, ns:] = hi


def _ssm_combine(hend, w, seg_len, n_seg):
    return pl.pallas_call(
        functools.partial(_ssm_combine_kernel, seg_len=seg_len, n_seg=n_seg),
        out_shape=jax.ShapeDtypeStruct(hend.shape, F32),
        name="ssm_combine",
    )(hend, w["a_re"], w["a_im"])


def _flash_kernel(q_ref, k_ref, vt_ref, o_ref, s_buf, p_buf, acc_ref, m_ref, alpha_ref):
    i = pl.program_id(2)
    tq = q_ref.shape[0]
    heads = range(FLASH_HEADS)
    q = [q_ref[:, hh * HEAD_BLOCK:(hh + 1) * HEAD_BLOCK] for hh in heads]

    def scores(t, slot):
        rows = pl.ds(pl.multiple_of(t * KV_BLOCK, KV_BLOCK), KV_BLOCK)
        for hh in heads:
            s_buf[slot, hh] = _dot_nt(k_ref[rows, hh * HEAD_BLOCK:(hh + 1) * HEAD_BLOCK], q[hh])

    def softmax(slot, hh, first_visible):
        s = s_buf[slot, hh]
        if first_visible is not None:
            ki = lax.broadcasted_iota(jnp.int32, s.shape, 0)
            qi = lax.broadcasted_iota(jnp.int32, s.shape, 1)
            s = jnp.where(ki + first_visible <= qi, s, NEG_INF)
        m = m_ref[hh]
        m_new = jnp.maximum(m, jnp.max(s, axis=0, keepdims=True))
        m_ref[hh] = m_new
        p_buf[slot, hh] = jnp.exp2(s - m_new).astype(BF16)
        alpha_ref[hh] = jnp.exp2(m - m_new)

    def values(t, slot, hh):
        acc_ref[hh] = (alpha_ref[hh] * acc_ref[hh]
                       + _dot(vt_ref[t, hh * VT_ROWS:(hh + 1) * VT_ROWS, :], p_buf[slot, hh]))

    def step(t, slot, first_visible=None, last=False):
        if not last:
            scores(t + 1, 1 - slot)
        for hh in heads:
            values(jnp.maximum(t - 1, 0), 1 - slot, hh)
        for hh in heads:
            softmax(slot, hh, first_visible)

    scores(0, 0)
    p_buf[1] = jnp.zeros(p_buf.shape[1:], BF16)
    alpha_ref[...] = jnp.ones(alpha_ref.shape, F32)
    m_ref[...] = jnp.full(m_ref.shape, NEG_INF, F32)
    acc_ref[...] = jnp.zeros(acc_ref.shape, F32)

    def two_steps(u, carry):
        step(2 * u, 0)
        step(2 * u + 1, 1)
        return carry

    lax.fori_loop(0, i, two_steps, 0)
    step(2 * i, 0, first_visible=0)
    step(2 * i + 1, 1, first_visible=KV_BLOCK, last=True)
    for hh in heads:
        values(2 * i + 1, 1, hh)
    out = [acc_ref[hh, :V_HEAD] / acc_ref[hh, V_HEAD:V_HEAD + 1] for hh in heads]
    o_ref[...] = jnp.concatenate(out, axis=0).T.astype(o_ref.dtype)


def _flash(q, k, vt, batch, seq):
    tq = 2 * KV_BLOCK
    nq = seq // tq
    hs = FLASH_HEADS
    return pl.pallas_call(
        _flash_kernel,
        grid=(batch, N_HEADS // hs, nq),
        in_specs=[pl.BlockSpec((tq, hs * HEAD_BLOCK), lambda b, p, i: (b * nq + i, p)),
                  pl.BlockSpec((seq, hs * HEAD_BLOCK), lambda b, p, i: (b, p)),
                  pl.BlockSpec((seq // KV_BLOCK, hs * VT_ROWS, KV_BLOCK), lambda b, p, i: (b, p, 0))],
        out_specs=pl.BlockSpec((tq, hs * V_HEAD), lambda b, p, i: (b * nq + i, p)),
        out_shape=jax.ShapeDtypeStruct((batch * seq, N_HEADS * V_HEAD), BF16),
        scratch_shapes=[pltpu.VMEM((2, hs, KV_BLOCK, tq), F32), pltpu.VMEM((2, hs, KV_BLOCK, tq), BF16),
                        pltpu.VMEM((hs, VT_ROWS, tq), F32), pltpu.VMEM((hs, 1, tq), F32),
                        pltpu.VMEM((hs, 1, tq), F32)],
        compiler_params=_params("parallel", "parallel", "arbitrary"),
        name="flash",
    )(q, k, vt)


def _absorb_kernel(q_ref, gk_ref, wabs_ref, sel_ref, qabs_ref, qr_ref):
    qg = (q_ref[...].astype(F32) * gk_ref[...]).astype(BF16)
    qabs_ref[...] = _dot(qg, wabs_ref[...]).astype(qabs_ref.dtype)
    qr_ref[...] = _dot(qg, sel_ref[...]).astype(qr_ref.dtype)


def _absorb(q, w):
    n = q.shape[0]
    return pl.pallas_call(
        _absorb_kernel,
        out_shape=(jax.ShapeDtypeStruct((n, w["w_abs"].shape[1]), BF16),
                   jax.ShapeDtypeStruct((n, w["sel_rope"].shape[1]), BF16)),
        compiler_params=_params(),
        name="absorb",
    )(q, w["g_kh"], w["w_abs"], w["sel_rope"])


def _paged_attn_kernel(pt_ref, qabs_ref, qr_ref, qabs_nx_ref, qr_nx_ref, ckvn_ref, kpetn_ref, wukt_ref,
                       ckv_hbm, kpet_hbm, o_ref, ckv_buf, kpe_buf, cb_buf, sc_buf, lhs_ref, m_ref, l_ref, acc_ref,
                       sem_c, sem_k, *, layer, n_chunks, chunk_pages, n_new):
    b = pl.program_id(0)
    total = pl.num_programs(0) * n_chunks
    n_k = wukt_ref.shape[0]
    rows_q = qabs_ref.shape[1]
    tk = chunk_pages * PAGE_SIZE
    parts = chunk_pages // PART_PAGES
    qr = (qr_ref[0][:, :QK_ROPE], qr_nx_ref[0][:, :QK_ROPE])

    def start_chunk(g, slt):
        g = jnp.minimum(g, total - 1)
        for p in range(chunk_pages):
            page = pt_ref[g * chunk_pages + p]
            dst = pl.ds(p * PAGE_SIZE, PAGE_SIZE)
            pltpu.make_async_copy(ckv_hbm.at[layer, page], ckv_buf.at[slt, dst, :], sem_c.at[slt]).start()
            pltpu.make_async_copy(kpet_hbm.at[layer, page], kpe_buf.at[slt, p], sem_k.at[slt]).start()

    def wait_chunk(slt):
        pltpu.make_async_copy(ckv_buf.at[slt], ckv_buf.at[slt], sem_c.at[slt]).wait()
        pltpu.make_async_copy(kpe_buf.at[slt], kpe_buf.at[slt], sem_k.at[slt]).wait()

    def scaled_scores(which, cb, kpet):
        keys = cb.shape[0]
        both = _dot_nt(lhs_ref[which], cb)
        kt = both[:n_k]
        ssq = jnp.sum((kt * kt).reshape(QK_NOPE, N_HEADS, keys), axis=0)
        ssq = ssq + jnp.sum(kpet * kpet, axis=0, keepdims=True)
        r = lax.rsqrt(ssq * (1.0 / QK_HEAD) + EPS)
        sc = both[n_k:] + _dot(qr[which], kpet.astype(BF16))
        return (sc.reshape(n_new, N_HEADS, keys) * r[None]).reshape(rows_q, keys)

    def stage_a(slt, which, part):
        pages = range(part * PART_PAGES, (part + 1) * PART_PAGES)
        rows = slice(pages[0] * PAGE_SIZE, (pages[-1] + 1) * PAGE_SIZE)
        cb = ckv_buf[slt, rows, :].astype(BF16)
        cb_buf[slt, rows, :] = cb
        kpet = jnp.concatenate([kpe_buf[slt, p] for p in pages], axis=1)
        sc_buf[slt, :, rows] = scaled_scores(which, cb, kpet)

    def stage_b(sc, cb):
        m_old = m_ref[...]
        m_new = jnp.maximum(m_old, jnp.max(sc, axis=1, keepdims=True))
        p = jnp.exp2(sc - m_new)
        alpha = jnp.exp2(m_old - m_new)
        l_ref[...] = alpha * l_ref[...] + jnp.sum(p, axis=1, keepdims=True)
        acc_ref[...] = alpha * acc_ref[...] + _dot(p.astype(BF16), cb)
        m_ref[...] = m_new

    @pl.when(b == 0)
    def _():
        for which in range(2):
            lhs_ref[which, :n_k, :] = wukt_ref[...]

    lhs_ref[0, n_k:, :] = qabs_ref[0]
    lhs_ref[1, n_k:, :] = qabs_nx_ref[0]

    @pl.when(b == 0)
    def _():
        start_chunk(0, 0)
        start_chunk(1, 1)
        wait_chunk(0)
        for part in range(parts):
            stage_a(0, 0, part)

    m_ref[...] = jnp.full(m_ref.shape, NEG_INF, F32)
    l_ref[...] = jnp.zeros(l_ref.shape, F32)
    acc_ref[...] = jnp.zeros(acc_ref.shape, F32)

    for c in range(n_chunks):
        g = b * n_chunks + c
        slot, nxt = c % 2, (c + 1) % 2
        which = 0 if c + 1 < n_chunks else 1

        wait_chunk(nxt)
        start_chunk(g + 2, slot)
        stage_a(nxt, which, 0)
        stage_b(sc_buf[slot], cb_buf[slot])
        for part in range(1, parts):
            stage_a(nxt, which, part)

    @pl.when(b == pl.num_programs(0) - 1)
    def _():
        wait_chunk((n_chunks - 1) % 2)

    tkn = kpetn_ref.shape[2]
    cbn = ckvn_ref[0].astype(BF16)
    cbn = jnp.concatenate([cbn, jnp.zeros((tkn - cbn.shape[0], cbn.shape[1]), BF16)], axis=0)
    sc = scaled_scores(0, cbn, kpetn_ref[0])
    key = lax.broadcasted_iota(jnp.int32, (rows_q, tkn), 1)
    tok = lax.broadcasted_iota(jnp.int32, (rows_q, tkn), 0) // N_HEADS
    stage_b(jnp.where(key <= tok, sc, NEG_INF), cbn)
    o_ref[0] = (acc_ref[...] / l_ref[...]).astype(o_ref.dtype)


def _paged_attn(page_table, qabs, qr, ckv_new, kpet_new, w_ukt, cache_ckv, cache_kpet, layer, chunk_pages):
    n_seq, pages_per_seq = page_table.shape
    assert pages_per_seq % chunk_pages == 0 and chunk_pages % PART_PAGES == 0
    n_chunks = pages_per_seq // chunk_pages
    assert n_chunks % 2 == 0 and n_seq * n_chunks >= 2
    rows_q, kv_lora = qabs.shape[1:]
    n_new = rows_q // N_HEADS
    tk = chunk_pages * PAGE_SIZE
    seq_block = lambda shape: pl.BlockSpec((1,) + shape, lambda b, pt: (b, 0, 0))
    nxt_block = lambda shape: pl.BlockSpec((1,) + shape, lambda b, pt: (jnp.minimum(b + 1, n_seq - 1), 0, 0))
    grid_spec = pltpu.PrefetchScalarGridSpec(
        num_scalar_prefetch=1,
        grid=(n_seq,),
        in_specs=[seq_block(qabs.shape[1:]), seq_block(qr.shape[1:]),
                  nxt_block(qabs.shape[1:]), nxt_block(qr.shape[1:]),
                  seq_block(ckv_new.shape[1:]), seq_block(kpet_new.shape[1:]),
                  pl.BlockSpec(w_ukt.shape, lambda b, pt: (0, 0)),
                  pl.BlockSpec(memory_space=pl.ANY), pl.BlockSpec(memory_space=pl.ANY)],
        out_specs=seq_block((rows_q, kv_lora)),
        scratch_shapes=[pltpu.VMEM((2, tk, kv_lora), F32), pltpu.VMEM((2, chunk_pages, QK_ROPE, PAGE_SIZE), F32),
                        pltpu.VMEM((2, tk, kv_lora), BF16), pltpu.VMEM((2, rows_q, tk), F32),
                        pltpu.VMEM((2, w_ukt.shape[0] + rows_q, kv_lora), BF16),
                        pltpu.VMEM((rows_q, 1), F32), pltpu.VMEM((rows_q, 1), F32),
                        pltpu.VMEM((rows_q, kv_lora), F32),
                        pltpu.SemaphoreType.DMA((2,)), pltpu.SemaphoreType.DMA((2,))])
    return pl.pallas_call(
        functools.partial(_paged_attn_kernel, layer=layer, n_chunks=n_chunks, chunk_pages=chunk_pages, n_new=n_new),
        grid_spec=grid_spec,
        out_shape=jax.ShapeDtypeStruct((n_seq, rows_q, kv_lora), BF16),
        compiler_params=_params("arbitrary"),
        name="paged_attn",
    )(page_table.reshape(-1), qabs, qr, qabs, qr, ckv_new, kpet_new, w_ukt, cache_ckv, cache_kpet)


def _out_proj_kernel(x_ref, a_ref, b_ref, sga_ref, sgb_ref, p_ref, wuvb_ref, wa_ref, wb_ref, wo_ref, gffn_ref,
                     wg_ref, wup_ref, wdn_ref, gple_ref, wpg_ref, wple_ref, y_ref, *, latent_b):
    b = b_ref[...]
    if latent_b:
        b = _dot(b, wuvb_ref[...]).astype(BF16)
    merged = (sga_ref[...].astype(F32) * _dot(a_ref[...], wa_ref[...])
              + sgb_ref[...].astype(F32) * _dot(b, wb_ref[...]))
    x = x_ref[...] + _dot(merged.astype(BF16), wo_ref[...])

    def norm(t, g_ref):
        return (t * lax.rsqrt(jnp.mean(t * t, axis=-1, keepdims=True) + EPS) * g_ref[...]).astype(BF16)

    hf = norm(x, gffn_ref)
    act = jax.nn.silu(_dot(hf, wg_ref[...])) * _dot(hf, wup_ref[...])
    x = x + _dot(act.astype(BF16), wdn_ref[...])
    gate = jax.nn.sigmoid(_dot(norm(x, gple_ref), wpg_ref[...]))
    y_ref[...] = x + gate * _dot(p_ref[...].astype(BF16), wple_ref[...])


def _out_proj(x, a, b, sga, sgb, p, w, tm, latent_b):
    n, d = x.shape
    assert n % tm == 0
    row = lambda width: pl.BlockSpec((tm, width), lambda i: (i, 0))
    consts = [w[k] for k in ("w_uv_bd", "w_a", "w_b", "w_out", "g_ffn", "w_gate", "w_up", "w_down", "g_ple",
                             "w_ple_gate", "w_ple")]
    return pl.pallas_call(
        functools.partial(_out_proj_kernel, latent_b=latent_b),
        grid=(n // tm,),
        in_specs=[row(d), row(a.shape[1]), row(b.shape[1]), row(d), row(d), row(p.shape[1])]
        + [_const_spec(c.shape) for c in consts],
        out_specs=row(d),
        out_shape=jax.ShapeDtypeStruct((n, d), F32),
        compiler_params=_params("parallel"),
        name="out_proj_latent" if latent_b else "out_proj",
    )(x, a, b, sga, sgb, p, *consts)


def _head_blocked(w_nope, w_rope):
    k = w_nope.shape[0] if w_nope is not None else w_rope.shape[0]
    nope = w_nope if w_nope is not None else jnp.zeros((k, N_HEADS, QK_NOPE), F32)
    rope = w_rope if w_rope is not None else jnp.zeros((k, N_HEADS, QK_ROPE), F32)
    pad = jnp.zeros((k, N_HEADS, HEAD_BLOCK - QK_HEAD), F32)
    return jnp.concatenate([nope, rope, pad], axis=-1).reshape(k, N_HEADS * HEAD_BLOCK)


def _rot_cols(w):
    half = QK_ROPE // 2
    return jnp.concatenate([-w[..., half:], w[..., :half]], axis=-1)


def _head_gain(g_nope, g_rope_half):
    g = jnp.concatenate([g_nope, g_rope_half, g_rope_half, jnp.zeros((HEAD_BLOCK - QK_HEAD,), F32)])
    return jnp.tile(g, N_HEADS)[None, :]


def _block_diag(blocks):
    g, r, c = blocks.shape
    eye = jnp.eye(g, dtype=blocks.dtype)
    return jnp.einsum("grc,gh->grhc", blocks, eye).reshape(g * r, g * c)


def _prepare(norm_attn_g, w_in, q_norm_g, w_uq, kv_norm_g, w_uk, w_uv, q_norm_nope_g, q_norm_rope_g,
             k_norm_nope_g, k_norm_rope_g, disc, ssm_c_re, ssm_c_im, ssm_d, w_glu, w_branch_a, w_branch_b,
             w_out, norm_ffn_g, w_gate, w_up, w_down, norm_ple_g, w_ple_gate, w_ple):
    d_model = w_in.shape[0]
    ssm_w = ssm_d.shape[0]
    q_lora, kv_lora = w_uq.shape[0], w_uk.shape[0]
    o = np.cumsum([0, ssm_w, q_lora, kv_lora, QK_ROPE, d_model, d_model])
    bf = lambda a: a.astype(BF16)
    w = {}
    w["g_attn"] = norm_attn_g[None, :]
    w["w_u"] = bf(w_in[:, o[0]:o[1]])
    w["w_q"] = bf(w_in[:, o[1]:o[2]])
    w["w_kv"] = bf(w_in[:, o[2]:o[3]])
    w_kr = w_in[:, o[3]:o[4]]
    place = lambda r: jnp.pad(r, ((0, 0), (QK_NOPE, HEAD_BLOCK - QK_HEAD)))
    w["w_kr"] = bf(jnp.concatenate([place(w_kr), place(_rot_cols(w_kr))], axis=1))
    w["w_ga"] = bf(w_in[:, o[4]:o[5]])
    w["w_gb"] = bf(w_in[:, o[5]:o[6]])
    w["g_q"] = q_norm_g[None, :]
    uq_nope, uq_rope = w_uq[..., :QK_NOPE], w_uq[..., QK_NOPE:]
    w["w_uq"] = bf(jnp.concatenate([_head_blocked(uq_nope, uq_rope), _head_blocked(None, _rot_cols(uq_rope))], axis=1))
    w["g_kv"] = kv_norm_g[None, :]
    w["w_uk"] = bf(_head_blocked(w_uk, None))
    uv_t = jnp.pad(jnp.transpose(w_uv, (1, 2, 0)), ((0, 0), (0, VT_ROWS - V_HEAD), (0, 0)))
    w["w_uv_t"] = bf(uv_t.reshape(N_HEADS * VT_ROWS, kv_lora))
    w["g_qh"] = _head_gain(q_norm_nope_g, q_norm_rope_g)
    w["g_kh"] = _head_gain(k_norm_nope_g, k_norm_rope_g)
    ukt = jnp.transpose(w_uk, (1, 2, 0))
    ukt = jnp.pad(ukt, ((0, 0), (0, HEAD_BLOCK - QK_NOPE), (0, 0)))
    w["w_abs"] = bf(_block_diag(ukt))
    sel = np.zeros((N_HEADS * HEAD_BLOCK, N_HEADS * HEAD_BLOCK), np.float32)
    for h in range(N_HEADS):
        for dd in range(QK_ROPE):
            sel[h * HEAD_BLOCK + QK_NOPE + dd, h * HEAD_BLOCK + dd] = 1.0
    w["sel_rope"] = jnp.asarray(sel, BF16)
    w["w_ukt"] = bf(jnp.transpose(w_uk, (2, 1, 0)).reshape(QK_NOPE * N_HEADS, kv_lora))
    w["w_uv_bd"] = bf(_block_diag(jnp.transpose(w_uv, (1, 0, 2))))
    lb_re, lb_im, bb_re, bb_im = disc
    w["a_re"] = lb_re.reshape(1, -1)
    w["a_im"] = lb_im.reshape(1, -1)
    tr = lambda a: jnp.transpose(a, (0, 2, 1))
    gp = bb_re.shape[0] // SSM_SPLIT
    part = lambda a, p: _block_diag(tr(a[p * gp:(p + 1) * gp]))
    w["w_bu"] = bf(jnp.stack([jnp.concatenate([part(bb_re, p), part(bb_im, p)], axis=1) for p in range(SSM_SPLIT)]))
    w["w_y"] = bf(jnp.stack([jnp.concatenate([part(ssm_c_re, p), -part(ssm_c_im, p)], axis=0)
                             for p in range(SSM_SPLIT)]))
    w["ssm_d"] = ssm_d[None, :]
    w["w_glu"] = bf(w_glu)
    w["w_a"], w["w_b"], w["w_out"] = bf(w_branch_a), bf(w_branch_b), bf(w_out)
    w["g_ffn"], w["g_ple"] = norm_ffn_g[None, :], norm_ple_g[None, :]
    w["w_gate"], w["w_up"], w["w_down"] = bf(w_gate), bf(w_up), bf(w_down)
    w["w_ple_gate"], w["w_ple"] = bf(w_ple_gate), bf(w_ple)
    return w


def _rope_tables(pos):
    half = QK_ROPE // 2
    inv = ROPE_THETA ** (-jnp.arange(half, dtype=F32) / half)
    ang = pos.astype(F32)[:, None] * inv[None, :]
    cos, sin = jnp.cos(ang), jnp.sin(ang)
    n = pos.shape[0]
    ones, zeros = jnp.ones((n, QK_NOPE), F32), jnp.zeros((n, QK_NOPE), F32)
    pad = jnp.zeros((n, HEAD_BLOCK - QK_HEAD), F32)
    return (jnp.concatenate([ones, cos, cos, pad], axis=1), jnp.concatenate([zeros, sin, sin, pad], axis=1))


def _pick(n, prefs):
    for t in prefs:
        if n % t == 0:
            return t
    return n


def kernel(x_prompt, x_sample, p_prompt, p_sample, cache_ckv, cache_kpe, state_ssm_re, state_ssm_im, page_table, norm_attn_g, w_in, q_norm_g, w_uq, kv_norm_g, w_uk, w_uv, q_norm_nope_g, q_norm_rope_g, k_norm_nope_g, k_norm_rope_g, ssm_lam_re, ssm_lam_im, ssm_log_dt, ssm_b_re, ssm_b_im, ssm_c_re, ssm_c_im, ssm_d, w_glu, w_branch_a, w_branch_b, w_out, norm_ffn_g, w_gate, w_up, w_down, norm_ple_g, w_ple_gate, w_ple):
    depth = w_in.shape[0]
    batch, seq, d_model = x_prompt.shape
    n_seq, n_new, _ = x_sample.shape
    past = page_table.shape[1] * PAGE_SIZE
    groups, n_state = ssm_lam_re.shape[1:]
    ns = groups * n_state

    cos_p, sin_p = _rope_tables(jnp.arange(seq))
    cos_p, sin_p = jnp.tile(cos_p, (batch, 1)), jnp.tile(sin_p, (batch, 1))
    cos_s, sin_s = _rope_tables(past + jnp.arange(n_new))
    cos_s, sin_s = jnp.tile(cos_s, (n_seq, 1)), jnp.tile(sin_s, (n_seq, 1))

    n_seg = SUBLANES
    assert seq % n_seg == 0
    seg_len = seq // n_seg
    chains_p = batch * n_seg
    steps_p = _pick(seg_len, (16, 8, 4, 2))
    steps_s = _pick(n_new, (4, 2))
    tm_p = _pick(batch * seq, (512, 256, 128))
    tm_in = _pick(batch * seq, (1024, 512, 256))
    tm_s = _pick(n_seq * n_new, (512, 256, 128))
    assert seq % (2 * KV_BLOCK) == 0
    chunk_pages = _pick(page_table.shape[1] // 2, (32, 16, 8))

    xp = x_prompt.reshape(batch * seq, d_model)
    xs = x_sample.reshape(n_seq * n_new, d_model)
    outs = {k: [] for k in ("ckv_p", "kpe_p", "hre_p", "him_p", "ckv_s", "kpe_s", "hre_s", "him_s")}
    for i in range(depth):
        disc = _discretize(ssm_lam_re[i], ssm_lam_im[i], ssm_log_dt[i], ssm_b_re[i], ssm_b_im[i])
        w = _prepare(norm_attn_g[i], w_in[i], q_norm_g[i], w_uq[i], kv_norm_g[i], w_uk[i], w_uv[i],
                     q_norm_nope_g[i], q_norm_rope_g[i], k_norm_nope_g[i], k_norm_rope_g[i], disc,
                     ssm_c_re[i], ssm_c_im[i], ssm_d[i], w_glu[i], w_branch_a[i], w_branch_b[i], w_out[i],
                     norm_ffn_g[i], w_gate[i], w_up[i], w_down[i], norm_ple_g[i], w_ple_gate[i], w_ple[i])
        ssm_w = w["w_u"].shape[1]

        u, q, ckv, kpe, sga, sgb, k, vt = _in_proj(xp, cos_p, sin_p, w, tm_in, with_kv=True)
        u_rows = u.reshape(batch, n_seg, seg_len, ssm_w).transpose(2, 0, 1, 3).reshape(batch * seq, ssm_w)
        zero_state = jnp.zeros((chains_p, 2 * ns), F32)
        h_end = _ssm(u_rows, zero_state, w, chains_p, steps_p, emit_y=False)
        h_start = _ssm_combine(_state_from_parts(h_end), w, seg_len, n_seg)
        a_rows, h_fin = _ssm(u_rows, _state_to_parts(h_start), w, chains_p, steps_p, emit_y=True)
        h_fin = _state_from_parts(h_fin)
        a_out = a_rows.reshape(seg_len, batch, n_seg, ssm_w).transpose(1, 2, 0, 3).reshape(batch * seq, ssm_w)
        b_out = _flash(q, k, vt, batch, seq)
        xp = _out_proj(xp, a_out, b_out, sga, sgb, p_prompt[i].reshape(batch * seq, -1), w, tm_p, latent_b=False)
        h_fin = h_fin.reshape(batch, n_seg, 2 * ns)[:, -1]
        outs["ckv_p"].append(ckv.reshape(batch, seq, -1))
        outs["kpe_p"].append(kpe.reshape(batch, seq, -1))
        outs["hre_p"].append(h_fin[:, :ns].reshape(batch, groups, n_state))
        outs["him_p"].append(h_fin[:, ns:].reshape(batch, groups, n_state))

        u, q, ckv, kpe, sga, sgb = _in_proj(xs, cos_s, sin_s, w, tm_s, with_kv=False)
        u_rows = u.reshape(n_seq, n_new, ssm_w).transpose(1, 0, 2).reshape(n_seq * n_new, ssm_w)
        h0 = jnp.concatenate([state_ssm_re[i].reshape(n_seq, ns), state_ssm_im[i].reshape(n_seq, ns)], axis=1)
        a_rows, h_fin = _ssm(u_rows, _state_to_parts(h0), w, n_seq, steps_s, emit_y=True)
        h_fin = _state_from_parts(h_fin)
        a_out = a_rows.reshape(n_new, n_seq, ssm_w).transpose(1, 0, 2).reshape(n_seq * n_new, ssm_w)
        qabs, qr = _absorb(q, w)
        rows_q = n_new * N_HEADS
        kv_lora = ckv.shape[1]
        pad_new = PAGE_SIZE - n_new
        ckv_new = ckv.reshape(n_seq, n_new, kv_lora)
        kpet_new = jnp.pad(kpe.reshape(n_seq, n_new, QK_ROPE).transpose(0, 2, 1), ((0, 0), (0, 0), (0, pad_new)))
        o_lat = _paged_attn(page_table, qabs.reshape(n_seq, rows_q, kv_lora), qr.reshape(n_seq, rows_q, HEAD_BLOCK),
                            ckv_new, kpet_new, w["w_ukt"], cache_ckv, jnp.swapaxes(cache_kpe, 2, 3), i, chunk_pages)
        b_lat = o_lat.reshape(n_seq * n_new, N_HEADS * kv_lora)
        xs = _out_proj(xs, a_out, b_lat, sga, sgb, p_sample[i].reshape(n_seq * n_new, -1), w, tm_s, latent_b=True)
        outs["ckv_s"].append(ckv.reshape(n_seq, n_new, -1))
        outs["kpe_s"].append(kpe.reshape(n_seq, n_new, -1))
        outs["hre_s"].append(h_fin[:, :ns].reshape(n_seq, groups, n_state))
        outs["him_s"].append(h_fin[:, ns:].reshape(n_seq, groups, n_state))

    st = lambda key: jnp.stack(outs[key])
    return (xp.reshape(batch, seq, d_model), xs.reshape(n_seq, n_new, d_model),
            st("ckv_p"), st("kpe_p"), st("hre_p"), st("him_p"),
            st("ckv_s"), st("kpe_s"), st("hre_s"), st("him_s"))
```

```python
import functools
import math

import numpy as np
import jax
import jax.numpy as jnp
from jax import lax
from jax.experimental import pallas as pl
from jax.experimental.pallas import tpu as pltpu

N_HEADS = 8
QK_NOPE = 64
QK_ROPE = 32
QK_HEAD = QK_NOPE + QK_ROPE
V_HEAD = 64
VT_ROWS = 80
HEAD_BLOCK = 128
ROPE_THETA = 10000.0
ATTN_SCALE = QK_HEAD ** -0.5
LOG2E = math.log2(math.e)
SSM_GROUP = 16
SSM_STATE = 64
PAGE_SIZE = 128
KV_BLOCK = 256
FLASH_HEADS = 2
PART_PAGES = 16
SSM_SPLIT = 2
EPS = 1e-6

LANES = 128
SUBLANES = 8
VMEM_LIMIT_BYTES = 56 * 1024 * 1024

BF16 = jnp.bfloat16
F32 = jnp.float32
NEG_INF = float("-inf")


def _dot(a, b):
    return jnp.dot(a, b, preferred_element_type=F32)


def _dot_nt(a, b):
    return lax.dot_general(a, b, (((1,), (1,)), ((), ())), preferred_element_type=F32)


def _const_spec(shape):
    nd = len(shape)
    return pl.BlockSpec(shape, lambda *_: (0,) * nd, pipeline_mode=pl.Buffered(1))


def _params(*sem):
    return pltpu.CompilerParams(dimension_semantics=sem, vmem_limit_bytes=VMEM_LIMIT_BYTES)


def _discretize_kernel(lr_ref, li_ref, ldt_ref, br_ref, bi_ref, lbr_ref, lbi_ref, bbr_ref, bbi_ref):
    lr, li = lr_ref[...], li_ref[...]
    dt = jnp.exp(ldt_ref[...])
    mag = jnp.exp(lr * dt)
    lb_re, lb_im = mag * jnp.cos(li * dt), mag * jnp.sin(li * dt)
    num_re, num_im = lb_re - 1.0, lb_im
    den = lr * lr + li * li
    f_re = (num_re * lr + num_im * li) / den
    f_im = (num_im * lr - num_re * li) / den
    br, bi = br_ref[...], bi_ref[...]
    lbr_ref[...] = lb_re
    lbi_ref[...] = lb_im
    bbr_ref[...] = f_re * br - f_im * bi
    bbi_ref[...] = f_re * bi + f_im * br


def _discretize(lam_re, lam_im, log_dt, b_re, b_im):
    g, n, c = b_re.shape
    rep = lambda a: jnp.repeat(a, c, axis=-1)
    args = (rep(lam_re), rep(lam_im), jnp.broadcast_to(log_dt[:, None], (g, n * c)),
            b_re.reshape(g, n * c), b_im.reshape(g, n * c))
    sds = jax.ShapeDtypeStruct((g, n * c), F32)
    lbr, lbi, bbr, bbi = pl.pallas_call(_discretize_kernel, out_shape=(sds,) * 4, name="discretize")(*args)
    lb_re = lbr.reshape(g, n, c)[..., 0]
    lb_im = lbi.reshape(g, n, c)[..., 0]
    return lb_re, lb_im, bbr.reshape(g, n, c), bbi.reshape(g, n, c)


def _in_proj_kernel(x_ref, cos_ref, sin_ref, gattn_ref, wu_ref, wq_ref, wkv_ref, wkr_ref, wga_ref, wgb_ref,
                    gq_ref, wuq_ref, gkv_ref, wuk_ref, wuv_ref, gqh_ref, gkh_ref,
                    u_ref, q_ref, ckv_ref, kpe_ref, sga_ref, sgb_ref, *kv_refs):
    x = x_ref[...]
    h = x * lax.rsqrt(jnp.mean(x * x, axis=-1, keepdims=True) + EPS) * gattn_ref[...]
    hb = h.astype(BF16)
    cos, sin = cos_ref[...], sin_ref[...]
    cos8 = jnp.concatenate([cos] * N_HEADS, axis=1)
    sin8 = jnp.concatenate([sin] * N_HEADS, axis=1)

    u_ref[...] = _dot(hb, wu_ref[...])
    sga_ref[...] = jax.nn.sigmoid(_dot(hb, wga_ref[...])).astype(sga_ref.dtype)
    sgb_ref[...] = jax.nn.sigmoid(_dot(hb, wgb_ref[...])).astype(sgb_ref.dtype)

    def head_norm(t, gain):
        blocks = []
        for hd in range(N_HEADS):
            blk = t[:, hd * HEAD_BLOCK:(hd + 1) * HEAD_BLOCK]
            ssq = jnp.sum(blk * blk, axis=-1, keepdims=True)
            blocks.append(blk * lax.rsqrt(ssq * (1.0 / QK_HEAD) + EPS))
        return jnp.concatenate(blocks, axis=1) * gain

    q_lat = _dot(hb, wq_ref[...])
    qc = q_lat * lax.rsqrt(jnp.mean(q_lat * q_lat, axis=-1, keepdims=True) + EPS) * gq_ref[...]
    qq = _dot(qc.astype(BF16), wuq_ref[...])
    hw = N_HEADS * HEAD_BLOCK
    q_pre = qq[:, :hw] * cos8 + qq[:, hw:] * sin8
    q_ref[...] = (head_norm(q_pre, gqh_ref[...]) * (ATTN_SCALE * LOG2E)).astype(q_ref.dtype)

    kv_lat = _dot(hb, wkv_ref[...])
    ckv = kv_lat * lax.rsqrt(jnp.mean(kv_lat * kv_lat, axis=-1, keepdims=True) + EPS) * gkv_ref[...]
    ckv_ref[...] = ckv
    kr = _dot(hb, wkr_ref[...])
    kpe_wide = kr[:, :HEAD_BLOCK] * cos + kr[:, HEAD_BLOCK:] * sin
    kpe_ref[...] = kpe_wide[:, QK_NOPE:QK_NOPE + QK_ROPE]

    if not kv_refs:
        return
    k_ref, vt_ref = kv_refs
    cb = ckv.astype(BF16)
    k_pre = _dot(cb, wuk_ref[...]) + jnp.concatenate([kpe_wide] * N_HEADS, axis=1)
    k_ref[...] = head_norm(k_pre, gkh_ref[...]).astype(k_ref.dtype)
    vt = _dot_nt(wuv_ref[...], cb)
    ones_row = lax.broadcasted_iota(jnp.int32, vt.shape, 0) % VT_ROWS == V_HEAD
    vt = jnp.where(ones_row, 1.0, vt).astype(vt_ref.dtype)
    for c in range(vt_ref.shape[0]):
        vt_ref[c] = vt[:, c * KV_BLOCK:(c + 1) * KV_BLOCK]


def _in_proj(x, cos_t, sin_t, w, tm, with_kv):
    n, d = x.shape
    assert n % tm == 0
    row = lambda width: pl.BlockSpec((tm, width), lambda i: (i, 0))
    consts = [w[k] for k in ("g_attn", "w_u", "w_q", "w_kv", "w_kr", "w_ga", "w_gb", "g_q", "w_uq", "g_kv",
                             "w_uk", "w_uv_t", "g_qh", "g_kh")]
    hw = N_HEADS * HEAD_BLOCK
    out_shape = (jax.ShapeDtypeStruct((n, w["w_u"].shape[1]), F32),
                 jax.ShapeDtypeStruct((n, hw), BF16),
                 jax.ShapeDtypeStruct((n, w["w_kv"].shape[1]), F32),
                 jax.ShapeDtypeStruct((n, QK_ROPE), F32),
                 jax.ShapeDtypeStruct((n, d), BF16),
                 jax.ShapeDtypeStruct((n, d), BF16))
    if with_kv:
        assert tm % KV_BLOCK == 0
        out_shape += (jax.ShapeDtypeStruct((n, hw), BF16),
                      jax.ShapeDtypeStruct((n // KV_BLOCK, N_HEADS * VT_ROWS, KV_BLOCK), BF16))
    return pl.pallas_call(
        _in_proj_kernel,
        grid=(n // tm,),
        in_specs=[row(d), row(LANES), row(LANES)] + [_const_spec(c.shape) for c in consts],
        out_specs=tuple(pl.BlockSpec((tm // KV_BLOCK,) + s.shape[1:], lambda i: (i, 0, 0)) if len(s.shape) == 3
                        else row(s.shape[1]) for s in out_shape),
        out_shape=out_shape,
        compiler_params=_params("parallel"),
        name="in_proj",
    )(x, cos_t, sin_t, *consts)


def _ssm_kernel(u_ref, h0_ref, ar_ref, ai_ref, wbu_ref, wy_ref, d_ref, wglu_ref,
                *rest, chains, steps, emit_y):
    if emit_y:
        a_ref, ht_ref, bu_ref, hc_ref = rest
    else:
        ht_ref, bu_ref, hc_ref = rest
    i = pl.program_id(0)
    nh = ar_ref.shape[1] // SSM_SPLIT
    cw = u_ref.shape[1] // SSM_SPLIT

    @pl.when(i == 0)
    def _():
        hc_ref[...] = h0_ref[...]

    u = u_ref[...]
    ub = u.astype(BF16)
    for p in range(SSM_SPLIT):
        bu_ref[:, p * 2 * nh:(p + 1) * 2 * nh] = _dot(ub[:, p * cw:(p + 1) * cw], wbu_ref[p])

    def advance(prev_ref, prev_rows, rows):
        for p in range(SSM_SPLIT):
            re, im = slice(p * 2 * nh, p * 2 * nh + nh), slice(p * 2 * nh + nh, (p + 1) * 2 * nh)
            ar, ai = ar_ref[:, p * nh:(p + 1) * nh], ai_ref[:, p * nh:(p + 1) * nh]
            hr, hi = prev_ref[prev_rows, re], prev_ref[prev_rows, im]
            nr = ar * hr - ai * hi + bu_ref[rows, re]
            ni = ar * hi + ai * hr + bu_ref[rows, im]
            bu_ref[rows, re] = nr
            bu_ref[rows, im] = ni

    row_block = lambda j: pl.ds(pl.multiple_of(j * chains, SUBLANES), chains)
    advance(hc_ref, slice(None), row_block(0))

    def step(j, carry):
        advance(bu_ref, row_block(j - 1), row_block(j))
        return carry

    lax.fori_loop(1, steps, step, 0)
    last = bu_ref[(steps - 1) * chains:, :]
    hc_ref[...] = last
    ht_ref[...] = last
    if emit_y:
        hb = bu_ref[...].astype(BF16)
        y = jnp.concatenate([_dot(hb[:, p * 2 * nh:(p + 1) * 2 * nh], wy_ref[p]) for p in range(SSM_SPLIT)], axis=1)
        z = jax.nn.gelu(y + d_ref[...] * u)
        a_ref[...] = (z * jax.nn.sigmoid(_dot(z.astype(BF16), wglu_ref[...]))).astype(a_ref.dtype)


def _state_to_parts(h):
    c, n2 = h.shape
    return h.reshape(c, 2, SSM_SPLIT, n2 // (2 * SSM_SPLIT)).transpose(0, 2, 1, 3).reshape(c, n2)


def _state_from_parts(h):
    c, n2 = h.shape
    return h.reshape(c, SSM_SPLIT, 2, n2 // (2 * SSM_SPLIT)).transpose(0, 2, 1, 3).reshape(c, n2)


def _ssm(u_rows, h0, w, chains, steps, emit_y):
    n, width = u_rows.shape
    rows = chains * steps
    assert n % rows == 0
    ns2 = h0.shape[1]
    consts = [w[k] for k in ("a_re", "a_im", "w_bu", "w_y", "ssm_d", "w_glu")]
    ht_sds = jax.ShapeDtypeStruct((chains, ns2), F32)
    ht_spec = pl.BlockSpec((chains, ns2), lambda i: (0, 0))
    if emit_y:
        out_shape = (jax.ShapeDtypeStruct((n, width), BF16), ht_sds)
        out_specs = (pl.BlockSpec((rows, width), lambda i: (i, 0)), ht_spec)
    else:
        out_shape, out_specs = ht_sds, ht_spec
    return pl.pallas_call(
        functools.partial(_ssm_kernel, chains=chains, steps=steps, emit_y=emit_y),
        grid=(n // rows,),
        in_specs=[pl.BlockSpec((rows, width), lambda i: (i, 0)), _const_spec(h0.shape)]
        + [_const_spec(c.shape) for c in consts],
        out_specs=out_specs,
        out_shape=out_shape,
        scratch_shapes=[pltpu.VMEM((rows, ns2), F32), pltpu.VMEM((chains, ns2), F32)],
        compiler_params=_params("arbitrary"),
        name="ssm_scan" if emit_y else "ssm_end_state",
    )(u_rows, h0, *consts)


def _ssm_combine_kernel(hend_ref, ar_ref, ai_ref, h0_ref, *, seg_len, n_seg):
    ns = ar_ref.shape[1]
    pr, pi = ar_ref[...], ai_ref[...]
    qr, qi = None, None
    e = seg_len
    while e:
        if e & 1:
            qr, qi = (pr, pi) if qr is None else (qr * pr - qi * pi, qr * pi + qi * pr)
        e >>= 1
        if e:
            pr, pi = pr * pr - pi * pi, 2.0 * pr * pi
    er, ei = hend_ref[:, :ns], hend_ref[:, ns:]
    seg = lax.broadcasted_iota(jnp.int32, er.shape, 0) % n_seg
    hr, hi = jnp.zeros_like(er), jnp.zeros_like(ei)
    for s in range(1, n_seg):
        xr = qr * hr - qi * hi + er
        xi = qr * hi + qi * hr + ei
        hr = jnp.where(seg == s, pltpu.roll(xr, 1, 0), hr)
        hi = jnp.where(seg == s, pltpu.roll(xi, 1, 0), hi)
    h0_ref[:, :ns] = hr
    h0_ref[:, ns:] = hi


def _ssm_combine(hend, w, seg_len, n_seg):
    return pl.pallas_call(
        functools.partial(_ssm_combine_kernel, seg_len=seg_len, n_seg=n_seg),
        out_shape=jax.ShapeDtypeStruct(hend.shape, F32),
        name="ssm_combine",
    )(hend, w["a_re"], w["a_im"])


def _flash_kernel(q_ref, k_ref, vt_ref, o_ref, s_buf, p_buf, acc_ref, m_ref, alpha_ref):
    tq = 2 * KV_BLOCK
    heads = range(FLASH_HEADS)

    def scores(q, t, slot):
        rows = pl.ds(pl.multiple_of(t * KV_BLOCK, KV_BLOCK), KV_BLOCK)
        for hh in heads:
            s_buf[slot, hh] = _dot_nt(k_ref[rows, hh * HEAD_BLOCK:(hh + 1) * HEAD_BLOCK], q[hh])

    def softmax(slot, hh, first_visible):
        s = s_buf[slot, hh]
        if first_visible is not None:
            ki = lax.broadcasted_iota(jnp.int32, s.shape, 0)
            qi = lax.broadcasted_iota(jnp.int32, s.shape, 1)
            s = jnp.where(ki + first_visible <= qi, s, NEG_INF)
        m = m_ref[hh]
        m_new = jnp.maximum(m, jnp.max(s, axis=0, keepdims=True))
        m_ref[hh] = m_new
        p_buf[slot, hh] = jnp.exp2(s - m_new).astype(BF16)
        alpha_ref[hh] = jnp.exp2(m - m_new)

    def values(t, slot, hh):
        acc_ref[hh] = (alpha_ref[hh] * acc_ref[hh]
                       + _dot(vt_ref[t, hh * VT_ROWS:(hh + 1) * VT_ROWS, :], p_buf[slot, hh]))

    def step(q, t, slot, first_visible=None):
        scores(q, t + 1, 1 - slot)
        for hh in heads:
            values(jnp.maximum(t - 1, 0), 1 - slot, hh)
        for hh in heads:
            softmax(slot, hh, first_visible)

    def load_q(i):
        rows = pl.ds(pl.multiple_of(i * tq, tq), tq)
        return rows, [q_ref[rows, hh * HEAD_BLOCK:(hh + 1) * HEAD_BLOCK] for hh in heads]

    n_q = q_ref.shape[0] // tq
    scores(load_q(0)[1], 0, 0)

    def query_block(i, carry):
        q_rows, q = load_q(i)
        p_buf[1] = jnp.zeros(p_buf.shape[1:], BF16)
        alpha_ref[...] = jnp.ones(alpha_ref.shape, F32)
        m_ref[...] = jnp.full(m_ref.shape, NEG_INF, F32)
        acc_ref[...] = jnp.zeros(acc_ref.shape, F32)

        def two_steps(u, c):
            step(q, 2 * u, 0)
            step(q, 2 * u + 1, 1)
            return c

        lax.fori_loop(0, i, two_steps, 0)
        step(q, 2 * i, 0, first_visible=0)
        scores(load_q(jnp.minimum(i + 1, n_q - 1))[1], 0, 0)
        for hh in heads:
            values(2 * i, 0, hh)
        for hh in heads:
            softmax(1, hh, KV_BLOCK)
        for hh in heads:
            values(2 * i + 1, 1, hh)
        out = [acc_ref[hh, :V_HEAD] / acc_ref[hh, V_HEAD:V_HEAD + 1] for hh in heads]
        o_ref[q_rows, :] = jnp.concatenate(out, axis=0).T.astype(o_ref.dtype)
        return carry

    lax.fori_loop(0, n_q, query_block, 0)


def _flash(q, k, vt, batch, seq):
    tq = 2 * KV_BLOCK
    hs = FLASH_HEADS
    return pl.pallas_call(
        _flash_kernel,
        grid=(batch, N_HEADS // hs),
        in_specs=[pl.BlockSpec((seq, hs * HEAD_BLOCK), lambda b, p: (b, p)),
                  pl.BlockSpec((seq, hs * HEAD_BLOCK), lambda b, p: (b, p)),
                  pl.BlockSpec((seq // KV_BLOCK, hs * VT_ROWS, KV_BLOCK), lambda b, p: (b, p, 0))],
        out_specs=pl.BlockSpec((seq, hs * V_HEAD), lambda b, p: (b, p)),
        out_shape=jax.ShapeDtypeStruct((batch * seq, N_HEADS * V_HEAD), BF16),
        scratch_shapes=[pltpu.VMEM((2, hs, KV_BLOCK, tq), F32), pltpu.VMEM((2, hs, KV_BLOCK, tq), BF16),
                        pltpu.VMEM((hs, VT_ROWS, tq), F32), pltpu.VMEM((hs, 1, tq), F32),
                        pltpu.VMEM((hs, 1, tq), F32)],
        compiler_params=_params("parallel", "parallel"),
        name="flash",
    )(q, k, vt)


def _absorb_kernel(q_ref, gk_ref, wabs_ref, sel_ref, qabs_ref, qr_ref):
    qg = (q_ref[...].astype(F32) * gk_ref[...]).astype(BF16)
    qabs_ref[...] = _dot(qg, wabs_ref[...]).astype(qabs_ref.dtype)
    qr_ref[...] = _dot(qg, sel_ref[...]).astype(qr_ref.dtype)


def _absorb(q, w):
    n = q.shape[0]
    return pl.pallas_call(
        _absorb_kernel,
        out_shape=(jax.ShapeDtypeStruct((n, w["w_abs"].shape[1]), BF16),
                   jax.ShapeDtypeStruct((n, w["sel_rope"].shape[1]), BF16)),
        compiler_params=_params(),
        name="absorb",
    )(q, w["g_kh"], w["w_abs"], w["sel_rope"])


def _paged_attn_kernel(pt_ref, qabs_ref, qr_ref, qabs_nx_ref, qr_nx_ref, ckvn_ref, kpetn_ref, wukt_ref,
                       ckv_hbm, kpet_hbm, o_ref, ckv_buf, kpe_buf, cb_buf, sc_buf, lhs_ref, m_ref, l_ref, acc_ref,
                       sem_c, sem_k, *, layer, n_chunks, chunk_pages, n_new):
    b = pl.program_id(0)
    total = pl.num_programs(0) * n_chunks
    n_k = wukt_ref.shape[0]
    rows_q = qabs_ref.shape[1]
    tk = chunk_pages * PAGE_SIZE
    part_pages = min(PART_PAGES, chunk_pages)
    parts = chunk_pages // part_pages
    qr = (qr_ref[0][:, :QK_ROPE], qr_nx_ref[0][:, :QK_ROPE])

    def start_chunk(g, slt):
        g = jnp.minimum(g, total - 1)
        for p in range(chunk_pages):
            page = pt_ref[g * chunk_pages + p]
            dst = pl.ds(p * PAGE_SIZE, PAGE_SIZE)
            pltpu.make_async_copy(ckv_hbm.at[layer, page], ckv_buf.at[slt, dst, :], sem_c.at[slt]).start()
            pltpu.make_async_copy(kpet_hbm.at[layer, page], kpe_buf.at[slt, p], sem_k.at[slt]).start()

    def wait_chunk(slt):
        pltpu.make_async_copy(ckv_buf.at[slt], ckv_buf.at[slt], sem_c.at[slt]).wait()
        pltpu.make_async_copy(kpe_buf.at[slt], kpe_buf.at[slt], sem_k.at[slt]).wait()

    def scaled_scores(which, cb, kpet):
        keys = cb.shape[0]
        both = _dot_nt(lhs_ref[which], cb)
        kt = both[:n_k]
        ssq = jnp.sum((kt * kt).reshape(QK_NOPE, N_HEADS, keys), axis=0)
        ssq = ssq + jnp.sum(kpet * kpet, axis=0, keepdims=True)
        r = lax.rsqrt(ssq * (1.0 / QK_HEAD) + EPS)
        sc = both[n_k:] + _dot(qr[which], kpet.astype(BF16))
        return (sc.reshape(n_new, N_HEADS, keys) * r[None]).reshape(rows_q, keys)

    def stage_a(slt, which, part):
        pages = range(part * part_pages, (part + 1) * part_pages)
        rows = slice(pages[0] * PAGE_SIZE, (pages[-1] + 1) * PAGE_SIZE)
        cb = ckv_buf[slt, rows, :].astype(BF16)
        cb_buf[slt, rows, :] = cb
        kpet = jnp.concatenate([kpe_buf[slt, p] for p in pages], axis=1)
        sc_buf[slt, :, rows] = scaled_scores(which, cb, kpet)

    def stage_b(sc, cb):
        m_old = m_ref[...]
        m_new = jnp.maximum(m_old, jnp.max(sc, axis=1, keepdims=True))
        p = jnp.exp2(sc - m_new)
        alpha = jnp.exp2(m_old - m_new)
        l_ref[...] = alpha * l_ref[...] + jnp.sum(p, axis=1, keepdims=True)
        acc_ref[...] = alpha * acc_ref[...] + _dot(p.astype(BF16), cb)
        m_ref[...] = m_new

    @pl.when(b == 0)
    def _():
        for which in range(2):
            lhs_ref[which, :n_k, :] = wukt_ref[...]

    lhs_ref[0, n_k:, :] = qabs_ref[0]
    lhs_ref[1, n_k:, :] = qabs_nx_ref[0]

    @pl.when(b == 0)
    def _():
        start_chunk(0, 0)
        start_chunk(1, 1)
        wait_chunk(0)
        for part in range(parts):
            stage_a(0, 0, part)

    m_ref[...] = jnp.full(m_ref.shape, NEG_INF, F32)
    l_ref[...] = jnp.zeros(l_ref.shape, F32)
    acc_ref[...] = jnp.zeros(acc_ref.shape, F32)

    for c in range(n_chunks):
        g = b * n_chunks + c
        slot, nxt = c % 2, (c + 1) % 2
        which = 0 if c + 1 < n_chunks else 1

        wait_chunk(nxt)
        start_chunk(g + 2, slot)
        stage_a(nxt, which, 0)
        stage_b(sc_buf[slot], cb_buf[slot])
        for part in range(1, parts):
            stage_a(nxt, which, part)

    @pl.when(b == pl.num_programs(0) - 1)
    def _():
        wait_chunk((n_chunks - 1) % 2)

    tkn = kpetn_ref.shape[2]
    cbn = ckvn_ref[0].astype(BF16)
    cbn = jnp.concatenate([cbn, jnp.zeros((tkn - cbn.shape[0], cbn.shape[1]), BF16)], axis=0)
    sc = scaled_scores(0, cbn, kpetn_ref[0])
    key = lax.broadcasted_iota(jnp.int32, (rows_q, tkn), 1)
    tok = lax.broadcasted_iota(jnp.int32, (rows_q, tkn), 0) // N_HEADS
    stage_b(jnp.where(key <= tok, sc, NEG_INF), cbn)
    o_ref[0] = (acc_ref[...] / l_ref[...]).astype(o_ref.dtype)


def _paged_attn(page_table, qabs, qr, ckv_new, kpet_new, w_ukt, cache_ckv, cache_kpet, layer, chunk_pages):
    n_seq, pages_per_seq = page_table.shape
    assert pages_per_seq % chunk_pages == 0 and chunk_pages % min(PART_PAGES, chunk_pages) == 0
    n_chunks = pages_per_seq // chunk_pages
    assert n_chunks % 2 == 0 and n_seq * n_chunks >= 2
    rows_q, kv_lora = qabs.shape[1:]
    n_new = rows_q // N_HEADS
    tk = chunk_pages * PAGE_SIZE
    seq_block = lambda shape: pl.BlockSpec((1,) + shape, lambda b, pt: (b, 0, 0))
    nxt_block = lambda shape: pl.BlockSpec((1,) + shape, lambda b, pt: (jnp.minimum(b + 1, n_seq - 1), 0, 0))
    grid_spec = pltpu.PrefetchScalarGridSpec(
        num_scalar_prefetch=1,
        grid=(n_seq,),
        in_specs=[seq_block(qabs.shape[1:]), seq_block(qr.shape[1:]),
                  nxt_block(qabs.shape[1:]), nxt_block(qr.shape[1:]),
                  seq_block(ckv_new.shape[1:]), seq_block(kpet_new.shape[1:]),
                  pl.BlockSpec(w_ukt.shape, lambda b, pt: (0, 0)),
                  pl.BlockSpec(memory_space=pl.ANY), pl.BlockSpec(memory_space=pl.ANY)],
        out_specs=seq_block((rows_q, kv_lora)),
        scratch_shapes=[pltpu.VMEM((2, tk, kv_lora), F32), pltpu.VMEM((2, chunk_pages, QK_ROPE, PAGE_SIZE), F32),
                        pltpu.VMEM((2, tk, kv_lora), BF16), pltpu.VMEM((2, rows_q, tk), F32),
                        pltpu.VMEM((2, w_ukt.shape[0] + rows_q, kv_lora), BF16),
                        pltpu.VMEM((rows_q, 1), F32), pltpu.VMEM((rows_q, 1), F32),
                        pltpu.VMEM((rows_q, kv_lora), F32),
                        pltpu.SemaphoreType.DMA((2,)), pltpu.SemaphoreType.DMA((2,))])
    return pl.pallas_call(
        functools.partial(_paged_attn_kernel, layer=layer, n_chunks=n_chunks, chunk_pages=chunk_pages, n_new=n_new),
        grid_spec=grid_spec,
        out_shape=jax.ShapeDtypeStruct((n_seq, rows_q, kv_lora), BF16),
        compiler_params=_params("arbitrary"),
        name="paged_attn",
    )(page_table.reshape(-1), qabs, qr, qabs, qr, ckv_new, kpet_new, w_ukt, cache_ckv, cache_kpet)


def _out_proj_kernel(x_ref, a_ref, b_ref, sga_ref, sgb_ref, p_ref, wuvb_ref, wa_ref, wb_ref, wo_ref, gffn_ref,
                     wg_ref, wup_ref, wdn_ref, gple_ref, wpg_ref, wple_ref, y_ref, *, latent_b):
    b = b_ref[...]
    if latent_b:
        b = _dot(b, wuvb_ref[...]).astype(BF16)
    merged = (sga_ref[...].astype(F32) * _dot(a_ref[...], wa_ref[...])
              + sgb_ref[...].astype(F32) * _dot(b, wb_ref[...]))
    x = x_ref[...] + _dot(merged.astype(BF16), wo_ref[...])

    def norm(t, g_ref):
        return (t * lax.rsqrt(jnp.mean(t * t, axis=-1, keepdims=True) + EPS) * g_ref[...]).astype(BF16)

    hf = norm(x, gffn_ref)
    act = jax.nn.silu(_dot(hf, wg_ref[...])) * _dot(hf, wup_ref[...])
    x = x + _dot(act.astype(BF16), wdn_ref[...])
    gate = jax.nn.sigmoid(_dot(norm(x, gple_ref), wpg_ref[...]))
    y_ref[...] = x + gate * _dot(p_ref[...].astype(BF16), wple_ref[...])


def _out_proj(x, a, b, sga, sgb, p, w, tm, latent_b):
    n, d = x.shape
    assert n % tm == 0
    row = lambda width: pl.BlockSpec((tm, width), lambda i: (i, 0))
    consts = [w[k] for k in ("w_uv_bd", "w_a", "w_b", "w_out", "g_ffn", "w_gate", "w_up", "w_down", "g_ple",
                             "w_ple_gate", "w_ple")]
    return pl.pallas_call(
        functools.partial(_out_proj_kernel, latent_b=latent_b),
        grid=(n // tm,),
        in_specs=[row(d), row(a.shape[1]), row(b.shape[1]), row(d), row(d), row(p.shape[1])]
        + [_const_spec(c.shape) for c in consts],
        out_specs=row(d),
        out_shape=jax.ShapeDtypeStruct((n, d), F32),
        compiler_params=_params("parallel"),
        name="out_proj_latent" if latent_b else "out_proj",
    )(x, a, b, sga, sgb, p, *consts)


def _head_blocked(w_nope, w_rope):
    k = w_nope.shape[0] if w_nope is not None else w_rope.shape[0]
    nope = w_nope if w_nope is not None else jnp.zeros((k, N_HEADS, QK_NOPE), F32)
    rope = w_rope if w_rope is not None else jnp.zeros((k, N_HEADS, QK_ROPE), F32)
    pad = jnp.zeros((k, N_HEADS, HEAD_BLOCK - QK_HEAD), F32)
    return jnp.concatenate([nope, rope, pad], axis=-1).reshape(k, N_HEADS * HEAD_BLOCK)


def _rot_cols(w):
    half = QK_ROPE // 2
    return jnp.concatenate([-w[..., half:], w[..., :half]], axis=-1)


def _head_gain(g_nope, g_rope_half):
    g = jnp.concatenate([g_nope, g_rope_half, g_rope_half, jnp.zeros((HEAD_BLOCK - QK_HEAD,), F32)])
    return jnp.tile(g, N_HEADS)[None, :]


def _block_diag(blocks):
    g, r, c = blocks.shape
    eye = jnp.eye(g, dtype=blocks.dtype)
    return jnp.einsum("grc,gh->grhc", blocks, eye).reshape(g * r, g * c)


def _prepare(norm_attn_g, w_in, q_norm_g, w_uq, kv_norm_g, w_uk, w_uv, q_norm_nope_g, q_norm_rope_g,
             k_norm_nope_g, k_norm_rope_g, disc, ssm_c_re, ssm_c_im, ssm_d, w_glu, w_branch_a, w_branch_b,
             w_out, norm_ffn_g, w_gate, w_up, w_down, norm_ple_g, w_ple_gate, w_ple):
    d_model = w_in.shape[0]
    ssm_w = ssm_d.shape[0]
    q_lora, kv_lora = w_uq.shape[0], w_uk.shape[0]
    o = np.cumsum([0, ssm_w, q_lora, kv_lora, QK_ROPE, d_model, d_model])
    bf = lambda a: a.astype(BF16)
    w = {}
    w["g_attn"] = norm_attn_g[None, :]
    w["w_u"] = bf(w_in[:, o[0]:o[1]])
    w["w_q"] = bf(w_in[:, o[1]:o[2]])
    w["w_kv"] = bf(w_in[:, o[2]:o[3]])
    w_kr = w_in[:, o[3]:o[4]]
    place = lambda r: jnp.pad(r, ((0, 0), (QK_NOPE, HEAD_BLOCK - QK_HEAD)))
    w["w_kr"] = bf(jnp.concatenate([place(w_kr), place(_rot_cols(w_kr))], axis=1))
    w["w_ga"] = bf(w_in[:, o[4]:o[5]])
    w["w_gb"] = bf(w_in[:, o[5]:o[6]])
    w["g_q"] = q_norm_g[None, :]
    uq_nope, uq_rope = w_uq[..., :QK_NOPE], w_uq[..., QK_NOPE:]
    w["w_uq"] = bf(jnp.concatenate([_head_blocked(uq_nope, uq_rope), _head_blocked(None, _rot_cols(uq_rope))], axis=1))
    w["g_kv"] = kv_norm_g[None, :]
    w["w_uk"] = bf(_head_blocked(w_uk, None))
    uv_t = jnp.pad(jnp.transpose(w_uv, (1, 2, 0)), ((0, 0), (0, VT_ROWS - V_HEAD), (0, 0)))
    w["w_uv_t"] = bf(uv_t.reshape(N_HEADS * VT_ROWS, kv_lora))
    w["g_qh"] = _head_gain(q_norm_nope_g, q_norm_rope_g)
    w["g_kh"] = _head_gain(k_norm_nope_g, k_norm_rope_g)
    ukt = jnp.transpose(w_uk, (1, 2, 0))
    ukt = jnp.pad(ukt, ((0, 0), (0, HEAD_BLOCK - QK_NOPE), (0, 0)))
    w["w_abs"] = bf(_block_diag(ukt))
    sel = np.zeros((N_HEADS * HEAD_BLOCK, N_HEADS * HEAD_BLOCK), np.float32)
    for h in range(N_HEADS):
        for dd in range(QK_ROPE):
            sel[h * HEAD_BLOCK + QK_NOPE + dd, h * HEAD_BLOCK + dd] = 1.0
    w["sel_rope"] = jnp.asarray(sel, BF16)
    w["w_ukt"] = bf(jnp.transpose(w_uk, (2, 1, 0)).reshape(QK_NOPE * N_HEADS, kv_lora))
    w["w_uv_bd"] = bf(_block_diag(jnp.transpose(w_uv, (1, 0, 2))))
    lb_re, lb_im, bb_re, bb_im = disc
    w["a_re"] = lb_re.reshape(1, -1)
    w["a_im"] = lb_im.reshape(1, -1)
    tr = lambda a: jnp.transpose(a, (0, 2, 1))
    gp = bb_re.shape[0] // SSM_SPLIT
    part = lambda a, p: _block_diag(tr(a[p * gp:(p + 1) * gp]))
    w["w_bu"] = bf(jnp.stack([jnp.concatenate([part(bb_re, p), part(bb_im, p)], axis=1) for p in range(SSM_SPLIT)]))
    w["w_y"] = bf(jnp.stack([jnp.concatenate([part(ssm_c_re, p), -part(ssm_c_im, p)], axis=0)
                             for p in range(SSM_SPLIT)]))
    w["ssm_d"] = ssm_d[None, :]
    w["w_glu"] = bf(w_glu)
    w["w_a"], w["w_b"], w["w_out"] = bf(w_branch_a), bf(w_branch_b), bf(w_out)
    w["g_ffn"], w["g_ple"] = norm_ffn_g[None, :], norm_ple_g[None, :]
    w["w_gate"], w["w_up"], w["w_down"] = bf(w_gate), bf(w_up), bf(w_down)
    w["w_ple_gate"], w["w_ple"] = bf(w_ple_gate), bf(w_ple)
    return w


def _rope_tables(pos):
    half = QK_ROPE // 2
    inv = ROPE_THETA ** (-jnp.arange(half, dtype=F32) / half)
    ang = pos.astype(F32)[:, None] * inv[None, :]
    cos, sin = jnp.cos(ang), jnp.sin(ang)
    n = pos.shape[0]
    ones, zeros = jnp.ones((n, QK_NOPE), F32), jnp.zeros((n, QK_NOPE), F32)
    pad = jnp.zeros((n, HEAD_BLOCK - QK_HEAD), F32)
    return (jnp.concatenate([ones, cos, cos, pad], axis=1), jnp.concatenate([zeros, sin, sin, pad], axis=1))


def _pick(n, prefs):
    for t in prefs:
        if n % t == 0:
            return t
    return n


def kernel(x_prompt, x_sample, p_prompt, p_sample, cache_ckv, cache_kpe, state_ssm_re, state_ssm_im, page_table, norm_attn_g, w_in, q_norm_g, w_uq, kv_norm_g, w_uk, w_uv, q_norm_nope_g, q_norm_rope_g, k_norm_nope_g, k_norm_rope_g, ssm_lam_re, ssm_lam_im, ssm_log_dt, ssm_b_re, ssm_b_im, ssm_c_re, ssm_c_im, ssm_d, w_glu, w_branch_a, w_branch_b, w_out, norm_ffn_g, w_gate, w_up, w_down, norm_ple_g, w_ple_gate, w_ple):
    depth = w_in.shape[0]
    batch, seq, d_model = x_prompt.shape
    n_seq, n_new, _ = x_sample.shape
    past = page_table.shape[1] * PAGE_SIZE
    groups, n_state = ssm_lam_re.shape[1:]
    ns = groups * n_state

    cos_p, sin_p = _rope_tables(jnp.arange(seq))
    cos_p, sin_p = jnp.tile(cos_p, (batch, 1)), jnp.tile(sin_p, (batch, 1))
    cos_s, sin_s = _rope_tables(past + jnp.arange(n_new))
    cos_s, sin_s = jnp.tile(cos_s, (n_seq, 1)), jnp.tile(sin_s, (n_seq, 1))

    n_seg = SUBLANES
    assert seq % n_seg == 0
    seg_len = seq // n_seg
    chains_p = batch * n_seg
    steps_p = _pick(seg_len, (16, 8, 4, 2))
    steps_s = _pick(n_new, (4, 2))
    tm_p = _pick(batch * seq, (512, 256, 128))
    tm_in = _pick(batch * seq, (1024, 512, 256))
    tm_s = _pick(n_seq * n_new, (512, 256, 128))
    assert seq % (2 * KV_BLOCK) == 0
    chunk_pages = _pick(page_table.shape[1] // 2, (32, 16, 8))

    xp = x_prompt.reshape(batch * seq, d_model)
    xs = x_sample.reshape(n_seq * n_new, d_model)
    outs = {k: [] for k in ("ckv_p", "kpe_p", "hre_p", "him_p", "ckv_s", "kpe_s", "hre_s", "him_s")}
    for i in range(depth):
        disc = _discretize(ssm_lam_re[i], ssm_lam_im[i], ssm_log_dt[i], ssm_b_re[i], ssm_b_im[i])
        w = _prepare(norm_attn_g[i], w_in[i], q_norm_g[i], w_uq[i], kv_norm_g[i], w_uk[i], w_uv[i],
                     q_norm_nope_g[i], q_norm_rope_g[i], k_norm_nope_g[i], k_norm_rope_g[i], disc,
                     ssm_c_re[i], ssm_c_im[i], ssm_d[i], w_glu[i], w_branch_a[i], w_branch_b[i], w_out[i],
                     norm_ffn_g[i], w_gate[i], w_up[i], w_down[i], norm_ple_g[i], w_ple_gate[i], w_ple[i])
        ssm_w = w["w_u"].shape[1]

        u, q, ckv, kpe, sga, sgb, k, vt = _in_proj(xp, cos_p, sin_p, w, tm_in, with_kv=True)
        u_rows = u.reshape(batch, n_seg, seg_len, ssm_w).transpose(2, 0, 1, 3).reshape(batch * seq, ssm_w)
        zero_state = jnp.zeros((chains_p, 2 * ns), F32)
        h_end = _ssm(u_rows, zero_state, w, chains_p, steps_p, emit_y=False)
        h_start = _ssm_combine(_state_from_parts(h_end), w, seg_len, n_seg)
        a_rows, h_fin = _ssm(u_rows, _state_to_parts(h_start), w, chains_p, steps_p, emit_y=True)
        h_fin = _state_from_parts(h_fin)
        a_out = a_rows.reshape(seg_len, batch, n_seg, ssm_w).transpose(1, 2, 0, 3).reshape(batch * seq, ssm_w)
        b_out = _flash(q, k, vt, batch, seq)
        xp = _out_proj(xp, a_out, b_out, sga, sgb, p_prompt[i].reshape(batch * seq, -1), w, tm_p, latent_b=False)
        h_fin = h_fin.reshape(batch, n_seg, 2 * ns)[:, -1]
        outs["ckv_p"].append(ckv.reshape(batch, seq, -1))
        outs["kpe_p"].append(kpe.reshape(batch, seq, -1))
        outs["hre_p"].append(h_fin[:, :ns].reshape(batch, groups, n_state))
        outs["him_p"].append(h_fin[:, ns:].reshape(batch, groups, n_state))

        u, q, ckv, kpe, sga, sgb = _in_proj(xs, cos_s, sin_s, w, tm_s, with_kv=False)
        u_rows = u.reshape(n_seq, n_new, ssm_w).transpose(1, 0, 2).reshape(n_seq * n_new, ssm_w)
        h0 = jnp.concatenate([state_ssm_re[i].reshape(n_seq, ns), state_ssm_im[i].reshape(n_seq, ns)], axis=1)
        a_rows, h_fin = _ssm(u_rows, _state_to_parts(h0), w, n_seq, steps_s, emit_y=True)
        h_fin = _state_from_parts(h_fin)
        a_out = a_rows.reshape(n_new, n_seq, ssm_w).transpose(1, 0, 2).reshape(n_seq * n_new, ssm_w)
        qabs, qr = _absorb(q, w)
        rows_q = n_new * N_HEADS
        kv_lora = ckv.shape[1]
        pad_new = PAGE_SIZE - n_new
        ckv_new = ckv.reshape(n_seq, n_new, kv_lora)
        kpet_new = jnp.pad(kpe.reshape(n_seq, n_new, QK_ROPE).transpose(0, 2, 1), ((0, 0), (0, 0), (0, pad_new)))
        o_lat = _paged_attn(page_table, qabs.reshape(n_seq, rows_q, kv_lora), qr.reshape(n_seq, rows_q, HEAD_BLOCK),
                            ckv_new, kpet_new, w["w_ukt"], cache_ckv, jnp.swapaxes(cache_kpe, 2, 3), i, chunk_pages)
        b_lat = o_lat.reshape(n_seq * n_new, N_HEADS * kv_lora)
        xs = _out_proj(xs, a_out, b_lat, sga, sgb, p_sample[i].reshape(n_seq * n_new, -1), w, tm_s, latent_b=True)
        outs["ckv_s"].append(ckv.reshape(n_seq, n_new, -1))
        outs["kpe_s"].append(kpe.reshape(n_seq, n_new, -1))
        outs["hre_s"].append(h_fin[:, :ns].reshape(n_seq, groups, n_state))
        outs["him_s"].append(h_fin[:, ns:].reshape(n_seq, groups, n_state))

    st = lambda key: jnp.stack(outs[key])
    return (xp.reshape(batch, seq, d_model), xs.reshape(n_seq, n_new, d_model),
            st("ckv_p"), st("kpe_p"), st("hre_p"), st("him_p"),
            st("ckv_s"), st("kpe_s"), st("hre_s"), st("him_s"))
```

```python
import functools
import math

import numpy as np
import jax
import jax.numpy as jnp
from jax import lax
from jax.experimental import pallas as pl
from jax.experimental.pallas import tpu as pltpu

N_HEADS = 8
QK_NOPE = 64
QK_ROPE = 32
QK_HEAD = QK_NOPE + QK_ROPE
V_HEAD = 64
VT_ROWS = 80
HEAD_BLOCK = 128
ROPE_THETA = 10000.0
ATTN_SCALE = QK_HEAD ** -0.5
LOG2E = math.log2(math.e)
SSM_GROUP = 16
SSM_STATE = 64
PAGE_SIZE = 128
KV_BLOCK = 256
FLASH_HEADS = 2
PART_PAGES = 16
SSM_SPLIT = 2
EPS = 1e-6

LANES = 128
SUBLANES = 8
VMEM_LIMIT_BYTES = 56 * 1024 * 1024

BF16 = jnp.bfloat16
F32 = jnp.float32
NEG_INF = float("-inf")


def _dot(a, b):
    return jnp.dot(a, b, preferred_element_type=F32)


def _dot_nt(a, b):
    return lax.dot_general(a, b, (((1,), (1,)), ((), ())), preferred_element_type=F32)


def _const_spec(shape):
    nd = len(shape)
    return pl.BlockSpec(shape, lambda *_: (0,) * nd, pipeline_mode=pl.Buffered(1))


def _params(*sem):
    return pltpu.CompilerParams(dimension_semantics=sem, vmem_limit_bytes=VMEM_LIMIT_BYTES)


def _discretize_kernel(lr_ref, li_ref, ldt_ref, br_ref, bi_ref, lbr_ref, lbi_ref, bbr_ref, bbi_ref):
    lr, li = lr_ref[...], li_ref[...]
    dt = jnp.exp(ldt_ref[...])
    mag = jnp.exp(lr * dt)
    lb_re, lb_im = mag * jnp.cos(li * dt), mag * jnp.sin(li * dt)
    num_re, num_im = lb_re - 1.0, lb_im
    den = lr * lr + li * li
    f_re = (num_re * lr + num_im * li) / den
    f_im = (num_im * lr - num_re * li) / den
    br, bi = br_ref[...], bi_ref[...]
    lbr_ref[...] = lb_re
    lbi_ref[...] = lb_im
    bbr_ref[...] = f_re * br - f_im * bi
    bbi_ref[...] = f_re * bi + f_im * br


def _discretize(lam_re, lam_im, log_dt, b_re, b_im):
    g, n, c = b_re.shape
    rep = lambda a: jnp.repeat(a, c, axis=-1)
    args = (rep(lam_re), rep(lam_im), jnp.broadcast_to(log_dt[:, None], (g, n * c)),
            b_re.reshape(g, n * c), b_im.reshape(g, n * c))
    sds = jax.ShapeDtypeStruct((g, n * c), F32)
    lbr, lbi, bbr, bbi = pl.pallas_call(_discretize_kernel, out_shape=(sds,) * 4, name="discretize")(*args)
    lb_re = lbr.reshape(g, n, c)[..., 0]
    lb_im = lbi.reshape(g, n, c)[..., 0]
    return lb_re, lb_im, bbr.reshape(g, n, c), bbi.reshape(g, n, c)


def _in_proj_kernel(x_ref, cos_ref, sin_ref, gattn_ref, wu_ref, wq_ref, wkv_ref, wkr_ref, wga_ref, wgb_ref,
                    gq_ref, wuq_ref, gkv_ref, wuk_ref, wuv_ref, gqh_ref, gkh_ref,
                    u_ref, q_ref, ckv_ref, kpe_ref, sga_ref, sgb_ref, *kv_refs):
    x = x_ref[...]
    h = x * lax.rsqrt(jnp.mean(x * x, axis=-1, keepdims=True) + EPS) * gattn_ref[...]
    hb = h.astype(BF16)
    cos, sin = cos_ref[...], sin_ref[...]
    cos8 = jnp.concatenate([cos] * N_HEADS, axis=1)
    sin8 = jnp.concatenate([sin] * N_HEADS, axis=1)

    u_ref[...] = _dot(hb, wu_ref[...])
    sga_ref[...] = jax.nn.sigmoid(_dot(hb, wga_ref[...])).astype(sga_ref.dtype)
    sgb_ref[...] = jax.nn.sigmoid(_dot(hb, wgb_ref[...])).astype(sgb_ref.dtype)

    def head_norm(t, gain):
        blocks = []
        for hd in range(N_HEADS):
            blk = t[:, hd * HEAD_BLOCK:(hd + 1) * HEAD_BLOCK]
            ssq = jnp.sum(blk * blk, axis=-1, keepdims=True)
            blocks.append(blk * lax.rsqrt(ssq * (1.0 / QK_HEAD) + EPS))
        return jnp.concatenate(blocks, axis=1) * gain

    q_lat = _dot(hb, wq_ref[...])
    qc = q_lat * lax.rsqrt(jnp.mean(q_lat * q_lat, axis=-1, keepdims=True) + EPS) * gq_ref[...]
    qq = _dot(qc.astype(BF16), wuq_ref[...])
    hw = N_HEADS * HEAD_BLOCK
    q_pre = qq[:, :hw] * cos8 + qq[:, hw:] * sin8
    q_ref[...] = (head_norm(q_pre, gqh_ref[...]) * (ATTN_SCALE * LOG2E)).astype(q_ref.dtype)

    kv_lat = _dot(hb, wkv_ref[...])
    ckv = kv_lat * lax.rsqrt(jnp.mean(kv_lat * kv_lat, axis=-1, keepdims=True) + EPS) * gkv_ref[...]
    ckv_ref[...] = ckv
    kr = _dot(hb, wkr_ref[...])
    kpe_wide = kr[:, :HEAD_BLOCK] * cos + kr[:, HEAD_BLOCK:] * sin
    kpe_ref[...] = kpe_wide[:, QK_NOPE:QK_NOPE + QK_ROPE]

    if not kv_refs:
        return
    k_ref, vt_ref = kv_refs
    cb = ckv.astype(BF16)
    k_pre = _dot(cb, wuk_ref[...]) + jnp.concatenate([kpe_wide] * N_HEADS, axis=1)
    k_ref[...] = head_norm(k_pre, gkh_ref[...]).astype(k_ref.dtype)
    vt = _dot_nt(wuv_ref[...], cb)
    ones_row = lax.broadcasted_iota(jnp.int32, vt.shape, 0) % VT_ROWS == V_HEAD
    vt = jnp.where(ones_row, 1.0, vt).astype(vt_ref.dtype)
    for c in range(vt_ref.shape[0]):
        vt_ref[c] = vt[:, c * KV_BLOCK:(c + 1) * KV_BLOCK]


def _in_proj(x, cos_t, sin_t, w, tm, with_kv):
    n, d = x.shape
    assert n % tm == 0
    row = lambda width: pl.BlockSpec((tm, width), lambda i: (i, 0))
    consts = [w[k] for k in ("g_attn", "w_u", "w_q", "w_kv", "w_kr", "w_ga", "w_gb", "g_q", "w_uq", "g_kv",
                             "w_uk", "w_uv_t", "g_qh", "g_kh")]
    hw = N_HEADS * HEAD_BLOCK
    out_shape = (jax.ShapeDtypeStruct((n, w["w_u"].shape[1]), F32),
                 jax.ShapeDtypeStruct((n, hw), BF16),
                 jax.ShapeDtypeStruct((n, w["w_kv"].shape[1]), F32),
                 jax.ShapeDtypeStruct((n, QK_ROPE), F32),
                 jax.ShapeDtypeStruct((n, d), BF16),
                 jax.ShapeDtypeStruct((n, d), BF16))
    if with_kv:
        assert tm % KV_BLOCK == 0
        out_shape += (jax.ShapeDtypeStruct((n, hw), BF16),
                      jax.ShapeDtypeStruct((n // KV_BLOCK, N_HEADS * VT_ROWS, KV_BLOCK), BF16))
    return pl.pallas_call(
        _in_proj_kernel,
        grid=(n // tm,),
        in_specs=[row(d), row(LANES), row(LANES)] + [_const_spec(c.shape) for c in consts],
        out_specs=tuple(pl.BlockSpec((tm // KV_BLOCK,) + s.shape[1:], lambda i: (i, 0, 0)) if len(s.shape) == 3
                        else row(s.shape[1]) for s in out_shape),
        out_shape=out_shape,
        compiler_params=_params("parallel"),
        name="in_proj",
    )(x, cos_t, sin_t, *consts)


def _ssm_kernel(u_ref, h0_ref, ar_ref, ai_ref, wbu_ref, wy_ref, d_ref, wglu_ref,
                *rest, chains, steps, emit_y):
    if emit_y:
        a_ref, ht_ref, bu_ref, hc_ref = rest
    else:
        ht_ref, bu_ref, hc_ref = rest
    i = pl.program_id(0)
    nh = ar_ref.shape[1] // SSM_SPLIT
    cw = u_ref.shape[1] // SSM_SPLIT

    @pl.when(i == 0)
    def _():
        hc_ref[...] = h0_ref[...]

    u = u_ref[...]
    ub = u.astype(BF16)
    for p in range(SSM_SPLIT):
        bu_ref[:, p * 2 * nh:(p + 1) * 2 * nh] = _dot(ub[:, p * cw:(p + 1) * cw], wbu_ref[p])

    def advance(prev_ref, prev_rows, rows):
        for p in range(SSM_SPLIT):
            re, im = slice(p * 2 * nh, p * 2 * nh + nh), slice(p * 2 * nh + nh, (p + 1) * 2 * nh)
            ar, ai = ar_ref[:, p * nh:(p + 1) * nh], ai_ref[:, p * nh:(p + 1) * nh]
            hr, hi = prev_ref[prev_rows, re], prev_ref[prev_rows, im]
            nr = ar * hr - ai * hi + bu_ref[rows, re]
            ni = ar * hi + ai * hr + bu_ref[rows, im]
            bu_ref[rows, re] = nr
            bu_ref[rows, im] = ni

    row_block = lambda j: pl.ds(pl.multiple_of(j * chains, SUBLANES), chains)
    advance(hc_ref, slice(None), row_block(0))

    def step(j, carry):
        advance(bu_ref, row_block(j - 1), row_block(j))
        return carry

    lax.fori_loop(1, steps, step, 0)
    last = bu_ref[(steps - 1) * chains:, :]
    hc_ref[...] = last
    ht_ref[...] = last
    if emit_y:
        hb = bu_ref[...].astype(BF16)
        y = jnp.concatenate([_dot(hb[:, p * 2 * nh:(p + 1) * 2 * nh], wy_ref[p]) for p in range(SSM_SPLIT)], axis=1)
        z = jax.nn.gelu(y + d_ref[...] * u)
        a_ref[...] = (z * jax.nn.sigmoid(_dot(z.astype(BF16), wglu_ref[...]))).astype(a_ref.dtype)


def _state_to_parts(h):
    c, n2 = h.shape
    return h.reshape(c, 2, SSM_SPLIT, n2 // (2 * SSM_SPLIT)).transpose(0, 2, 1, 3).reshape(c, n2)


def _state_from_parts(h):
    c, n2 = h.shape
    return h.reshape(c, SSM_SPLIT, 2, n2 // (2 * SSM_SPLIT)).transpose(0, 2, 1, 3).reshape(c, n2)


def _ssm(u_rows, h0, w, chains, steps, emit_y):
    n, width = u_rows.shape
    rows = chains * steps
    assert n % rows == 0
    ns2 = h0.shape[1]
    consts = [w[k] for k in ("a_re", "a_im", "w_bu", "w_y", "ssm_d", "w_glu")]
    ht_sds = jax.ShapeDtypeStruct((chains, ns2), F32)
    ht_spec = pl.BlockSpec((chains, ns2), lambda i: (0, 0))
    if emit_y:
        out_shape = (jax.ShapeDtypeStruct((n, width), BF16), ht_sds)
        out_specs = (pl.BlockSpec((rows, width), lambda i: (i, 0)), ht_spec)
    else:
        out_shape, out_specs = ht_sds, ht_spec
    return pl.pallas_call(
        functools.partial(_ssm_kernel, chains=chains, steps=steps, emit_y=emit_y),
        grid=(n // rows,),
        in_specs=[pl.BlockSpec((rows, width), lambda i: (i, 0)), _const_spec(h0.shape)]
        + [_const_spec(c.shape) for c in consts],
        out_specs=out_specs,
        out_shape=out_shape,
        scratch_shapes=[pltpu.VMEM((rows, ns2), F32), pltpu.VMEM((chains, ns2), F32)],
        compiler_params=_params("arbitrary"),
        name="ssm_scan" if emit_y else "ssm_end_state",
    )(u_rows, h0, *consts)


def _ssm_combine_kernel(hend_ref, ar_ref, ai_ref, h0_ref, *, seg_len, n_seg):
    ns = ar_ref.shape[1]
    pr, pi = ar_ref[...], ai_ref[...]
    qr, qi = None, None
    e = seg_len
    while e:
        if e & 1:
            qr, qi = (pr, pi) if qr is None else (qr * pr - qi * pi, qr * pi + qi * pr)
        e >>= 1
        if e:
            pr, pi = pr * pr - pi * pi, 2.0 * pr * pi
    er, ei = hend_ref[:, :ns], hend_ref[:, ns:]
    seg = lax.broadcasted_iota(jnp.int32, er.shape, 0) % n_seg
    hr, hi = jnp.zeros_like(er), jnp.zeros_like(ei)
    for s in range(1, n_seg):
        xr = qr * hr - qi * hi + er
        xi = qr * hi + qi * hr + ei
        hr = jnp.where(seg == s, pltpu.roll(xr, 1, 0), hr)
        hi = jnp.where(seg == s, pltpu.roll(xi, 1, 0), hi)
    h0_ref[:, :ns] = hr
    h0_ref[:, ns:] = hi


def _ssm_combine(hend, w, seg_len, n_seg):
    return pl.pallas_call(
        functools.partial(_ssm_combine_kernel, seg_len=seg_len, n_seg=n_seg),
        out_shape=jax.ShapeDtypeStruct(hend.shape, F32),
        name="ssm_combine",
    )(hend, w["a_re"], w["a_im"])


def _flash_kernel(q_ref, k_ref, vt_ref, o_ref, s_buf, p_buf, acc_ref, m_ref, alpha_ref):
    tq = 2 * KV_BLOCK
    heads = range(FLASH_HEADS)

    def scores(q, t, slot):
        rows = pl.ds(pl.multiple_of(t * KV_BLOCK, KV_BLOCK), KV_BLOCK)
        for hh in heads:
            s_buf[slot, hh] = _dot_nt(k_ref[rows, hh * HEAD_BLOCK:(hh + 1) * HEAD_BLOCK], q[hh])

    def softmax(slot, hh, first_visible):
        s = s_buf[slot, hh]
        if first_visible is not None:
            ki = lax.broadcasted_iota(jnp.int32, s.shape, 0)
            qi = lax.broadcasted_iota(jnp.int32, s.shape, 1)
            s = jnp.where(ki + first_visible <= qi, s, NEG_INF)
        m = m_ref[hh]
        m_new = jnp.maximum(m, jnp.max(s, axis=0, keepdims=True))
        m_ref[hh] = m_new
        p_buf[slot, hh] = jnp.exp2(s - m_new).astype(BF16)
        alpha_ref[hh] = jnp.exp2(m - m_new)

    def values(t, slot, hh):
        acc_ref[hh] = (alpha_ref[hh] * acc_ref[hh]
                       + _dot(vt_ref[t, hh * VT_ROWS:(hh + 1) * VT_ROWS, :], p_buf[slot, hh]))

    def step(q, t, slot, first_visible=None):
        for hh in heads:
            values(jnp.maximum(t - 1, 0), 1 - slot, hh)
        scores(q, t + 1, 1 - slot)
        for hh in heads:
            softmax(slot, hh, first_visible)

    def load_q(i):
        rows = pl.ds(pl.multiple_of(i * tq, tq), tq)
        return rows, [q_ref[rows, hh * HEAD_BLOCK:(hh + 1) * HEAD_BLOCK] for hh in heads]

    n_q = q_ref.shape[0] // tq
    scores(load_q(0)[1], 0, 0)

    def query_block(i, carry):
        q_rows, q = load_q(i)
        p_buf[1] = jnp.zeros(p_buf.shape[1:], BF16)
        alpha_ref[...] = jnp.ones(alpha_ref.shape, F32)
        m_ref[...] = jnp.full(m_ref.shape, NEG_INF, F32)
        acc_ref[...] = jnp.zeros(acc_ref.shape, F32)

        def two_steps(u, c):
            step(q, 2 * u, 0)
            step(q, 2 * u + 1, 1)
            return c

        lax.fori_loop(0, i, two_steps, 0)
        step(q, 2 * i, 0, first_visible=0)
        for hh in heads:
            values(2 * i, 0, hh)
        scores(load_q(jnp.minimum(i + 1, n_q - 1))[1], 0, 0)
        for hh in heads:
            softmax(1, hh, KV_BLOCK)
        for hh in heads:
            values(2 * i + 1, 1, hh)
        out = [acc_ref[hh, :V_HEAD] / acc_ref[hh, V_HEAD:V_HEAD + 1] for hh in heads]
        o_ref[q_rows, :] = jnp.concatenate(out, axis=0).T.astype(o_ref.dtype)
        return carry

    lax.fori_loop(0, n_q, query_block, 0)


def _flash(q, k, vt, batch, seq):
    tq = 2 * KV_BLOCK
    hs = FLASH_HEADS
    return pl.pallas_call(
        _flash_kernel,
        grid=(batch, N_HEADS // hs),
        in_specs=[pl.BlockSpec((seq, hs * HEAD_BLOCK), lambda b, p: (b, p)),
                  pl.BlockSpec((seq, hs * HEAD_BLOCK), lambda b, p: (b, p)),
                  pl.BlockSpec((seq // KV_BLOCK, hs * VT_ROWS, KV_BLOCK), lambda b, p: (b, p, 0))],
        out_specs=pl.BlockSpec((seq, hs * V_HEAD), lambda b, p: (b, p)),
        out_shape=jax.ShapeDtypeStruct((batch * seq, N_HEADS * V_HEAD), BF16),
        scratch_shapes=[pltpu.VMEM((2, hs, KV_BLOCK, tq), F32), pltpu.VMEM((2, hs, KV_BLOCK, tq), BF16),
                        pltpu.VMEM((hs, VT_ROWS, tq), F32), pltpu.VMEM((hs, 1, tq), F32),
                        pltpu.VMEM((hs, 1, tq), F32)],
        compiler_params=_params("parallel", "parallel"),
        name="flash",
    )(q, k, vt)


def _absorb_kernel(q_ref, gk_ref, wabs_ref, sel_ref, qabs_ref, qr_ref):
    qg = (q_ref[...].astype(F32) * gk_ref[...]).astype(BF16)
    qabs_ref[...] = _dot(qg, wabs_ref[...]).astype(qabs_ref.dtype)
    qr_ref[...] = _dot(qg, sel_ref[...]).astype(qr_ref.dtype)


def _absorb(q, w):
    n = q.shape[0]
    return pl.pallas_call(
        _absorb_kernel,
        out_shape=(jax.ShapeDtypeStruct((n, w["w_abs"].shape[1]), BF16),
                   jax.ShapeDtypeStruct((n, w["sel_rope"].shape[1]), BF16)),
        compiler_params=_params(),
        name="absorb",
    )(q, w["g_kh"], w["w_abs"], w["sel_rope"])


def _paged_attn_kernel(pt_ref, qabs_ref, qr_ref, qabs_nx_ref, qr_nx_ref, ckvn_ref, kpetn_ref, wukt_ref,
                       ckv_hbm, kpet_hbm, o_ref, ckv_buf, kpe_buf, cb_buf, sc_buf, lhs_ref, m_ref, l_ref, acc_ref,
                       sem_c, sem_k, *, layer, n_chunks, chunk_pages, n_new):
    b = pl.program_id(0)
    total = pl.num_programs(0) * n_chunks
    n_k = wukt_ref.shape[0]
    rows_q = qabs_ref.shape[1]
    tk = chunk_pages * PAGE_SIZE
    part_pages = min(PART_PAGES, chunk_pages)
    parts = chunk_pages // part_pages
    qr = (qr_ref[0][:, :QK_ROPE], qr_nx_ref[0][:, :QK_ROPE])

    def start_chunk(g, slt):
        g = jnp.minimum(g, total - 1)
        for p in range(chunk_pages):
            page = pt_ref[g * chunk_pages + p]
            dst = pl.ds(p * PAGE_SIZE, PAGE_SIZE)
            pltpu.make_async_copy(ckv_hbm.at[layer, page], ckv_buf.at[slt, dst, :], sem_c.at[slt]).start()
            pltpu.make_async_copy(kpet_hbm.at[layer, page], kpe_buf.at[slt, p], sem_k.at[slt]).start()

    def wait_chunk(slt):
        pltpu.make_async_copy(ckv_buf.at[slt], ckv_buf.at[slt], sem_c.at[slt]).wait()
        pltpu.make_async_copy(kpe_buf.at[slt], kpe_buf.at[slt], sem_k.at[slt]).wait()

    def scaled_scores(which, cb, kpet):
        keys = cb.shape[0]
        both = _dot_nt(lhs_ref[which], cb)
        kt = both[:n_k]
        ssq = jnp.sum((kt * kt).reshape(QK_NOPE, N_HEADS, keys), axis=0)
        ssq = ssq + jnp.sum(kpet * kpet, axis=0, keepdims=True)
        r = lax.rsqrt(ssq * (1.0 / QK_HEAD) + EPS)
        sc = both[n_k:] + _dot(qr[which], kpet.astype(BF16))
        return (sc.reshape(n_new, N_HEADS, keys) * r[None]).reshape(rows_q, keys)

    def stage_a(slt, which, part):
        pages = range(part * part_pages, (part + 1) * part_pages)
        rows = slice(pages[0] * PAGE_SIZE, (pages[-1] + 1) * PAGE_SIZE)
        cb = ckv_buf[slt, rows, :].astype(BF16)
        cb_buf[slt, rows, :] = cb
        kpet = jnp.concatenate([kpe_buf[slt, p] for p in pages], axis=1)
        sc_buf[slt, :, rows] = scaled_scores(which, cb, kpet)

    def stage_b(sc, cb):
        m_old = m_ref[...]
        m_new = jnp.maximum(m_old, jnp.max(sc, axis=1, keepdims=True))
        p = jnp.exp2(sc - m_new)
        alpha = jnp.exp2(m_old - m_new)
        l_ref[...] = alpha * l_ref[...] + jnp.sum(p, axis=1, keepdims=True)
        acc_ref[...] = alpha * acc_ref[...] + _dot(p.astype(BF16), cb)
        m_ref[...] = m_new

    @pl.when(b == 0)
    def _():
        for which in range(2):
            lhs_ref[which, :n_k, :] = wukt_ref[...]

    lhs_ref[0, n_k:, :] = qabs_ref[0]
    lhs_ref[1, n_k:, :] = qabs_nx_ref[0]

    @pl.when(b == 0)
    def _():
        start_chunk(0, 0)
        start_chunk(1, 1)
        wait_chunk(0)
        for part in range(parts):
            stage_a(0, 0, part)

    m_ref[...] = jnp.full(m_ref.shape, NEG_INF, F32)
    l_ref[...] = jnp.zeros(l_ref.shape, F32)
    acc_ref[...] = jnp.zeros(acc_ref.shape, F32)

    for c in range(n_chunks):
        g = b * n_chunks + c
        slot, nxt = c % 2, (c + 1) % 2
        which = 0 if c + 1 < n_chunks else 1

        wait_chunk(nxt)
        start_chunk(g + 2, slot)
        stage_a(nxt, which, 0)
        stage_b(sc_buf[slot], cb_buf[slot])
        for part in range(1, parts):
            stage_a(nxt, which, part)

    @pl.when(b == pl.num_programs(0) - 1)
    def _():
        wait_chunk((n_chunks - 1) % 2)

    tkn = kpetn_ref.shape[2]
    cbn = ckvn_ref[0].astype(BF16)
    cbn = jnp.concatenate([cbn, jnp.zeros((tkn - cbn.shape[0], cbn.shape[1]), BF16)], axis=0)
    sc = scaled_scores(0, cbn, kpetn_ref[0])
    key = lax.broadcasted_iota(jnp.int32, (rows_q, tkn), 1)
    tok = lax.broadcasted_iota(jnp.int32, (rows_q, tkn), 0) // N_HEADS
    stage_b(jnp.where(key <= tok, sc, NEG_INF), cbn)
    o_ref[0] = (acc_ref[...] / l_ref[...]).astype(o_ref.dtype)


def _paged_attn(page_table, qabs, qr, ckv_new, kpet_new, w_ukt, cache_ckv, cache_kpet, layer, chunk_pages):
    n_seq, pages_per_seq = page_table.shape
    assert pages_per_seq % chunk_pages == 0 and chunk_pages % min(PART_PAGES, chunk_pages) == 0
    n_chunks = pages_per_seq // chunk_pages
    assert n_chunks % 2 == 0 and n_seq * n_chunks >= 2
    rows_q, kv_lora = qabs.shape[1:]
    n_new = rows_q // N_HEADS
    tk = chunk_pages * PAGE_SIZE
    seq_block = lambda shape: pl.BlockSpec((1,) + shape, lambda b, pt: (b, 0, 0))
    nxt_block = lambda shape: pl.BlockSpec((1,) + shape, lambda b, pt: (jnp.minimum(b + 1, n_seq - 1), 0, 0))
    grid_spec = pltpu.PrefetchScalarGridSpec(
        num_scalar_prefetch=1,
        grid=(n_seq,),
        in_specs=[seq_block(qabs.shape[1:]), seq_block(qr.shape[1:]),
                  nxt_block(qabs.shape[1:]), nxt_block(qr.shape[1:]),
                  seq_block(ckv_new.shape[1:]), seq_block(kpet_new.shape[1:]),
                  pl.BlockSpec(w_ukt.shape, lambda b, pt: (0, 0)),
                  pl.BlockSpec(memory_space=pl.ANY), pl.BlockSpec(memory_space=pl.ANY)],
        out_specs=seq_block((rows_q, kv_lora)),
        scratch_shapes=[pltpu.VMEM((2, tk, kv_lora), F32), pltpu.VMEM((2, chunk_pages, QK_ROPE, PAGE_SIZE), F32),
                        pltpu.VMEM((2, tk, kv_lora), BF16), pltpu.VMEM((2, rows_q, tk), F32),
                        pltpu.VMEM((2, w_ukt.shape[0] + rows_q, kv_lora), BF16),
                        pltpu.VMEM((rows_q, 1), F32), pltpu.VMEM((rows_q, 1), F32),
                        pltpu.VMEM((rows_q, kv_lora), F32),
                        pltpu.SemaphoreType.DMA((2,)), pltpu.SemaphoreType.DMA((2,))])
    return pl.pallas_call(
        functools.partial(_paged_attn_kernel, layer=layer, n_chunks=n_chunks, chunk_pages=chunk_pages, n_new=n_new),
        grid_spec=grid_spec,
        out_shape=jax.ShapeDtypeStruct((n_seq, rows_q, kv_lora), BF16),
        compiler_params=_params("arbitrary"),
        name="paged_attn",
    )(page_table.reshape(-1), qabs, qr, qabs, qr, ckv_new, kpet_new, w_ukt, cache_ckv, cache_kpet)


def _out_proj_kernel(x_ref, a_ref, b_ref, sga_ref, sgb_ref, p_ref, wuvb_ref, wa_ref, wb_ref, wo_ref, gffn_ref,
                     wg_ref, wup_ref, wdn_ref, gple_ref, wpg_ref, wple_ref, y_ref, *, latent_b):
    b = b_ref[...]
    if latent_b:
        b = _dot(b, wuvb_ref[...]).astype(BF16)
    merged = (sga_ref[...].astype(F32) * _dot(a_ref[...], wa_ref[...])
              + sgb_ref[...].astype(F32) * _dot(b, wb_ref[...]))
    x = x_ref[...] + _dot(merged.astype(BF16), wo_ref[...])

    def norm(t, g_ref):
        return (t * lax.rsqrt(jnp.mean(t * t, axis=-1, keepdims=True) + EPS) * g_ref[...]).astype(BF16)

    hf = norm(x, gffn_ref)
    act = jax.nn.silu(_dot(hf, wg_ref[...])) * _dot(hf, wup_ref[...])
    x = x + _dot(act.astype(BF16), wdn_ref[...])
    gate = jax.nn.sigmoid(_dot(norm(x, gple_ref), wpg_ref[...]))
    y_ref[...] = x + gate * _dot(p_ref[...].astype(BF16), wple_ref[...])


def _out_proj(x, a, b, sga, sgb, p, w, tm, latent_b):
    n, d = x.shape
    assert n % tm == 0
    row = lambda width: pl.BlockSpec((tm, width), lambda i: (i, 0))
    consts = [w[k] for k in ("w_uv_bd", "w_a", "w_b", "w_out", "g_ffn", "w_gate", "w_up", "w_down", "g_ple",
                             "w_ple_gate", "w_ple")]
    return pl.pallas_call(
        functools.partial(_out_proj_kernel, latent_b=latent_b),
        grid=(n // tm,),
        in_specs=[row(d), row(a.shape[1]), row(b.shape[1]), row(d), row(d), row(p.shape[1])]
        + [_const_spec(c.shape) for c in consts],
        out_specs=row(d),
        out_shape=jax.ShapeDtypeStruct((n, d), F32),
        compiler_params=_params("parallel"),
        name="out_proj_latent" if latent_b else "out_proj",
    )(x, a, b, sga, sgb, p, *consts)


def _head_blocked(w_nope, w_rope):
    k = w_nope.shape[0] if w_nope is not None else w_rope.shape[0]
    nope = w_nope if w_nope is not None else jnp.zeros((k, N_HEADS, QK_NOPE), F32)
    rope = w_rope if w_rope is not None else jnp.zeros((k, N_HEADS, QK_ROPE), F32)
    pad = jnp.zeros((k, N_HEADS, HEAD_BLOCK - QK_HEAD), F32)
    return jnp.concatenate([nope, rope, pad], axis=-1).reshape(k, N_HEADS * HEAD_BLOCK)


def _rot_cols(w):
    half = QK_ROPE // 2
    return jnp.concatenate([-w[..., half:], w[..., :half]], axis=-1)


def _head_gain(g_nope, g_rope_half):
    g = jnp.concatenate([g_nope, g_rope_half, g_rope_half, jnp.zeros((HEAD_BLOCK - QK_HEAD,), F32)])
    return jnp.tile(g, N_HEADS)[None, :]


def _block_diag(blocks):
    g, r, c = blocks.shape
    eye = jnp.eye(g, dtype=blocks.dtype)
    return jnp.einsum("grc,gh->grhc", blocks, eye).reshape(g * r, g * c)


def _prepare(norm_attn_g, w_in, q_norm_g, w_uq, kv_norm_g, w_uk, w_uv, q_norm_nope_g, q_norm_rope_g,
             k_norm_nope_g, k_norm_rope_g, disc, ssm_c_re, ssm_c_im, ssm_d, w_glu, w_branch_a, w_branch_b,
             w_out, norm_ffn_g, w_gate, w_up, w_down, norm_ple_g, w_ple_gate, w_ple):
    d_model = w_in.shape[0]
    ssm_w = ssm_d.shape[0]
    q_lora, kv_lora = w_uq.shape[0], w_uk.shape[0]
    o = np.cumsum([0, ssm_w, q_lora, kv_lora, QK_ROPE, d_model, d_model])
    bf = lambda a: a.astype(BF16)
    w = {}
    w["g_attn"] = norm_attn_g[None, :]
    w["w_u"] = bf(w_in[:, o[0]:o[1]])
    w["w_q"] = bf(w_in[:, o[1]:o[2]])
    w["w_kv"] = bf(w_in[:, o[2]:o[3]])
    w_kr = w_in[:, o[3]:o[4]]
    place = lambda r: jnp.pad(r, ((0, 0), (QK_NOPE, HEAD_BLOCK - QK_HEAD)))
    w["w_kr"] = bf(jnp.concatenate([place(w_kr), place(_rot_cols(w_kr))], axis=1))
    w["w_ga"] = bf(w_in[:, o[4]:o[5]])
    w["w_gb"] = bf(w_in[:, o[5]:o[6]])
    w["g_q"] = q_norm_g[None, :]
    uq_nope, uq_rope = w_uq[..., :QK_NOPE], w_uq[..., QK_NOPE:]
    w["w_uq"] = bf(jnp.concatenate([_head_blocked(uq_nope, uq_rope), _head_blocked(None, _rot_cols(uq_rope))], axis=1))
    w["g_kv"] = kv_norm_g[None, :]
    w["w_uk"] = bf(_head_blocked(w_uk, None))
    uv_t = jnp.pad(jnp.transpose(w_uv, (1, 2, 0)), ((0, 0), (0, VT_ROWS - V_HEAD), (0, 0)))
    w["w_uv_t"] = bf(uv_t.reshape(N_HEADS * VT_ROWS, kv_lora))
    w["g_qh"] = _head_gain(q_norm_nope_g, q_norm_rope_g)
    w["g_kh"] = _head_gain(k_norm_nope_g, k_norm_rope_g)
    ukt = jnp.transpose(w_uk, (1, 2, 0))
    ukt = jnp.pad(ukt, ((0, 0), (0, HEAD_BLOCK - QK_NOPE), (0, 0)))
    w["w_abs"] = bf(_block_diag(ukt))
    sel = np.zeros((N_HEADS * HEAD_BLOCK, N_HEADS * HEAD_BLOCK), np.float32)
    for h in range(N_HEADS):
        for dd in range(QK_ROPE):
            sel[h * HEAD_BLOCK + QK_NOPE + dd, h * HEAD_BLOCK + dd] = 1.0
    w["sel_rope"] = jnp.asarray(sel, BF16)
    w["w_ukt"] = bf(jnp.transpose(w_uk, (2, 1, 0)).reshape(QK_NOPE * N_HEADS, kv_lora))
    w["w_uv_bd"] = bf(_block_diag(jnp.transpose(w_uv, (1, 0, 2))))
    lb_re, lb_im, bb_re, bb_im = disc
    w["a_re"] = lb_re.reshape(1, -1)
    w["a_im"] = lb_im.reshape(1, -1)
    tr = lambda a: jnp.transpose(a, (0, 2, 1))
    gp = bb_re.shape[0] // SSM_SPLIT
    part = lambda a, p: _block_diag(tr(a[p * gp:(p + 1) * gp]))
    w["w_bu"] = bf(jnp.stack([jnp.concatenate([part(bb_re, p), part(bb_im, p)], axis=1) for p in range(SSM_SPLIT)]))
    w["w_y"] = bf(jnp.stack([jnp.concatenate([part(ssm_c_re, p), -part(ssm_c_im, p)], axis=0)
                             for p in range(SSM_SPLIT)]))
    w["ssm_d"] = ssm_d[None, :]
    w["w_glu"] = bf(w_glu)
    w["w_a"], w["w_b"], w["w_out"] = bf(w_branch_a), bf(w_branch_b), bf(w_out)
    w["g_ffn"], w["g_ple"] = norm_ffn_g[None, :], norm_ple_g[None, :]
    w["w_gate"], w["w_up"], w["w_down"] = bf(w_gate), bf(w_up), bf(w_down)
    w["w_ple_gate"], w["w_ple"] = bf(w_ple_gate), bf(w_ple)
    return w


def _rope_tables(pos):
    half = QK_ROPE // 2
    inv = ROPE_THETA ** (-jnp.arange(half, dtype=F32) / half)
    ang = pos.astype(F32)[:, None] * inv[None, :]
    cos, sin = jnp.cos(ang), jnp.sin(ang)
    n = pos.shape[0]
    ones, zeros = jnp.ones((n, QK_NOPE), F32), jnp.zeros((n, QK_NOPE), F32)
    pad = jnp.zeros((n, HEAD_BLOCK - QK_HEAD), F32)
    return (jnp.concatenate([ones, cos, cos, pad], axis=1), jnp.concatenate([zeros, sin, sin, pad], axis=1))


def _pick(n, prefs):
    for t in prefs:
        if n % t == 0:
            return t
    return n


def kernel(x_prompt, x_sample, p_prompt, p_sample, cache_ckv, cache_kpe, state_ssm_re, state_ssm_im, page_table, norm_attn_g, w_in, q_norm_g, w_uq, kv_norm_g, w_uk, w_uv, q_norm_nope_g, q_norm_rope_g, k_norm_nope_g, k_norm_rope_g, ssm_lam_re, ssm_lam_im, ssm_log_dt, ssm_b_re, ssm_b_im, ssm_c_re, ssm_c_im, ssm_d, w_glu, w_branch_a, w_branch_b, w_out, norm_ffn_g, w_gate, w_up, w_down, norm_ple_g, w_ple_gate, w_ple):
    depth = w_in.shape[0]
    batch, seq, d_model = x_prompt.shape
    n_seq, n_new, _ = x_sample.shape
    past = page_table.shape[1] * PAGE_SIZE
    groups, n_state = ssm_lam_re.shape[1:]
    ns = groups * n_state

    cos_p, sin_p = _rope_tables(jnp.arange(seq))
    cos_p, sin_p = jnp.tile(cos_p, (batch, 1)), jnp.tile(sin_p, (batch, 1))
    cos_s, sin_s = _rope_tables(past + jnp.arange(n_new))
    cos_s, sin_s = jnp.tile(cos_s, (n_seq, 1)), jnp.tile(sin_s, (n_seq, 1))

    n_seg = SUBLANES
    assert seq % n_seg == 0
    seg_len = seq // n_seg
    chains_p = batch * n_seg
    steps_p = _pick(seg_len, (16, 8, 4, 2))
    steps_s = _pick(n_new, (4, 2))
    tm_p = _pick(batch * seq, (512, 256, 128))
    tm_in = _pick(batch * seq, (1024, 512, 256))
    tm_s = _pick(n_seq * n_new, (512, 256, 128))
    assert seq % (2 * KV_BLOCK) == 0
    chunk_pages = _pick(page_table.shape[1] // 2, (32, 16, 8))

    xp = x_prompt.reshape(batch * seq, d_model)
    xs = x_sample.reshape(n_seq * n_new, d_model)
    outs = {k: [] for k in ("ckv_p", "kpe_p", "hre_p", "him_p", "ckv_s", "kpe_s", "hre_s", "him_s")}
    for i in range(depth):
        disc = _discretize(ssm_lam_re[i], ssm_lam_im[i], ssm_log_dt[i], ssm_b_re[i], ssm_b_im[i])
        w = _prepare(norm_attn_g[i], w_in[i], q_norm_g[i], w_uq[i], kv_norm_g[i], w_uk[i], w_uv[i],
                     q_norm_nope_g[i], q_norm_rope_g[i], k_norm_nope_g[i], k_norm_rope_g[i], disc,
                     ssm_c_re[i], ssm_c_im[i], ssm_d[i], w_glu[i], w_branch_a[i], w_branch_b[i], w_out[i],
                     norm_ffn_g[i], w_gate[i], w_up[i], w_down[i], norm_ple_g[i], w_ple_gate[i], w_ple[i])
        ssm_w = w["w_u"].shape[1]

        u, q, ckv, kpe, sga, sgb, k, vt = _in_proj(xp, cos_p, sin_p, w, tm_in, with_kv=True)
        u_rows = u.reshape(batch, n_seg, seg_len, ssm_w).transpose(2, 0, 1, 3).reshape(batch * seq, ssm_w)
        zero_state = jnp.zeros((chains_p, 2 * ns), F32)
        h_end = _ssm(u_rows, zero_state, w, chains_p, steps_p, emit_y=False)
        h_start = _ssm_combine(_state_from_parts(h_end), w, seg_len, n_seg)
        a_rows, h_fin = _ssm(u_rows, _state_to_parts(h_start), w, chains_p, steps_p, emit_y=True)
        h_fin = _state_from_parts(h_fin)
        a_out = a_rows.reshape(seg_len, batch, n_seg, ssm_w).transpose(1, 2, 0, 3).reshape(batch * seq, ssm_w)
        b_out = _flash(q, k, vt, batch, seq)
        xp = _out_proj(xp, a_out, b_out, sga, sgb, p_prompt[i].reshape(batch * seq, -1), w, tm_p, latent_b=False)
        h_fin = h_fin.reshape(batch, n_seg, 2 * ns)[:, -1]
        outs["ckv_p"].append(ckv.reshape(batch, seq, -1))
        outs["kpe_p"].append(kpe.reshape(batch, seq, -1))
        outs["hre_p"].append(h_fin[:, :ns].reshape(batch, groups, n_state))
        outs["him_p"].append(h_fin[:, ns:].reshape(batch, groups, n_state))

        u, q, ckv, kpe, sga, sgb = _in_proj(xs, cos_s, sin_s, w, tm_s, with_kv=False)
        u_rows = u.reshape(n_seq, n_new, ssm_w).transpose(1, 0, 2).reshape(n_seq * n_new, ssm_w)
        h0 = jnp.concatenate([state_ssm_re[i].reshape(n_seq, ns), state_ssm_im[i].reshape(n_seq, ns)], axis=1)
        a_rows, h_fin = _ssm(u_rows, _state_to_parts(h0), w, n_seq, steps_s, emit_y=True)
        h_fin = _state_from_parts(h_fin)
        a_out = a_rows.reshape(n_new, n_seq, ssm_w).transpose(1, 0, 2).reshape(n_seq * n_new, ssm_w)
        qabs, qr = _absorb(q, w)
        rows_q = n_new * N_HEADS
        kv_lora = ckv.shape[1]
        pad_new = PAGE_SIZE - n_new
        ckv_new = ckv.reshape(n_seq, n_new, kv_lora)
        kpet_new = jnp.pad(kpe.reshape(n_seq, n_new, QK_ROPE).transpose(0, 2, 1), ((0, 0), (0, 0), (0, pad_new)))
        o_lat = _paged_attn(page_table, qabs.reshape(n_seq, rows_q, kv_lora), qr.reshape(n_seq, rows_q, HEAD_BLOCK),
                            ckv_new, kpet_new, w["w_ukt"], cache_ckv, jnp.swapaxes(cache_kpe, 2, 3), i, chunk_pages)
        b_lat = o_lat.reshape(n_seq * n_new, N_HEADS * kv_lora)
        xs = _out_proj(xs, a_out, b_lat, sga, sgb, p_sample[i].reshape(n_seq * n_new, -1), w, tm_s, latent_b=True)
        outs["ckv_s"].append(ckv.reshape(n_seq, n_new, -1))
        outs["kpe_s"].append(kpe.reshape(n_seq, n_new, -1))
        outs["hre_s"].append(h_fin[:, :ns].reshape(n_seq, groups, n_state))
        outs["him_s"].append(h_fin[:, ns:].reshape(n_seq, groups, n_state))

    st = lambda key: jnp.stack(outs[key])
    return (xp.reshape(batch, seq, d_model), xs.reshape(n_seq, n_new, d_model),
            st("ckv_p"), st("kpe_p"), st("hre_p"), st("him_p"),
            st("ckv_s"), st("kpe_s"), st("hre_s"), st("him_s"))
```

```python
import functools
import math

import numpy as np
import jax
import jax.numpy as jnp
from jax import lax
from jax.experimental import pallas as pl
from jax.experimental.pallas import tpu as pltpu

N_HEADS = 8
QK_NOPE = 64
QK_ROPE = 32
QK_HEAD = QK_NOPE + QK_ROPE
V_HEAD = 64
VT_ROWS = 80
HEAD_BLOCK = 128
ROPE_THETA = 10000.0
ATTN_SCALE = QK_HEAD ** -0.5
LOG2E = math.log2(math.e)
SSM_GROUP = 16
SSM_STATE = 64
PAGE_SIZE = 128
KV_BLOCK = 256
FLASH_HEADS = 2
PART_PAGES = 16
SSM_SPLIT = 2
EPS = 1e-6

LANES = 128
SUBLANES = 8
VMEM_LIMIT_BYTES = 56 * 1024 * 1024

BF16 = jnp.bfloat16
F32 = jnp.float32
NEG_INF = float("-inf")


def _dot(a, b):
    return jnp.dot(a, b, preferred_element_type=F32)


def _dot_nt(a, b):
    return lax.dot_general(a, b, (((1,), (1,)), ((), ())), preferred_element_type=F32)


def _const_spec(shape):
    nd = len(shape)
    return pl.BlockSpec(shape, lambda *_: (0,) * nd, pipeline_mode=pl.Buffered(1))


def _params(*sem):
    return pltpu.CompilerParams(dimension_semantics=sem, vmem_limit_bytes=VMEM_LIMIT_BYTES)


def _discretize_kernel(lr_ref, li_ref, ldt_ref, br_ref, bi_ref, lbr_ref, lbi_ref, bbr_ref, bbi_ref):
    lr, li = lr_ref[...], li_ref[...]
    dt = jnp.exp(ldt_ref[...])
    mag = jnp.exp(lr * dt)
    lb_re, lb_im = mag * jnp.cos(li * dt), mag * jnp.sin(li * dt)
    num_re, num_im = lb_re - 1.0, lb_im
    den = lr * lr + li * li
    f_re = (num_re * lr + num_im * li) / den
    f_im = (num_im * lr - num_re * li) / den
    br, bi = br_ref[...], bi_ref[...]
    lbr_ref[...] = lb_re
    lbi_ref[...] = lb_im
    bbr_ref[...] = f_re * br - f_im * bi
    bbi_ref[...] = f_re * bi + f_im * br


def _discretize(lam_re, lam_im, log_dt, b_re, b_im):
    g, n, c = b_re.shape
    rep = lambda a: jnp.repeat(a, c, axis=-1)
    args = (rep(lam_re), rep(lam_im), jnp.broadcast_to(log_dt[:, None], (g, n * c)),
            b_re.reshape(g, n * c), b_im.reshape(g, n * c))
    sds = jax.ShapeDtypeStruct((g, n * c), F32)
    lbr, lbi, bbr, bbi = pl.pallas_call(_discretize_kernel, out_shape=(sds,) * 4, name="discretize")(*args)
    lb_re = lbr.reshape(g, n, c)[..., 0]
    lb_im = lbi.reshape(g, n, c)[..., 0]
    return lb_re, lb_im, bbr.reshape(g, n, c), bbi.reshape(g, n, c)


def _in_proj_kernel(x_ref, cos_ref, sin_ref, gattn_ref, wu_ref, wq_ref, wkv_ref, wkr_ref, wga_ref, wgb_ref,
                    gq_ref, wuq_ref, gkv_ref, wuk_ref, wuv_ref, gqh_ref, gkh_ref,
                    u_ref, q_ref, ckv_ref, kpe_ref, sga_ref, sgb_ref, *kv_refs):
    x = x_ref[...]
    h = x * lax.rsqrt(jnp.mean(x * x, axis=-1, keepdims=True) + EPS) * gattn_ref[...]
    hb = h.astype(BF16)
    cos, sin = cos_ref[...], sin_ref[...]
    cos8 = jnp.concatenate([cos] * N_HEADS, axis=1)
    sin8 = jnp.concatenate([sin] * N_HEADS, axis=1)

    u_ref[...] = _dot(hb, wu_ref[...])
    sga_ref[...] = jax.nn.sigmoid(_dot(hb, wga_ref[...])).astype(sga_ref.dtype)
    sgb_ref[...] = jax.nn.sigmoid(_dot(hb, wgb_ref[...])).astype(sgb_ref.dtype)

    def head_norm(t, gain):
        blocks = []
        for hd in range(N_HEADS):
            blk = t[:, hd * HEAD_BLOCK:(hd + 1) * HEAD_BLOCK]
            ssq = jnp.sum(blk * blk, axis=-1, keepdims=True)
            blocks.append(blk * lax.rsqrt(ssq * (1.0 / QK_HEAD) + EPS))
        return jnp.concatenate(blocks, axis=1) * gain

    q_lat = _dot(hb, wq_ref[...])
    qc = q_lat * lax.rsqrt(jnp.mean(q_lat * q_lat, axis=-1, keepdims=True) + EPS) * gq_ref[...]
    qq = _dot(qc.astype(BF16), wuq_ref[...])
    hw = N_HEADS * HEAD_BLOCK
    q_pre = qq[:, :hw] * cos8 + qq[:, hw:] * sin8
    q_ref[...] = (head_norm(q_pre, gqh_ref[...]) * (ATTN_SCALE * LOG2E)).astype(q_ref.dtype)

    kv_lat = _dot(hb, wkv_ref[...])
    ckv = kv_lat * lax.rsqrt(jnp.mean(kv_lat * kv_lat, axis=-1, keepdims=True) + EPS) * gkv_ref[...]
    ckv_ref[...] = ckv
    kr = _dot(hb, wkr_ref[...])
    kpe_wide = kr[:, :HEAD_BLOCK] * cos + kr[:, HEAD_BLOCK:] * sin
    kpe_ref[...] = kpe_wide[:, QK_NOPE:QK_NOPE + QK_ROPE]

    if not kv_refs:
        return
    k_ref, vt_ref = kv_refs
    cb = ckv.astype(BF16)
    k_pre = _dot(cb, wuk_ref[...]) + jnp.concatenate([kpe_wide] * N_HEADS, axis=1)
    k_ref[...] = head_norm(k_pre, gkh_ref[...]).astype(k_ref.dtype)
    vt = _dot_nt(wuv_ref[...], cb)
    ones_row = lax.broadcasted_iota(jnp.int32, vt.shape, 0) % VT_ROWS == V_HEAD
    vt = jnp.where(ones_row, 1.0, vt).astype(vt_ref.dtype)
    for c in range(vt_ref.shape[0]):
        vt_ref[c] = vt[:, c * KV_BLOCK:(c + 1) * KV_BLOCK]


def _in_proj(x, cos_t, sin_t, w, tm, with_kv):
    n, d = x.shape
    assert n % tm == 0
    row = lambda width: pl.BlockSpec((tm, width), lambda i: (i, 0))
    consts = [w[k] for k in ("g_attn", "w_u", "w_q", "w_kv", "w_kr", "w_ga", "w_gb", "g_q", "w_uq", "g_kv",
                             "w_uk", "w_uv_t", "g_qh", "g_kh")]
    hw = N_HEADS * HEAD_BLOCK
    out_shape = (jax.ShapeDtypeStruct((n, w["w_u"].shape[1]), F32),
                 jax.ShapeDtypeStruct((n, hw), BF16),
                 jax.ShapeDtypeStruct((n, w["w_kv"].shape[1]), F32),
                 jax.ShapeDtypeStruct((n, QK_ROPE), F32),
                 jax.ShapeDtypeStruct((n, d), BF16),
                 jax.ShapeDtypeStruct((n, d), BF16))
    if with_kv:
        assert tm % KV_BLOCK == 0
        out_shape += (jax.ShapeDtypeStruct((n, hw), BF16),
                      jax.ShapeDtypeStruct((n // KV_BLOCK, N_HEADS * VT_ROWS, KV_BLOCK), BF16))
    return pl.pallas_call(
        _in_proj_kernel,
        grid=(n // tm,),
        in_specs=[row(d), row(LANES), row(LANES)] + [_const_spec(c.shape) for c in consts],
        out_specs=tuple(pl.BlockSpec((tm // KV_BLOCK,) + s.shape[1:], lambda i: (i, 0, 0)) if len(s.shape) == 3
                        else row(s.shape[1]) for s in out_shape),
        out_shape=out_shape,
        compiler_params=_params("parallel"),
        name="in_proj",
    )(x, cos_t, sin_t, *consts)


def _ssm_kernel(u_ref, h0_ref, ar_ref, ai_ref, wbu_ref, wy_ref, d_ref, wglu_ref,
                *rest, chains, steps, emit_y):
    if emit_y:
        a_ref, ht_ref, bu_ref, hc_ref = rest
    else:
        ht_ref, bu_ref, hc_ref = rest
    i = pl.program_id(0)
    nh = ar_ref.shape[1] // SSM_SPLIT
    cw = u_ref.shape[1] // SSM_SPLIT

    @pl.when(i == 0)
    def _():
        hc_ref[...] = h0_ref[...]

    u = u_ref[...]
    ub = u.astype(BF16)
    for p in range(SSM_SPLIT):
        bu_ref[:, p * 2 * nh:(p + 1) * 2 * nh] = _dot(ub[:, p * cw:(p + 1) * cw], wbu_ref[p])

    def advance(prev_ref, prev_rows, rows):
        for p in range(SSM_SPLIT):
            re, im = slice(p * 2 * nh, p * 2 * nh + nh), slice(p * 2 * nh + nh, (p + 1) * 2 * nh)
            ar, ai = ar_ref[:, p * nh:(p + 1) * nh], ai_ref[:, p * nh:(p + 1) * nh]
            hr, hi = prev_ref[prev_rows, re], prev_ref[prev_rows, im]
            nr = ar * hr - ai * hi + bu_ref[rows, re]
            ni = ar * hi + ai * hr + bu_ref[rows, im]
            bu_ref[rows, re] = nr
            bu_ref[rows, im] = ni

    row_block = lambda j: pl.ds(pl.multiple_of(j * chains, SUBLANES), chains)
    advance(hc_ref, slice(None), row_block(0))

    def step(j, carry):
        advance(bu_ref, row_block(j - 1), row_block(j))
        return carry

    lax.fori_loop(1, steps, step, 0)
    last = bu_ref[(steps - 1) * chains:, :]
    hc_ref[...] = last
    ht_ref[...] = last
    if emit_y:
        hb = bu_ref[...].astype(BF16)
        y = jnp.concatenate([_dot(hb[:, p * 2 * nh:(p + 1) * 2 * nh], wy_ref[p]) for p in range(SSM_SPLIT)], axis=1)
        z = jax.nn.gelu(y + d_ref[...] * u)
        a_ref[...] = (z * jax.nn.sigmoid(_dot(z.astype(BF16), wglu_ref[...]))).astype(a_ref.dtype)


def _state_to_parts(h):
    c, n2 = h.shape
    return h.reshape(c, 2, SSM_SPLIT, n2 // (2 * SSM_SPLIT)).transpose(0, 2, 1, 3).reshape(c, n2)


def _state_from_parts(h):
    c, n2 = h.shape
    return h.reshape(c, SSM_SPLIT, 2, n2 // (2 * SSM_SPLIT)).transpose(0, 2, 1, 3).reshape(c, n2)


def _ssm(u_rows, h0, w, chains, steps, emit_y):
    n, width = u_rows.shape
    rows = chains * steps
    assert n % rows == 0
    ns2 = h0.shape[1]
    consts = [w[k] for k in ("a_re", "a_im", "w_bu", "w_y", "ssm_d", "w_glu")]
    ht_sds = jax.ShapeDtypeStruct((chains, ns2), F32)
    ht_spec = pl.BlockSpec((chains, ns2), lambda i: (0, 0))
    if emit_y:
        out_shape = (jax.ShapeDtypeStruct((n, width), BF16), ht_sds)
        out_specs = (pl.BlockSpec((rows, width), lambda i: (i, 0)), ht_spec)
    else:
        out_shape, out_specs = ht_sds, ht_spec
    return pl.pallas_call(
        functools.partial(_ssm_kernel, chains=chains, steps=steps, emit_y=emit_y),
        grid=(n // rows,),
        in_specs=[pl.BlockSpec((rows, width), lambda i: (i, 0)), _const_spec(h0.shape)]
        + [_const_spec(c.shape) for c in consts],
        out_specs=out_specs,
        out_shape=out_shape,
        scratch_shapes=[pltpu.VMEM((rows, ns2), F32), pltpu.VMEM((chains, ns2), F32)],
        compiler_params=_params("arbitrary"),
        name="ssm_scan" if emit_y else "ssm_end_state",
    )(u_rows, h0, *consts)


def _ssm_combine_kernel(hend_ref, ar_ref, ai_ref, h0_ref, *, seg_len, n_seg):
    ns = ar_ref.shape[1]
    pr, pi = ar_ref[...], ai_ref[...]
    qr, qi = None, None
    e = seg_len
    while e:
        if e & 1:
            qr, qi = (pr, pi) if qr is None else (qr * pr - qi * pi, qr * pi + qi * pr)
        e >>= 1
        if e:
            pr, pi = pr * pr - pi * pi, 2.0 * pr * pi
    er, ei = hend_ref[:, :ns], hend_ref[:, ns:]
    seg = lax.broadcasted_iota(jnp.int32, er.shape, 0) % n_seg
    hr, hi = jnp.zeros_like(er), jnp.zeros_like(ei)
    for s in range(1, n_seg):
        xr = qr * hr - qi * hi + er
        xi = qr * hi + qi * hr + ei
        hr = jnp.where(seg == s, pltpu.roll(xr, 1, 0), hr)
        hi = jnp.where(seg == s, pltpu.roll(xi, 1, 0), hi)
    h0_ref[:, :ns] = hr
    h0_ref[:, ns:] = hi


def _ssm_combine(hend, w, seg_len, n_seg):
    return pl.pallas_call(
        functools.partial(_ssm_combine_kernel, seg_len=seg_len, n_seg=n_seg),
        out_shape=jax.ShapeDtypeStruct(hend.shape, F32),
        name="ssm_combine",
    )(hend, w["a_re"], w["a_im"])


def _flash_kernel(q_ref, k_ref, vt_ref, o_ref, s_buf, p_buf, acc_ref, m_ref, alpha_ref):
    tq = 2 * KV_BLOCK
    heads = range(FLASH_HEADS)

    def scores(q, t, slot):
        rows = pl.ds(pl.multiple_of(t * KV_BLOCK, KV_BLOCK), KV_BLOCK)
        for hh in heads:
            s_buf[slot, hh] = _dot_nt(k_ref[rows, hh * HEAD_BLOCK:(hh + 1) * HEAD_BLOCK], q[hh])

    def softmax(slot, hh, first_visible):
        s = s_buf[slot, hh]
        if first_visible is not None:
            ki = lax.broadcasted_iota(jnp.int32, s.shape, 0)
            qi = lax.broadcasted_iota(jnp.int32, s.shape, 1)
            s = jnp.where(ki + first_visible <= qi, s, NEG_INF)
        m = m_ref[hh]
        m_new = jnp.maximum(m, jnp.max(s, axis=0, keepdims=True))
        m_ref[hh] = m_new
        p_buf[slot, hh] = jnp.exp2(s - m_new).astype(BF16)
        alpha_ref[hh] = jnp.exp2(m - m_new)

    def values(t, slot, hh):
        acc_ref[hh] = (alpha_ref[hh] * acc_ref[hh]
                       + _dot(vt_ref[t, hh * VT_ROWS:(hh + 1) * VT_ROWS, :], p_buf[slot, hh]))

    def step(q, t, slot, first_visible=None):
        for hh in heads:
            values(jnp.maximum(t - 1, 0), 1 - slot, hh)
        scores(q, t + 1, 1 - slot)
        for hh in heads:
            softmax(slot, hh, first_visible)

    def load_q(i):
        rows = pl.ds(pl.multiple_of(i * tq, tq), tq)
        return rows, [q_ref[rows, hh * HEAD_BLOCK:(hh + 1) * HEAD_BLOCK] for hh in heads]

    n_q = q_ref.shape[0] // tq
    scores(load_q(0)[1], 0, 0)

    def query_block(i, carry):
        q_rows, q = load_q(i)
        p_buf[1] = jnp.zeros(p_buf.shape[1:], BF16)
        alpha_ref[...] = jnp.ones(alpha_ref.shape, F32)
        m_ref[...] = jnp.full(m_ref.shape, NEG_INF, F32)
        acc_ref[...] = jnp.zeros(acc_ref.shape, F32)

        def two_steps(u, c):
            step(q, 2 * u, 0)
            step(q, 2 * u + 1, 1)
            return c

        lax.fori_loop(0, i, two_steps, 0)
        step(q, 2 * i, 0, first_visible=0)
        for hh in heads:
            values(2 * i, 0, hh)
        scores(load_q(jnp.minimum(i + 1, n_q - 1))[1], 0, 0)
        for hh in heads:
            softmax(1, hh, KV_BLOCK)
        for hh in heads:
            values(2 * i + 1, 1, hh)
        out = [acc_ref[hh, :V_HEAD] / acc_ref[hh, V_HEAD:V_HEAD + 1] for hh in heads]
        o_ref[q_rows, :] = jnp.concatenate(out, axis=0).T.astype(o_ref.dtype)
        return carry

    lax.fori_loop(0, n_q, query_block, 0)


def _flash(q, k, vt, batch, seq):
    tq = 2 * KV_BLOCK
    hs = FLASH_HEADS
    return pl.pallas_call(
        _flash_kernel,
        grid=(batch, N_HEADS // hs),
        in_specs=[pl.BlockSpec((seq, hs * HEAD_BLOCK), lambda b, p: (b, p)),
                  pl.BlockSpec((seq, hs * HEAD_BLOCK), lambda b, p: (b, p)),
                  pl.BlockSpec((seq // KV_BLOCK, hs * VT_ROWS, KV_BLOCK), lambda b, p: (b, p, 0))],
        out_specs=pl.BlockSpec((seq, hs * V_HEAD), lambda b, p: (b, p)),
        out_shape=jax.ShapeDtypeStruct((batch * seq, N_HEADS * V_HEAD), BF16),
        scratch_shapes=[pltpu.VMEM((2, hs, KV_BLOCK, tq), F32), pltpu.VMEM((2, hs, KV_BLOCK, tq), BF16),
                        pltpu.VMEM((hs, VT_ROWS, tq), F32), pltpu.VMEM((hs, 1, tq), F32),
                        pltpu.VMEM((hs, 1, tq), F32)],
        compiler_params=_params("parallel", "parallel"),
        name="flash",
    )(q, k, vt)


def _absorb_kernel(q_ref, gk_ref, wabs_ref, sel_ref, qabs_ref, qr_ref):
    qg = (q_ref[...].astype(F32) * gk_ref[...]).astype(BF16)
    qabs_ref[...] = _dot(qg, wabs_ref[...]).astype(qabs_ref.dtype)
    qr_ref[...] = _dot(qg, sel_ref[...]).astype(qr_ref.dtype)


def _absorb(q, w):
    n = q.shape[0]
    return pl.pallas_call(
        _absorb_kernel,
        out_shape=(jax.ShapeDtypeStruct((n, w["w_abs"].shape[1]), BF16),
                   jax.ShapeDtypeStruct((n, w["sel_rope"].shape[1]), BF16)),
        compiler_params=_params(),
        name="absorb",
    )(q, w["g_kh"], w["w_abs"], w["sel_rope"])


def _paged_attn_kernel(pt_ref, qabs_ref, qr_ref, qabs_nx_ref, qr_nx_ref, ckvn_ref, kpetn_ref, wukt_ref,
                       ckv_hbm, kpet_hbm, o_ref, ckv_buf, kpe_buf, cb_buf, sc_buf, lhs_ref, m_ref, l_ref, acc_ref,
                       sem_c, sem_k, *, layer, n_chunks, chunk_pages, n_new):
    b = pl.program_id(0)
    total = pl.num_programs(0) * n_chunks
    n_k = wukt_ref.shape[0]
    rows_q = qabs_ref.shape[1]
    tk = chunk_pages * PAGE_SIZE
    part_pages = min(PART_PAGES, chunk_pages)
    parts = chunk_pages // part_pages
    qr = (qr_ref[0][:, :QK_ROPE], qr_nx_ref[0][:, :QK_ROPE])

    def start_chunk(g, slt):
        g = jnp.minimum(g, total - 1)
        for p in range(chunk_pages):
            page = pt_ref[g * chunk_pages + p]
            dst = pl.ds(p * PAGE_SIZE, PAGE_SIZE)
            pltpu.make_async_copy(ckv_hbm.at[layer, page], ckv_buf.at[slt, dst, :], sem_c.at[slt]).start()
            pltpu.make_async_copy(kpet_hbm.at[layer, page], kpe_buf.at[slt, p], sem_k.at[slt]).start()

    def wait_chunk(slt):
        pltpu.make_async_copy(ckv_buf.at[slt], ckv_buf.at[slt], sem_c.at[slt]).wait()
        pltpu.make_async_copy(kpe_buf.at[slt], kpe_buf.at[slt], sem_k.at[slt]).wait()

    def scaled_scores(which, cb, kpet):
        keys = cb.shape[0]
        both = _dot_nt(lhs_ref[which], cb)
        kt = both[:n_k]
        ssq = jnp.sum((kt * kt).reshape(QK_NOPE, N_HEADS, keys), axis=0)
        ssq = ssq + jnp.sum(kpet * kpet, axis=0, keepdims=True)
        r = lax.rsqrt(ssq * (1.0 / QK_HEAD) + EPS)
        sc = both[n_k:] + _dot(qr[which], kpet.astype(BF16))
        return (sc.reshape(n_new, N_HEADS, keys) * r[None]).reshape(rows_q, keys)

    def stage_a(slt, which, part):
        pages = range(part * part_pages, (part + 1) * part_pages)
        rows = slice(pages[0] * PAGE_SIZE, (pages[-1] + 1) * PAGE_SIZE)
        cb = ckv_buf[slt, rows, :].astype(BF16)
        cb_buf[slt, rows, :] = cb
        kpet = jnp.concatenate([kpe_buf[slt, p] for p in pages], axis=1)
        sc_buf[slt, :, rows] = scaled_scores(which, cb, kpet)

    def stage_b(sc, cb):
        m_old = m_ref[...]
        m_new = jnp.maximum(m_old, jnp.max(sc, axis=1, keepdims=True))
        p = jnp.exp2(sc - m_new)
        alpha = jnp.exp2(m_old - m_new)
        l_ref[...] = alpha * l_ref[...] + jnp.sum(p, axis=1, keepdims=True)
        acc_ref[...] = alpha * acc_ref[...] + _dot(p.astype(BF16), cb)
        m_ref[...] = m_new

    @pl.when(b == 0)
    def _():
        for which in range(2):
            lhs_ref[which, :n_k, :] = wukt_ref[...]

    lhs_ref[0, n_k:, :] = qabs_ref[0]
    lhs_ref[1, n_k:, :] = qabs_nx_ref[0]

    @pl.when(b == 0)
    def _():
        start_chunk(0, 0)
        start_chunk(1, 1)
        wait_chunk(0)
        for part in range(parts):
            stage_a(0, 0, part)

    m_ref[...] = jnp.full(m_ref.shape, NEG_INF, F32)
    l_ref[...] = jnp.zeros(l_ref.shape, F32)
    acc_ref[...] = jnp.zeros(acc_ref.shape, F32)

    for c in range(n_chunks):
        g = b * n_chunks + c
        slot, nxt = c % 2, (c + 1) % 2
        which = 0 if c + 1 < n_chunks else 1

        wait_chunk(nxt)
        start_chunk(g + 2, slot)
        for part in range(parts):
            stage_a(nxt, which, part)
        stage_b(sc_buf[slot], cb_buf[slot])

    @pl.when(b == pl.num_programs(0) - 1)
    def _():
        wait_chunk((n_chunks - 1) % 2)

    tkn = kpetn_ref.shape[2]
    cbn = ckvn_ref[0].astype(BF16)
    cbn = jnp.concatenate([cbn, jnp.zeros((tkn - cbn.shape[0], cbn.shape[1]), BF16)], axis=0)
    sc = scaled_scores(0, cbn, kpetn_ref[0])
    key = lax.broadcasted_iota(jnp.int32, (rows_q, tkn), 1)
    tok = lax.broadcasted_iota(jnp.int32, (rows_q, tkn), 0) // N_HEADS
    stage_b(jnp.where(key <= tok, sc, NEG_INF), cbn)
    o_ref[0] = (acc_ref[...] / l_ref[...]).astype(o_ref.dtype)


def _paged_attn(page_table, qabs, qr, ckv_new, kpet_new, w_ukt, cache_ckv, cache_kpet, layer, chunk_pages):
    n_seq, pages_per_seq = page_table.shape
    assert pages_per_seq % chunk_pages == 0 and chunk_pages % min(PART_PAGES, chunk_pages) == 0
    n_chunks = pages_per_seq // chunk_pages
    assert n_chunks % 2 == 0 and n_seq * n_chunks >= 2
    rows_q, kv_lora = qabs.shape[1:]
    n_new = rows_q // N_HEADS
    tk = chunk_pages * PAGE_SIZE
    seq_block = lambda shape: pl.BlockSpec((1,) + shape, lambda b, pt: (b, 0, 0))
    nxt_block = lambda shape: pl.BlockSpec((1,) + shape, lambda b, pt: (jnp.minimum(b + 1, n_seq - 1), 0, 0))
    grid_spec = pltpu.PrefetchScalarGridSpec(
        num_scalar_prefetch=1,
        grid=(n_seq,),
        in_specs=[seq_block(qabs.shape[1:]), seq_block(qr.shape[1:]),
                  nxt_block(qabs.shape[1:]), nxt_block(qr.shape[1:]),
                  seq_block(ckv_new.shape[1:]), seq_block(kpet_new.shape[1:]),
                  pl.BlockSpec(w_ukt.shape, lambda b, pt: (0, 0)),
                  pl.BlockSpec(memory_space=pl.ANY), pl.BlockSpec(memory_space=pl.ANY)],
        out_specs=seq_block((rows_q, kv_lora)),
        scratch_shapes=[pltpu.VMEM((2, tk, kv_lora), F32), pltpu.VMEM((2, chunk_pages, QK_ROPE, PAGE_SIZE), F32),
                        pltpu.VMEM((2, tk, kv_lora), BF16), pltpu.VMEM((2, rows_q, tk), F32),
                        pltpu.VMEM((2, w_ukt.shape[0] + rows_q, kv_lora), BF16),
                        pltpu.VMEM((rows_q, 1), F32), pltpu.VMEM((rows_q, 1), F32),
                        pltpu.VMEM((rows_q, kv_lora), F32),
                        pltpu.SemaphoreType.DMA((2,)), pltpu.SemaphoreType.DMA((2,))])
    return pl.pallas_call(
        functools.partial(_paged_attn_kernel, layer=layer, n_chunks=n_chunks, chunk_pages=chunk_pages, n_new=n_new),
        grid_spec=grid_spec,
        out_shape=jax.ShapeDtypeStruct((n_seq, rows_q, kv_lora), BF16),
        compiler_params=_params("arbitrary"),
        name="paged_attn",
    )(page_table.reshape(-1), qabs, qr, qabs, qr, ckv_new, kpet_new, w_ukt, cache_ckv, cache_kpet)


def _out_proj_kernel(x_ref, a_ref, b_ref, sga_ref, sgb_ref, p_ref, wuvb_ref, wa_ref, wb_ref, wo_ref, gffn_ref,
                     wg_ref, wup_ref, wdn_ref, gple_ref, wpg_ref, wple_ref, y_ref, *, latent_b):
    b = b_ref[...]
    if latent_b:
        b = _dot(b, wuvb_ref[...]).astype(BF16)
    merged = (sga_ref[...].astype(F32) * _dot(a_ref[...], wa_ref[...])
              + sgb_ref[...].astype(F32) * _dot(b, wb_ref[...]))
    x = x_ref[...] + _dot(merged.astype(BF16), wo_ref[...])

    def norm(t, g_ref):
        return (t * lax.rsqrt(jnp.mean(t * t, axis=-1, keepdims=True) + EPS) * g_ref[...]).astype(BF16)

    hf = norm(x, gffn_ref)
    act = jax.nn.silu(_dot(hf, wg_ref[...])) * _dot(hf, wup_ref[...])
    x = x + _dot(act.astype(BF16), wdn_ref[...])
    gate = jax.nn.sigmoid(_dot(norm(x, gple_ref), wpg_ref[...]))
    y_ref[...] = x + gate * _dot(p_ref[...].astype(BF16), wple_ref[...])


def _out_proj(x, a, b, sga, sgb, p, w, tm, latent_b):
    n, d = x.shape
    assert n % tm == 0
    row = lambda width: pl.BlockSpec((tm, width), lambda i: (i, 0))
    consts = [w[k] for k in ("w_uv_bd", "w_a", "w_b", "w_out", "g_ffn", "w_gate", "w_up", "w_down", "g_ple",
                             "w_ple_gate", "w_ple")]
    return pl.pallas_call(
        functools.partial(_out_proj_kernel, latent_b=latent_b),
        grid=(n // tm,),
        in_specs=[row(d), row(a.shape[1]), row(b.shape[1]), row(d), row(d), row(p.shape[1])]
        + [_const_spec(c.shape) for c in consts],
        out_specs=row(d),
        out_shape=jax.ShapeDtypeStruct((n, d), F32),
        compiler_params=_params("parallel"),
        name="out_proj_latent" if latent_b else "out_proj",
    )(x, a, b, sga, sgb, p, *consts)


def _head_blocked(w_nope, w_rope):
    k = w_nope.shape[0] if w_nope is not None else w_rope.shape[0]
    nope = w_nope if w_nope is not None else jnp.zeros((k, N_HEADS, QK_NOPE), F32)
    rope = w_rope if w_rope is not None else jnp.zeros((k, N_HEADS, QK_ROPE), F32)
    pad = jnp.zeros((k, N_HEADS, HEAD_BLOCK - QK_HEAD), F32)
    return jnp.concatenate([nope, rope, pad], axis=-1).reshape(k, N_HEADS * HEAD_BLOCK)


def _rot_cols(w):
    half = QK_ROPE // 2
    return jnp.concatenate([-w[..., half:], w[..., :half]], axis=-1)


def _head_gain(g_nope, g_rope_half):
    g = jnp.concatenate([g_nope, g_rope_half, g_rope_half, jnp.zeros((HEAD_BLOCK - QK_HEAD,), F32)])
    return jnp.tile(g, N_HEADS)[None, :]


def _block_diag(blocks):
    g, r, c = blocks.shape
    eye = jnp.eye(g, dtype=blocks.dtype)
    return jnp.einsum("grc,gh->grhc", blocks, eye).reshape(g * r, g * c)


def _prepare(norm_attn_g, w_in, q_norm_g, w_uq, kv_norm_g, w_uk, w_uv, q_norm_nope_g, q_norm_rope_g,
             k_norm_nope_g, k_norm_rope_g, disc, ssm_c_re, ssm_c_im, ssm_d, w_glu, w_branch_a, w_branch_b,
             w_out, norm_ffn_g, w_gate, w_up, w_down, norm_ple_g, w_ple_gate, w_ple):
    d_model = w_in.shape[0]
    ssm_w = ssm_d.shape[0]
    q_lora, kv_lora = w_uq.shape[0], w_uk.shape[0]
    o = np.cumsum([0, ssm_w, q_lora, kv_lora, QK_ROPE, d_model, d_model])
    bf = lambda a: a.astype(BF16)
    w = {}
    w["g_attn"] = norm_attn_g[None, :]
    w["w_u"] = bf(w_in[:, o[0]:o[1]])
    w["w_q"] = bf(w_in[:, o[1]:o[2]])
    w["w_kv"] = bf(w_in[:, o[2]:o[3]])
    w_kr = w_in[:, o[3]:o[4]]
    place = lambda r: jnp.pad(r, ((0, 0), (QK_NOPE, HEAD_BLOCK - QK_HEAD)))
    w["w_kr"] = bf(jnp.concatenate([place(w_kr), place(_rot_cols(w_kr))], axis=1))
    w["w_ga"] = bf(w_in[:, o[4]:o[5]])
    w["w_gb"] = bf(w_in[:, o[5]:o[6]])
    w["g_q"] = q_norm_g[None, :]
    uq_nope, uq_rope = w_uq[..., :QK_NOPE], w_uq[..., QK_NOPE:]
    w["w_uq"] = bf(jnp.concatenate([_head_blocked(uq_nope, uq_rope), _head_blocked(None, _rot_cols(uq_rope))], axis=1))
    w["g_kv"] = kv_norm_g[None, :]
    w["w_uk"] = bf(_head_blocked(w_uk, None))
    uv_t = jnp.pad(jnp.transpose(w_uv, (1, 2, 0)), ((0, 0), (0, VT_ROWS - V_HEAD), (0, 0)))
    w["w_uv_t"] = bf(uv_t.reshape(N_HEADS * VT_ROWS, kv_lora))
    w["g_qh"] = _head_gain(q_norm_nope_g, q_norm_rope_g)
    w["g_kh"] = _head_gain(k_norm_nope_g, k_norm_rope_g)
    ukt = jnp.transpose(w_uk, (1, 2, 0))
    ukt = jnp.pad(ukt, ((0, 0), (0, HEAD_BLOCK - QK_NOPE), (0, 0)))
    w["w_abs"] = bf(_block_diag(ukt))
    sel = np.zeros((N_HEADS * HEAD_BLOCK, N_HEADS * HEAD_BLOCK), np.float32)
    for h in range(N_HEADS):
        for dd in range(QK_ROPE):
            sel[h * HEAD_BLOCK + QK_NOPE + dd, h * HEAD_BLOCK + dd] = 1.0
    w["sel_rope"] = jnp.asarray(sel, BF16)
    w["w_ukt"] = bf(jnp.transpose(w_uk, (2, 1, 0)).reshape(QK_NOPE * N_HEADS, kv_lora))
    w["w_uv_bd"] = bf(_block_diag(jnp.transpose(w_uv, (1, 0, 2))))
    lb_re, lb_im, bb_re, bb_im = disc
    w["a_re"] = lb_re.reshape(1, -1)
    w["a_im"] = lb_im.reshape(1, -1)
    tr = lambda a: jnp.transpose(a, (0, 2, 1))
    gp = bb_re.shape[0] // SSM_SPLIT
    part = lambda a, p: _block_diag(tr(a[p * gp:(p + 1) * gp]))
    w["w_bu"] = bf(jnp.stack([jnp.concatenate([part(bb_re, p), part(bb_im, p)], axis=1) for p in range(SSM_SPLIT)]))
    w["w_y"] = bf(jnp.stack([jnp.concatenate([part(ssm_c_re, p), -part(ssm_c_im, p)], axis=0)
                             for p in range(SSM_SPLIT)]))
    w["ssm_d"] = ssm_d[None, :]
    w["w_glu"] = bf(w_glu)
    w["w_a"], w["w_b"], w["w_out"] = bf(w_branch_a), bf(w_branch_b), bf(w_out)
    w["g_ffn"], w["g_ple"] = norm_ffn_g[None, :], norm_ple_g[None, :]
    w["w_gate"], w["w_up"], w["w_down"] = bf(w_gate), bf(w_up), bf(w_down)
    w["w_ple_gate"], w["w_ple"] = bf(w_ple_gate), bf(w_ple)
    return w


def _rope_tables(pos):
    half = QK_ROPE // 2
    inv = ROPE_THETA ** (-jnp.arange(half, dtype=F32) / half)
    ang = pos.astype(F32)[:, None] * inv[None, :]
    cos, sin = jnp.cos(ang), jnp.sin(ang)
    n = pos.shape[0]
    ones, zeros = jnp.ones((n, QK_NOPE), F32), jnp.zeros((n, QK_NOPE), F32)
    pad = jnp.zeros((n, HEAD_BLOCK - QK_HEAD), F32)
    return (jnp.concatenate([ones, cos, cos, pad], axis=1), jnp.concatenate([zeros, sin, sin, pad], axis=1))


def _pick(n, prefs):
    for t in prefs:
        if n % t == 0:
            return t
    return n


def kernel(x_prompt, x_sample, p_prompt, p_sample, cache_ckv, cache_kpe, state_ssm_re, state_ssm_im, page_table, norm_attn_g, w_in, q_norm_g, w_uq, kv_norm_g, w_uk, w_uv, q_norm_nope_g, q_norm_rope_g, k_norm_nope_g, k_norm_rope_g, ssm_lam_re, ssm_lam_im, ssm_log_dt, ssm_b_re, ssm_b_im, ssm_c_re, ssm_c_im, ssm_d, w_glu, w_branch_a, w_branch_b, w_out, norm_ffn_g, w_gate, w_up, w_down, norm_ple_g, w_ple_gate, w_ple):
    depth = w_in.shape[0]
    batch, seq, d_model = x_prompt.shape
    n_seq, n_new, _ = x_sample.shape
    past = page_table.shape[1] * PAGE_SIZE
    groups, n_state = ssm_lam_re.shape[1:]
    ns = groups * n_state

    cos_p, sin_p = _rope_tables(jnp.arange(seq))
    cos_p, sin_p = jnp.tile(cos_p, (batch, 1)), jnp.tile(sin_p, (batch, 1))
    cos_s, sin_s = _rope_tables(past + jnp.arange(n_new))
    cos_s, sin_s = jnp.tile(cos_s, (n_seq, 1)), jnp.tile(sin_s, (n_seq, 1))

    n_seg = SUBLANES
    assert seq % n_seg == 0
    seg_len = seq // n_seg
    chains_p = batch * n_seg
    steps_p = _pick(seg_len, (16, 8, 4, 2))
    steps_s = _pick(n_new, (4, 2))
    tm_p = _pick(batch * seq, (512, 256, 128))
    tm_in = _pick(batch * seq, (1024, 512, 256))
    tm_s = _pick(n_seq * n_new, (512, 256, 128))
    assert seq % (2 * KV_BLOCK) == 0
    chunk_pages = _pick(page_table.shape[1] // 2, (32, 16, 8))

    xp = x_prompt.reshape(batch * seq, d_model)
    xs = x_sample.reshape(n_seq * n_new, d_model)
    outs = {k: [] for k in ("ckv_p", "kpe_p", "hre_p", "him_p", "ckv_s", "kpe_s", "hre_s", "him_s")}
    for i in range(depth):
        disc = _discretize(ssm_lam_re[i], ssm_lam_im[i], ssm_log_dt[i], ssm_b_re[i], ssm_b_im[i])
        w = _prepare(norm_attn_g[i], w_in[i], q_norm_g[i], w_uq[i], kv_norm_g[i], w_uk[i], w_uv[i],
                     q_norm_nope_g[i], q_norm_rope_g[i], k_norm_nope_g[i], k_norm_rope_g[i], disc,
                     ssm_c_re[i], ssm_c_im[i], ssm_d[i], w_glu[i], w_branch_a[i], w_branch_b[i], w_out[i],
                     norm_ffn_g[i], w_gate[i], w_up[i], w_down[i], norm_ple_g[i], w_ple_gate[i], w_ple[i])
        ssm_w = w["w_u"].shape[1]

        u, q, ckv, kpe, sga, sgb, k, vt = _in_proj(xp, cos_p, sin_p, w, tm_in, with_kv=True)
        u_rows = u.reshape(batch, n_seg, seg_len, ssm_w).transpose(2, 0, 1, 3).reshape(batch * seq, ssm_w)
        zero_state = jnp.zeros((chains_p, 2 * ns), F32)
        h_end = _ssm(u_rows, zero_state, w, chains_p, steps_p, emit_y=False)
        h_start = _ssm_combine(_state_from_parts(h_end), w, seg_len, n_seg)
        a_rows, h_fin = _ssm(u_rows, _state_to_parts(h_start), w, chains_p, steps_p, emit_y=True)
        h_fin = _state_from_parts(h_fin)
        a_out = a_rows.reshape(seg_len, batch, n_seg, ssm_w).transpose(1, 2, 0, 3).reshape(batch * seq, ssm_w)
        b_out = _flash(q, k, vt, batch, seq)
        xp = _out_proj(xp, a_out, b_out, sga, sgb, p_prompt[i].reshape(batch * seq, -1), w, tm_p, latent_b=False)
        h_fin = h_fin.reshape(batch, n_seg, 2 * ns)[:, -1]
        outs["ckv_p"].append(ckv.reshape(batch, seq, -1))
        outs["kpe_p"].append(kpe.reshape(batch, seq, -1))
        outs["hre_p"].append(h_fin[:, :ns].reshape(batch, groups, n_state))
        outs["him_p"].append(h_fin[:, ns:].reshape(batch, groups, n_state))

        u, q, ckv, kpe, sga, sgb = _in_proj(xs, cos_s, sin_s, w, tm_s, with_kv=False)
        u_rows = u.reshape(n_seq, n_new, ssm_w).transpose(1, 0, 2).reshape(n_seq * n_new, ssm_w)
        h0 = jnp.concatenate([state_ssm_re[i].reshape(n_seq, ns), state_ssm_im[i].reshape(n_seq, ns)], axis=1)
        a_rows, h_fin = _ssm(u_rows, _state_to_parts(h0), w, n_seq, steps_s, emit_y=True)
        h_fin = _state_from_parts(h_fin)
        a_out = a_rows.reshape(n_new, n_seq, ssm_w).transpose(1, 0, 2).reshape(n_seq * n_new, ssm_w)
        qabs, qr = _absorb(q, w)
        rows_q = n_new * N_HEADS
        kv_lora = ckv.shape[1]
        pad_new = PAGE_SIZE - n_new
        ckv_new = ckv.reshape(n_seq, n_new, kv_lora)
        kpet_new = jnp.pad(kpe.reshape(n_seq, n_new, QK_ROPE).transpose(0, 2, 1), ((0, 0), (0, 0), (0, pad_new)))
        o_lat = _paged_attn(page_table, qabs.reshape(n_seq, rows_q, kv_lora), qr.reshape(n_seq, rows_q, HEAD_BLOCK),
                            ckv_new, kpet_new, w["w_ukt"], cache_ckv, jnp.swapaxes(cache_kpe, 2, 3), i, chunk_pages)
        b_lat = o_lat.reshape(n_seq * n_new, N_HEADS * kv_lora)
        xs = _out_proj(xs, a_out, b_lat, sga, sgb, p_sample[i].reshape(n_seq * n_new, -1), w, tm_s, latent_b=True)
        outs["ckv_s"].append(ckv.reshape(n_seq, n_new, -1))
        outs["kpe_s"].append(kpe.reshape(n_seq, n_new, -1))
        outs["hre_s"].append(h_fin[:, :ns].reshape(n_seq, groups, n_state))
        outs["him_s"].append(h_fin[:, ns:].reshape(n_seq, groups, n_state))

    st = lambda key: jnp.stack(outs[key])
    return (xp.reshape(batch, seq, d_model), xs.reshape(n_seq, n_new, d_model),
            st("ckv_p"), st("kpe_p"), st("hre_p"), st("him_p"),
            st("ckv_s"), st("kpe_s"), st("hre_s"), st("him_s"))
```

```python
import functools
import math

import numpy as np
import jax
import jax.numpy as jnp
from jax import lax
from jax.experimental import pallas as pl
from jax.experimental.pallas import tpu as pltpu

N_HEADS = 8
QK_NOPE = 64
QK_ROPE = 32
QK_HEAD = QK_NOPE + QK_ROPE
V_HEAD = 64
VT_ROWS = 80
HEAD_BLOCK = 128
ROPE_THETA = 10000.0
ATTN_SCALE = QK_HEAD ** -0.5
LOG2E = math.log2(math.e)
SSM_GROUP = 16
SSM_STATE = 64
PAGE_SIZE = 128
KV_BLOCK = 256
FFN_PIECES = 2
FLASH_HEADS = 2
PART_PAGES = 16
SSM_SPLIT = 2
EPS = 1e-6

LANES = 128
SUBLANES = 8
VMEM_LIMIT_BYTES = 56 * 1024 * 1024

BF16 = jnp.bfloat16
F32 = jnp.float32
NEG_INF = float("-inf")


def _dot(a, b):
    return jnp.dot(a, b, preferred_element_type=F32)


def _dot_nt(a, b):
    return lax.dot_general(a, b, (((1,), (1,)), ((), ())), preferred_element_type=F32)


def _const_spec(shape):
    nd = len(shape)
    return pl.BlockSpec(shape, lambda *_: (0,) * nd, pipeline_mode=pl.Buffered(1))


def _params(*sem):
    return pltpu.CompilerParams(dimension_semantics=sem, vmem_limit_bytes=VMEM_LIMIT_BYTES)


def _discretize_kernel(lr_ref, li_ref, ldt_ref, br_ref, bi_ref, lbr_ref, lbi_ref, bbr_ref, bbi_ref):
    lr, li = lr_ref[...], li_ref[...]
    dt = jnp.exp(ldt_ref[...])
    mag = jnp.exp(lr * dt)
    lb_re, lb_im = mag * jnp.cos(li * dt), mag * jnp.sin(li * dt)
    num_re, num_im = lb_re - 1.0, lb_im
    den = lr * lr + li * li
    f_re = (num_re * lr + num_im * li) / den
    f_im = (num_im * lr - num_re * li) / den
    br, bi = br_ref[...], bi_ref[...]
    lbr_ref[...] = lb_re
    lbi_ref[...] = lb_im
    bbr_ref[...] = f_re * br - f_im * bi
    bbi_ref[...] = f_re * bi + f_im * br


def _discretize(lam_re, lam_im, log_dt, b_re, b_im):
    g, n, c = b_re.shape
    rep = lambda a: jnp.repeat(a, c, axis=-1)
    args = (rep(lam_re), rep(lam_im), jnp.broadcast_to(log_dt[:, None], (g, n * c)),
            b_re.reshape(g, n * c), b_im.reshape(g, n * c))
    sds = jax.ShapeDtypeStruct((g, n * c), F32)
    lbr, lbi, bbr, bbi = pl.pallas_call(_discretize_kernel, out_shape=(sds,) * 4, name="discretize")(*args)
    lb_re = lbr.reshape(g, n, c)[..., 0]
    lb_im = lbi.reshape(g, n, c)[..., 0]
    return lb_re, lb_im, bbr.reshape(g, n, c), bbi.reshape(g, n, c)


def _in_proj_kernel(x_ref, cos_ref, sin_ref, gattn_ref, wu_ref, wq_ref, wkv_ref, wkr_ref, wga_ref, wgb_ref,
                    gq_ref, wuq_ref, gkv_ref, wuk_ref, wuv_ref, gqh_ref, gkh_ref,
                    u_ref, q_ref, ckv_ref, kpe_ref, sga_ref, sgb_ref, *kv_refs):
    x = x_ref[...]
    h = x * lax.rsqrt(jnp.mean(x * x, axis=-1, keepdims=True) + EPS) * gattn_ref[...]
    hb = h.astype(BF16)
    cos, sin = cos_ref[...], sin_ref[...]
    cos8 = jnp.concatenate([cos] * N_HEADS, axis=1)
    sin8 = jnp.concatenate([sin] * N_HEADS, axis=1)

    u_ref[...] = _dot(hb, wu_ref[...])
    sga_ref[...] = jax.nn.sigmoid(_dot(hb, wga_ref[...])).astype(sga_ref.dtype)
    sgb_ref[...] = jax.nn.sigmoid(_dot(hb, wgb_ref[...])).astype(sgb_ref.dtype)

    def head_norm(t, gain):
        blocks = []
        for hd in range(N_HEADS):
            blk = t[:, hd * HEAD_BLOCK:(hd + 1) * HEAD_BLOCK]
            ssq = jnp.sum(blk * blk, axis=-1, keepdims=True)
            blocks.append(blk * lax.rsqrt(ssq * (1.0 / QK_HEAD) + EPS))
        return jnp.concatenate(blocks, axis=1) * gain

    q_lat = _dot(hb, wq_ref[...])
    qc = q_lat * lax.rsqrt(jnp.mean(q_lat * q_lat, axis=-1, keepdims=True) + EPS) * gq_ref[...]
    qq = _dot(qc.astype(BF16), wuq_ref[...])
    hw = N_HEADS * HEAD_BLOCK
    q_pre = qq[:, :hw] * cos8 + qq[:, hw:] * sin8
    q_ref[...] = (head_norm(q_pre, gqh_ref[...]) * (ATTN_SCALE * LOG2E)).astype(q_ref.dtype)

    kv_lat = _dot(hb, wkv_ref[...])
    ckv = kv_lat * lax.rsqrt(jnp.mean(kv_lat * kv_lat, axis=-1, keepdims=True) + EPS) * gkv_ref[...]
    ckv_ref[...] = ckv
    kr = _dot(hb, wkr_ref[...])
    kpe_wide = kr[:, :HEAD_BLOCK] * cos + kr[:, HEAD_BLOCK:] * sin
    kpe_ref[...] = kpe_wide[:, QK_NOPE:QK_NOPE + QK_ROPE]

    if not kv_refs:
        return
    k_ref, vt_ref = kv_refs
    cb = ckv.astype(BF16)
    k_pre = _dot(cb, wuk_ref[...]) + jnp.concatenate([kpe_wide] * N_HEADS, axis=1)
    k_ref[...] = head_norm(k_pre, gkh_ref[...]).astype(k_ref.dtype)
    vt = _dot_nt(wuv_ref[...], cb)
    ones_row = lax.broadcasted_iota(jnp.int32, vt.shape, 0) % VT_ROWS == V_HEAD
    vt = jnp.where(ones_row, 1.0, vt).astype(vt_ref.dtype)
    for c in range(vt_ref.shape[0]):
        vt_ref[c] = vt[:, c * KV_BLOCK:(c + 1) * KV_BLOCK]


def _in_proj(x, cos_t, sin_t, w, tm, with_kv):
    n, d = x.shape
    assert n % tm == 0
    row = lambda width: pl.BlockSpec((tm, width), lambda i: (i, 0))
    consts = [w[k] for k in ("g_attn", "w_u", "w_q", "w_kv", "w_kr", "w_ga", "w_gb", "g_q", "w_uq", "g_kv",
                             "w_uk", "w_uv_t", "g_qh", "g_kh")]
    hw = N_HEADS * HEAD_BLOCK
    out_shape = (jax.ShapeDtypeStruct((n, w["w_u"].shape[1]), F32),
                 jax.ShapeDtypeStruct((n, hw), BF16),
                 jax.ShapeDtypeStruct((n, w["w_kv"].shape[1]), F32),
                 jax.ShapeDtypeStruct((n, QK_ROPE), F32),
                 jax.ShapeDtypeStruct((n, d), BF16),
                 jax.ShapeDtypeStruct((n, d), BF16))
    if with_kv:
        assert tm % KV_BLOCK == 0
        out_shape += (jax.ShapeDtypeStruct((n, hw), BF16),
                      jax.ShapeDtypeStruct((n // KV_BLOCK, N_HEADS * VT_ROWS, KV_BLOCK), BF16))
    return pl.pallas_call(
        _in_proj_kernel,
        grid=(n // tm,),
        in_specs=[row(d), row(LANES), row(LANES)] + [_const_spec(c.shape) for c in consts],
        out_specs=tuple(pl.BlockSpec((tm // KV_BLOCK,) + s.shape[1:], lambda i: (i, 0, 0)) if len(s.shape) == 3
                        else row(s.shape[1]) for s in out_shape),
        out_shape=out_shape,
        compiler_params=_params("parallel"),
        name="in_proj",
    )(x, cos_t, sin_t, *consts)


def _ssm_kernel(u_ref, h0_ref, ar_ref, ai_ref, wbu_ref, wy_ref, d_ref, wglu_ref,
                *rest, chains, steps, emit_y):
    if emit_y:
        a_ref, ht_ref, bu_ref, hc_ref = rest
    else:
        ht_ref, bu_ref, hc_ref = rest
    i = pl.program_id(0)
    nh = ar_ref.shape[1] // SSM_SPLIT
    cw = u_ref.shape[1] // SSM_SPLIT

    @pl.when(i == 0)
    def _():
        hc_ref[...] = h0_ref[...]

    u = u_ref[...]
    ub = u.astype(BF16)
    for p in range(SSM_SPLIT):
        bu_ref[:, p * 2 * nh:(p + 1) * 2 * nh] = _dot(ub[:, p * cw:(p + 1) * cw], wbu_ref[p])

    def advance(prev_ref, prev_rows, rows):
        for p in range(SSM_SPLIT):
            re, im = slice(p * 2 * nh, p * 2 * nh + nh), slice(p * 2 * nh + nh, (p + 1) * 2 * nh)
            ar, ai = ar_ref[:, p * nh:(p + 1) * nh], ai_ref[:, p * nh:(p + 1) * nh]
            hr, hi = prev_ref[prev_rows, re], prev_ref[prev_rows, im]
            nr = ar * hr - ai * hi + bu_ref[rows, re]
            ni = ar * hi + ai * hr + bu_ref[rows, im]
            bu_ref[rows, re] = nr
            bu_ref[rows, im] = ni

    row_block = lambda j: pl.ds(pl.multiple_of(j * chains, SUBLANES), chains)
    advance(hc_ref, slice(None), row_block(0))

    def step(j, carry):
        advance(bu_ref, row_block(j - 1), row_block(j))
        return carry

    lax.fori_loop(1, steps, step, 0)
    last = bu_ref[(steps - 1) * chains:, :]
    hc_ref[...] = last
    ht_ref[...] = last
    if emit_y:
        hb = bu_ref[...].astype(BF16)
        y = jnp.concatenate([_dot(hb[:, p * 2 * nh:(p + 1) * 2 * nh], wy_ref[p]) for p in range(SSM_SPLIT)], axis=1)
        z = jax.nn.gelu(y + d_ref[...] * u)
        a_ref[...] = (z * jax.nn.sigmoid(_dot(z.astype(BF16), wglu_ref[...]))).astype(a_ref.dtype)


def _state_to_parts(h):
    c, n2 = h.shape
    return h.reshape(c, 2, SSM_SPLIT, n2 // (2 * SSM_SPLIT)).transpose(0, 2, 1, 3).reshape(c, n2)


def _state_from_parts(h):
    c, n2 = h.shape
    return h.reshape(c, SSM_SPLIT, 2, n2 // (2 * SSM_SPLIT)).transpose(0, 2, 1, 3).reshape(c, n2)


def _ssm(u_rows, h0, w, chains, steps, emit_y):
    n, width = u_rows.shape
    rows = chains * steps
    assert n % rows == 0
    ns2 = h0.shape[1]
    consts = [w[k] for k in ("a_re", "a_im", "w_bu", "w_y", "ssm_d", "w_glu")]
    ht_sds = jax.ShapeDtypeStruct((chains, ns2), F32)
    ht_spec = pl.BlockSpec((chains, ns2), lambda i: (0, 0))
    if emit_y:
        out_shape = (jax.ShapeDtypeStruct((n, width), BF16), ht_sds)
        out_specs = (pl.BlockSpec((rows, width), lambda i: (i, 0)), ht_spec)
    else:
        out_shape, out_specs = ht_sds, ht_spec
    return pl.pallas_call(
        functools.partial(_ssm_kernel, chains=chains, steps=steps, emit_y=emit_y),
        grid=(n // rows,),
        in_specs=[pl.BlockSpec((rows, width), lambda i: (i, 0)), _const_spec(h0.shape)]
        + [_const_spec(c.shape) for c in consts],
        out_specs=out_specs,
        out_shape=out_shape,
        scratch_shapes=[pltpu.VMEM((rows, ns2), F32), pltpu.VMEM((chains, ns2), F32)],
        compiler_params=_params("arbitrary"),
        name="ssm_scan" if emit_y else "ssm_end_state",
    )(u_rows, h0, *consts)


def _ssm_combine_kernel(hend_ref, ar_ref, ai_ref, h0_ref, *, seg_len, n_seg):
    ns = ar_ref.shape[1]
    pr, pi = ar_ref[...], ai_ref[...]
    qr, qi = None, None
    e = seg_len
    while e:
        if e & 1:
            qr, qi = (pr, pi) if qr is None else (qr * pr - qi * pi, qr * pi + qi * pr)
        e >>= 1
        if e:
            pr, pi = pr * pr - pi * pi, 2.0 * pr * pi
    er, ei = hend_ref[:, :ns], hend_ref[:, ns:]
    seg = lax.broadcasted_iota(jnp.int32, er.shape, 0) % n_seg
    hr, hi = jnp.zeros_like(er), jnp.zeros_like(ei)
    for s in range(1, n_seg):
        xr = qr * hr - qi * hi + er
        xi = qr * hi + qi * hr + ei
        hr = jnp.where(seg == s, pltpu.roll(xr, 1, 0), hr)
        hi = jnp.where(seg == s, pltpu.roll(xi, 1, 0), hi)
    h0_ref[:, :ns] = hr
    h0_ref[:, ns:] = hi


def _ssm_combine(hend, w, seg_len, n_seg):
    return pl.pallas_call(
        functools.partial(_ssm_combine_kernel, seg_len=seg_len, n_seg=n_seg),
        out_shape=jax.ShapeDtypeStruct(hend.shape, F32),
        name="ssm_combine",
    )(hend, w["a_re"], w["a_im"])


def _flash_kernel(q_ref, k_ref, vt_ref, o_ref, s_buf, p_buf, acc_ref, m_ref, alpha_ref):
    tq = 2 * KV_BLOCK
    heads = range(FLASH_HEADS)

    def scores(q, t, slot):
        rows = pl.ds(pl.multiple_of(t * KV_BLOCK, KV_BLOCK), KV_BLOCK)
        for hh in heads:
            s_buf[slot, hh] = _dot_nt(k_ref[rows, hh * HEAD_BLOCK:(hh + 1) * HEAD_BLOCK], q[hh])

    def softmax(slot, hh, first_visible):
        s = s_buf[slot, hh]
        if first_visible is not None:
            ki = lax.broadcasted_iota(jnp.int32, s.shape, 0)
            qi = lax.broadcasted_iota(jnp.int32, s.shape, 1)
            s = jnp.where(ki + first_visible <= qi, s, NEG_INF)
        m = m_ref[hh]
        m_new = jnp.maximum(m, jnp.max(s, axis=0, keepdims=True))
        m_ref[hh] = m_new
        p_buf[slot, hh] = jnp.exp2(s - m_new).astype(BF16)
        alpha_ref[hh] = jnp.exp2(m - m_new)

    def values(t, slot, hh):
        acc_ref[hh] = (alpha_ref[hh] * acc_ref[hh]
                       + _dot(vt_ref[t, hh * VT_ROWS:(hh + 1) * VT_ROWS, :], p_buf[slot, hh]))

    def step(q, t, slot, first_visible=None):
        for hh in heads:
            values(jnp.maximum(t - 1, 0), 1 - slot, hh)
        scores(q, t + 1, 1 - slot)
        for hh in heads:
            softmax(slot, hh, first_visible)

    def load_q(i):
        rows = pl.ds(pl.multiple_of(i * tq, tq), tq)
        return rows, [q_ref[rows, hh * HEAD_BLOCK:(hh + 1) * HEAD_BLOCK] for hh in heads]

    n_q = q_ref.shape[0] // tq
    scores(load_q(0)[1], 0, 0)

    def query_block(i, carry):
        q_rows, q = load_q(i)
        p_buf[1] = jnp.zeros(p_buf.shape[1:], BF16)
        alpha_ref[...] = jnp.ones(alpha_ref.shape, F32)
        m_ref[...] = jnp.full(m_ref.shape, NEG_INF, F32)
        acc_ref[...] = jnp.zeros(acc_ref.shape, F32)

        def two_steps(u, c):
            step(q, 2 * u, 0)
            step(q, 2 * u + 1, 1)
            return c

        lax.fori_loop(0, i, two_steps, 0)
        step(q, 2 * i, 0, first_visible=0)
        for hh in heads:
            values(2 * i, 0, hh)
        scores(load_q(jnp.minimum(i + 1, n_q - 1))[1], 0, 0)
        for hh in heads:
            softmax(1, hh, KV_BLOCK)
        for hh in heads:
            values(2 * i + 1, 1, hh)
        out = [acc_ref[hh, :V_HEAD] / acc_ref[hh, V_HEAD:V_HEAD + 1] for hh in heads]
        o_ref[q_rows, :] = jnp.concatenate(out, axis=0).T.astype(o_ref.dtype)
        return carry

    lax.fori_loop(0, n_q, query_block, 0)


def _flash(q, k, vt, batch, seq):
    tq = 2 * KV_BLOCK
    hs = FLASH_HEADS
    return pl.pallas_call(
        _flash_kernel,
        grid=(batch, N_HEADS // hs),
        in_specs=[pl.BlockSpec((seq, hs * HEAD_BLOCK), lambda b, p: (b, p)),
                  pl.BlockSpec((seq, hs * HEAD_BLOCK), lambda b, p: (b, p)),
                  pl.BlockSpec((seq // KV_BLOCK, hs * VT_ROWS, KV_BLOCK), lambda b, p: (b, p, 0))],
        out_specs=pl.BlockSpec((seq, hs * V_HEAD), lambda b, p: (b, p)),
        out_shape=jax.ShapeDtypeStruct((batch * seq, N_HEADS * V_HEAD), BF16),
        scratch_shapes=[pltpu.VMEM((2, hs, KV_BLOCK, tq), F32), pltpu.VMEM((2, hs, KV_BLOCK, tq), BF16),
                        pltpu.VMEM((hs, VT_ROWS, tq), F32), pltpu.VMEM((hs, 1, tq), F32),
                        pltpu.VMEM((hs, 1, tq), F32)],
        compiler_params=_params("parallel", "parallel"),
        name="flash",
    )(q, k, vt)


def _absorb_kernel(q_ref, gk_ref, wabs_ref, sel_ref, qabs_ref, qr_ref):
    qg = (q_ref[...].astype(F32) * gk_ref[...]).astype(BF16)
    qabs_ref[...] = _dot(qg, wabs_ref[...]).astype(qabs_ref.dtype)
    qr_ref[...] = _dot(qg, sel_ref[...]).astype(qr_ref.dtype)


def _absorb(q, w):
    n = q.shape[0]
    return pl.pallas_call(
        _absorb_kernel,
        out_shape=(jax.ShapeDtypeStruct((n, w["w_abs"].shape[1]), BF16),
                   jax.ShapeDtypeStruct((n, w["sel_rope"].shape[1]), BF16)),
        compiler_params=_params(),
        name="absorb",
    )(q, w["g_kh"], w["w_abs"], w["sel_rope"])


def _paged_attn_kernel(pt_ref, qabs_ref, qr_ref, qabs_nx_ref, qr_nx_ref, ckvn_ref, kpetn_ref, wukt_ref,
                       ckv_hbm, kpet_hbm, o_ref, ckv_buf, kpe_buf, cb_buf, sc_buf, lhs_ref, m_ref, l_ref, acc_ref,
                       sem_c, sem_k, *, layer, n_chunks, chunk_pages, n_new):
    b = pl.program_id(0)
    total = pl.num_programs(0) * n_chunks
    n_k = wukt_ref.shape[0]
    rows_q = qabs_ref.shape[1]
    tk = chunk_pages * PAGE_SIZE
    part_pages = min(PART_PAGES, chunk_pages)
    parts = chunk_pages // part_pages
    qr = (qr_ref[0][:, :QK_ROPE], qr_nx_ref[0][:, :QK_ROPE])

    def start_chunk(g, slt):
        g = jnp.minimum(g, total - 1)
        for p in range(chunk_pages):
            page = pt_ref[g * chunk_pages + p]
            dst = pl.ds(p * PAGE_SIZE, PAGE_SIZE)
            pltpu.make_async_copy(ckv_hbm.at[layer, page], ckv_buf.at[slt, dst, :], sem_c.at[slt]).start()
            pltpu.make_async_copy(kpet_hbm.at[layer, page], kpe_buf.at[slt, p], sem_k.at[slt]).start()

    def wait_chunk(slt):
        pltpu.make_async_copy(ckv_buf.at[slt], ckv_buf.at[slt], sem_c.at[slt]).wait()
        pltpu.make_async_copy(kpe_buf.at[slt], kpe_buf.at[slt], sem_k.at[slt]).wait()

    def scaled_scores(which, cb, kpet):
        keys = cb.shape[0]
        both = _dot_nt(lhs_ref[which], cb)
        kt = both[:n_k]
        ssq = jnp.sum((kt * kt).reshape(QK_NOPE, N_HEADS, keys), axis=0)
        ssq = ssq + jnp.sum(kpet * kpet, axis=0, keepdims=True)
        r = lax.rsqrt(ssq * (1.0 / QK_HEAD) + EPS)
        sc = both[n_k:] + _dot(qr[which], kpet.astype(BF16))
        return (sc.reshape(n_new, N_HEADS, keys) * r[None]).reshape(rows_q, keys)

    def stage_a(slt, which, part):
        pages = range(part * part_pages, (part + 1) * part_pages)
        rows = slice(pages[0] * PAGE_SIZE, (pages[-1] + 1) * PAGE_SIZE)
        cb = ckv_buf[slt, rows, :].astype(BF16)
        cb_buf[slt, rows, :] = cb
        kpet = jnp.concatenate([kpe_buf[slt, p] for p in pages], axis=1)
        sc_buf[slt, :, rows] = scaled_scores(which, cb, kpet)

    def stage_b(sc, cb):
        m_old = m_ref[...]
        m_new = jnp.maximum(m_old, jnp.max(sc, axis=1, keepdims=True))
        p = jnp.exp2(sc - m_new)
        alpha = jnp.exp2(m_old - m_new)
        l_ref[...] = alpha * l_ref[...] + jnp.sum(p, axis=1, keepdims=True)
        acc_ref[...] = alpha * acc_ref[...] + _dot(p.astype(BF16), cb)
        m_ref[...] = m_new

    @pl.when(b == 0)
    def _():
        for which in range(2):
            lhs_ref[which, :n_k, :] = wukt_ref[...]

    lhs_ref[0, n_k:, :] = qabs_ref[0]
    lhs_ref[1, n_k:, :] = qabs_nx_ref[0]

    @pl.when(b == 0)
    def _():
        start_chunk(0, 0)
        start_chunk(1, 1)
        wait_chunk(0)
        for part in range(parts):
            stage_a(0, 0, part)

    m_ref[...] = jnp.full(m_ref.shape, NEG_INF, F32)
    l_ref[...] = jnp.zeros(l_ref.shape, F32)
    acc_ref[...] = jnp.zeros(acc_ref.shape, F32)

    for c in range(n_chunks):
        g = b * n_chunks + c
        slot, nxt = c % 2, (c + 1) % 2
        which = 0 if c + 1 < n_chunks else 1

        wait_chunk(nxt)
        start_chunk(g + 2, slot)
        stage_a(nxt, which, 0)
        stage_b(sc_buf[slot], cb_buf[slot])
        for part in range(1, parts):
            stage_a(nxt, which, part)

    @pl.when(b == pl.num_programs(0) - 1)
    def _():
        wait_chunk((n_chunks - 1) % 2)

    tkn = kpetn_ref.shape[2]
    cbn = ckvn_ref[0].astype(BF16)
    cbn = jnp.concatenate([cbn, jnp.zeros((tkn - cbn.shape[0], cbn.shape[1]), BF16)], axis=0)
    sc = scaled_scores(0, cbn, kpetn_ref[0])
    key = lax.broadcasted_iota(jnp.int32, (rows_q, tkn), 1)
    tok = lax.broadcasted_iota(jnp.int32, (rows_q, tkn), 0) // N_HEADS
    stage_b(jnp.where(key <= tok, sc, NEG_INF), cbn)
    o_ref[0] = (acc_ref[...] / l_ref[...]).astype(o_ref.dtype)


def _paged_attn(page_table, qabs, qr, ckv_new, kpet_new, w_ukt, cache_ckv, cache_kpet, layer, chunk_pages):
    n_seq, pages_per_seq = page_table.shape
    assert pages_per_seq % chunk_pages == 0 and chunk_pages % min(PART_PAGES, chunk_pages) == 0
    n_chunks = pages_per_seq // chunk_pages
    assert n_chunks % 2 == 0 and n_seq * n_chunks >= 2
    rows_q, kv_lora = qabs.shape[1:]
    n_new = rows_q // N_HEADS
    tk = chunk_pages * PAGE_SIZE
    seq_block = lambda shape: pl.BlockSpec((1,) + shape, lambda b, pt: (b, 0, 0))
    nxt_block = lambda shape: pl.BlockSpec((1,) + shape, lambda b, pt: (jnp.minimum(b + 1, n_seq - 1), 0, 0))
    grid_spec = pltpu.PrefetchScalarGridSpec(
        num_scalar_prefetch=1,
        grid=(n_seq,),
        in_specs=[seq_block(qabs.shape[1:]), seq_block(qr.shape[1:]),
                  nxt_block(qabs.shape[1:]), nxt_block(qr.shape[1:]),
                  seq_block(ckv_new.shape[1:]), seq_block(kpet_new.shape[1:]),
                  pl.BlockSpec(w_ukt.shape, lambda b, pt: (0, 0)),
                  pl.BlockSpec(memory_space=pl.ANY), pl.BlockSpec(memory_space=pl.ANY)],
        out_specs=seq_block((rows_q, kv_lora)),
        scratch_shapes=[pltpu.VMEM((2, tk, kv_lora), F32), pltpu.VMEM((2, chunk_pages, QK_ROPE, PAGE_SIZE), F32),
                        pltpu.VMEM((2, tk, kv_lora), BF16), pltpu.VMEM((2, rows_q, tk), F32),
                        pltpu.VMEM((2, w_ukt.shape[0] + rows_q, kv_lora), BF16),
                        pltpu.VMEM((rows_q, 1), F32), pltpu.VMEM((rows_q, 1), F32),
                        pltpu.VMEM((rows_q, kv_lora), F32),
                        pltpu.SemaphoreType.DMA((2,)), pltpu.SemaphoreType.DMA((2,))])
    return pl.pallas_call(
        functools.partial(_paged_attn_kernel, layer=layer, n_chunks=n_chunks, chunk_pages=chunk_pages, n_new=n_new),
        grid_spec=grid_spec,
        out_shape=jax.ShapeDtypeStruct((n_seq, rows_q, kv_lora), BF16),
        compiler_params=_params("arbitrary"),
        name="paged_attn",
    )(page_table.reshape(-1), qabs, qr, qabs, qr, ckv_new, kpet_new, w_ukt, cache_ckv, cache_kpet)


def _out_proj_kernel(x_ref, a_ref, b_ref, sga_ref, sgb_ref, p_ref, wuvb_ref, wa_ref, wb_ref, wo_ref, gffn_ref,
                     wg_ref, wup_ref, wdn_ref, gple_ref, wpg_ref, wple_ref, y_ref, *, latent_b):
    b = b_ref[...]
    if latent_b:
        b = _dot(b, wuvb_ref[...]).astype(BF16)
    merged = (sga_ref[...].astype(F32) * _dot(a_ref[...], wa_ref[...])
              + sgb_ref[...].astype(F32) * _dot(b, wb_ref[...]))
    x = x_ref[...] + _dot(merged.astype(BF16), wo_ref[...])

    def norm(t, g_ref):
        return (t * lax.rsqrt(jnp.mean(t * t, axis=-1, keepdims=True) + EPS) * g_ref[...]).astype(BF16)

    hf = norm(x, gffn_ref)
    d_ff = wg_ref.shape[1]
    piece = d_ff // FFN_PIECES
    for c in range(FFN_PIECES):
        cols = slice(c * piece, (c + 1) * piece)
        act = jax.nn.silu(_dot(hf, wg_ref[:, cols])) * _dot(hf, wup_ref[:, cols])
        x = x + _dot(act.astype(BF16), wdn_ref[cols, :])
    gate = jax.nn.sigmoid(_dot(norm(x, gple_ref), wpg_ref[...]))
    y_ref[...] = x + gate * _dot(p_ref[...].astype(BF16), wple_ref[...])


def _out_proj(x, a, b, sga, sgb, p, w, tm, latent_b):
    n, d = x.shape
    assert n % tm == 0
    row = lambda width: pl.BlockSpec((tm, width), lambda i: (i, 0))
    consts = [w[k] for k in ("w_uv_bd", "w_a", "w_b", "w_out", "g_ffn", "w_gate", "w_up", "w_down", "g_ple",
                             "w_ple_gate", "w_ple")]
    return pl.pallas_call(
        functools.partial(_out_proj_kernel, latent_b=latent_b),
        grid=(n // tm,),
        in_specs=[row(d), row(a.shape[1]), row(b.shape[1]), row(d), row(d), row(p.shape[1])]
        + [_const_spec(c.shape) for c in consts],
        out_specs=row(d),
        out_shape=jax.ShapeDtypeStruct((n, d), F32),
        compiler_params=_params("parallel"),
        name="out_proj_latent" if latent_b else "out_proj",
    )(x, a, b, sga, sgb, p, *consts)


def _head_blocked(w_nope, w_rope):
    k = w_nope.shape[0] if w_nope is not None else w_rope.shape[0]
    nope = w_nope if w_nope is not None else jnp.zeros((k, N_HEADS, QK_NOPE), F32)
    rope = w_rope if w_rope is not None else jnp.zeros((k, N_HEADS, QK_ROPE), F32)
    pad = jnp.zeros((k, N_HEADS, HEAD_BLOCK - QK_HEAD), F32)
    return jnp.concatenate([nope, rope, pad], axis=-1).reshape(k, N_HEADS * HEAD_BLOCK)


def _rot_cols(w):
    half = QK_ROPE // 2
    return jnp.concatenate([-w[..., half:], w[..., :half]], axis=-1)


def _head_gain(g_nope, g_rope_half):
    g = jnp.concatenate([g_nope, g_rope_half, g_rope_half, jnp.zeros((HEAD_BLOCK - QK_HEAD,), F32)])
    return jnp.tile(g, N_HEADS)[None, :]


def _block_diag(blocks):
    g, r, c = blocks.shape
    eye = jnp.eye(g, dtype=blocks.dtype)
    return jnp.einsum("grc,gh->grhc", blocks, eye).reshape(g * r, g * c)


def _prepare(norm_attn_g, w_in, q_norm_g, w_uq, kv_norm_g, w_uk, w_uv, q_norm_nope_g, q_norm_rope_g,
             k_norm_nope_g, k_norm_rope_g, disc, ssm_c_re, ssm_c_im, ssm_d, w_glu, w_branch_a, w_branch_b,
             w_out, norm_ffn_g, w_gate, w_up, w_down, norm_ple_g, w_ple_gate, w_ple):
    d_model = w_in.shape[0]
    ssm_w = ssm_d.shape[0]
    q_lora, kv_lora = w_uq.shape[0], w_uk.shape[0]
    o = np.cumsum([0, ssm_w, q_lora, kv_lora, QK_ROPE, d_model, d_model])
    bf = lambda a: a.astype(BF16)
    w = {}
    w["g_attn"] = norm_attn_g[None, :]
    w["w_u"] = bf(w_in[:, o[0]:o[1]])
    w["w_q"] = bf(w_in[:, o[1]:o[2]])
    w["w_kv"] = bf(w_in[:, o[2]:o[3]])
    w_kr = w_in[:, o[3]:o[4]]
    place = lambda r: jnp.pad(r, ((0, 0), (QK_NOPE, HEAD_BLOCK - QK_HEAD)))
    w["w_kr"] = bf(jnp.concatenate([place(w_kr), place(_rot_cols(w_kr))], axis=1))
    w["w_ga"] = bf(w_in[:, o[4]:o[5]])
    w["w_gb"] = bf(w_in[:, o[5]:o[6]])
    w["g_q"] = q_norm_g[None, :]
    uq_nope, uq_rope = w_uq[..., :QK_NOPE], w_uq[..., QK_NOPE:]
    w["w_uq"] = bf(jnp.concatenate([_head_blocked(uq_nope, uq_rope), _head_blocked(None, _rot_cols(uq_rope))], axis=1))
    w["g_kv"] = kv_norm_g[None, :]
    w["w_uk"] = bf(_head_blocked(w_uk, None))
    uv_t = jnp.pad(jnp.transpose(w_uv, (1, 2, 0)), ((0, 0), (0, VT_ROWS - V_HEAD), (0, 0)))
    w["w_uv_t"] = bf(uv_t.reshape(N_HEADS * VT_ROWS, kv_lora))
    w["g_qh"] = _head_gain(q_norm_nope_g, q_norm_rope_g)
    w["g_kh"] = _head_gain(k_norm_nope_g, k_norm_rope_g)
    ukt = jnp.transpose(w_uk, (1, 2, 0))
    ukt = jnp.pad(ukt, ((0, 0), (0, HEAD_BLOCK - QK_NOPE), (0, 0)))
    w["w_abs"] = bf(_block_diag(ukt))
    sel = np.zeros((N_HEADS * HEAD_BLOCK, N_HEADS * HEAD_BLOCK), np.float32)
    for h in range(N_HEADS):
        for dd in range(QK_ROPE):
            sel[h * HEAD_BLOCK + QK_NOPE + dd, h * HEAD_BLOCK + dd] = 1.0
    w["sel_rope"] = jnp.asarray(sel, BF16)
    w["w_ukt"] = bf(jnp.transpose(w_uk, (2, 1, 0)).reshape(QK_NOPE * N_HEADS, kv_lora))
    w["w_uv_bd"] = bf(_block_diag(jnp.transpose(w_uv, (1, 0, 2))))
    lb_re, lb_im, bb_re, bb_im = disc
    w["a_re"] = lb_re.reshape(1, -1)
    w["a_im"] = lb_im.reshape(1, -1)
    tr = lambda a: jnp.transpose(a, (0, 2, 1))
    gp = bb_re.shape[0] // SSM_SPLIT
    part = lambda a, p: _block_diag(tr(a[p * gp:(p + 1) * gp]))
    w["w_bu"] = bf(jnp.stack([jnp.concatenate([part(bb_re, p), part(bb_im, p)], axis=1) for p in range(SSM_SPLIT)]))
    w["w_y"] = bf(jnp.stack([jnp.concatenate([part(ssm_c_re, p), -part(ssm_c_im, p)], axis=0)
                             for p in range(SSM_SPLIT)]))
    w["ssm_d"] = ssm_d[None, :]
    w["w_glu"] = bf(w_glu)
    w["w_a"], w["w_b"], w["w_out"] = bf(w_branch_a), bf(w_branch_b), bf(w_out)
    w["g_ffn"], w["g_ple"] = norm_ffn_g[None, :], norm_ple_g[None, :]
    w["w_gate"], w["w_up"], w["w_down"] = bf(w_gate), bf(w_up), bf(w_down)
    w["w_ple_gate"], w["w_ple"] = bf(w_ple_gate), bf(w_ple)
    return w


def _rope_tables(pos):
    half = QK_ROPE // 2
    inv = ROPE_THETA ** (-jnp.arange(half, dtype=F32) / half)
    ang = pos.astype(F32)[:, None] * inv[None, :]
    cos, sin = jnp.cos(ang), jnp.sin(ang)
    n = pos.shape[0]
    ones, zeros = jnp.ones((n, QK_NOPE), F32), jnp.zeros((n, QK_NOPE), F32)
    pad = jnp.zeros((n, HEAD_BLOCK - QK_HEAD), F32)
    return (jnp.concatenate([ones, cos, cos, pad], axis=1), jnp.concatenate([zeros, sin, sin, pad], axis=1))


def _pick(n, prefs):
    for t in prefs:
        if n % t == 0:
            return t
    return n


def kernel(x_prompt, x_sample, p_prompt, p_sample, cache_ckv, cache_kpe, state_ssm_re, state_ssm_im, page_table, norm_attn_g, w_in, q_norm_g, w_uq, kv_norm_g, w_uk, w_uv, q_norm_nope_g, q_norm_rope_g, k_norm_nope_g, k_norm_rope_g, ssm_lam_re, ssm_lam_im, ssm_log_dt, ssm_b_re, ssm_b_im, ssm_c_re, ssm_c_im, ssm_d, w_glu, w_branch_a, w_branch_b, w_out, norm_ffn_g, w_gate, w_up, w_down, norm_ple_g, w_ple_gate, w_ple):
    depth = w_in.shape[0]
    batch, seq, d_model = x_prompt.shape
    n_seq, n_new, _ = x_sample.shape
    past = page_table.shape[1] * PAGE_SIZE
    groups, n_state = ssm_lam_re.shape[1:]
    ns = groups * n_state

    cos_p, sin_p = _rope_tables(jnp.arange(seq))
    cos_p, sin_p = jnp.tile(cos_p, (batch, 1)), jnp.tile(sin_p, (batch, 1))
    cos_s, sin_s = _rope_tables(past + jnp.arange(n_new))
    cos_s, sin_s = jnp.tile(cos_s, (n_seq, 1)), jnp.tile(sin_s, (n_seq, 1))

    n_seg = SUBLANES
    assert seq % n_seg == 0
    seg_len = seq // n_seg
    chains_p = batch * n_seg
    steps_p = _pick(seg_len, (16, 8, 4, 2))
    steps_s = _pick(n_new, (4, 2))
    tm_p = _pick(batch * seq, (512, 256, 128))
    tm_in = _pick(batch * seq, (1024, 512, 256))
    tm_s = _pick(n_seq * n_new, (512, 256, 128))
    assert seq % (2 * KV_BLOCK) == 0
    chunk_pages = _pick(page_table.shape[1] // 2, (32, 16, 8))

    xp = x_prompt.reshape(batch * seq, d_model)
    xs = x_sample.reshape(n_seq * n_new, d_model)
    outs = {k: [] for k in ("ckv_p", "kpe_p", "hre_p", "him_p", "ckv_s", "kpe_s", "hre_s", "him_s")}
    for i in range(depth):
        disc = _discretize(ssm_lam_re[i], ssm_lam_im[i], ssm_log_dt[i], ssm_b_re[i], ssm_b_im[i])
        w = _prepare(norm_attn_g[i], w_in[i], q_norm_g[i], w_uq[i], kv_norm_g[i], w_uk[i], w_uv[i],
                     q_norm_nope_g[i], q_norm_rope_g[i], k_norm_nope_g[i], k_norm_rope_g[i], disc,
                     ssm_c_re[i], ssm_c_im[i], ssm_d[i], w_glu[i], w_branch_a[i], w_branch_b[i], w_out[i],
                     norm_ffn_g[i], w_gate[i], w_up[i], w_down[i], norm_ple_g[i], w_ple_gate[i], w_ple[i])
        ssm_w = w["w_u"].shape[1]

        u, q, ckv, kpe, sga, sgb, k, vt = _in_proj(xp, cos_p, sin_p, w, tm_in, with_kv=True)
        u_rows = u.reshape(batch, n_seg, seg_len, ssm_w).transpose(2, 0, 1, 3).reshape(batch * seq, ssm_w)
        zero_state = jnp.zeros((chains_p, 2 * ns), F32)
        h_end = _ssm(u_rows, zero_state, w, chains_p, steps_p, emit_y=False)
        h_start = _ssm_combine(_state_from_parts(h_end), w, seg_len, n_seg)
        a_rows, h_fin = _ssm(u_rows, _state_to_parts(h_start), w, chains_p, steps_p, emit_y=True)
        h_fin = _state_from_parts(h_fin)
        a_out = a_rows.reshape(seg_len, batch, n_seg, ssm_w).transpose(1, 2, 0, 3).reshape(batch * seq, ssm_w)
        b_out = _flash(q, k, vt, batch, seq)
        xp = _out_proj(xp, a_out, b_out, sga, sgb, p_prompt[i].reshape(batch * seq, -1), w, tm_p, latent_b=False)
        h_fin = h_fin.reshape(batch, n_seg, 2 * ns)[:, -1]
        outs["ckv_p"].append(ckv.reshape(batch, seq, -1))
        outs["kpe_p"].append(kpe.reshape(batch, seq, -1))
        outs["hre_p"].append(h_fin[:, :ns].reshape(batch, groups, n_state))
        outs["him_p"].append(h_fin[:, ns:].reshape(batch, groups, n_state))

        u, q, ckv, kpe, sga, sgb = _in_proj(xs, cos_s, sin_s, w, tm_s, with_kv=False)
        u_rows = u.reshape(n_seq, n_new, ssm_w).transpose(1, 0, 2).reshape(n_seq * n_new, ssm_w)
        h0 = jnp.concatenate([state_ssm_re[i].reshape(n_seq, ns), state_ssm_im[i].reshape(n_seq, ns)], axis=1)
        a_rows, h_fin = _ssm(u_rows, _state_to_parts(h0), w, n_seq, steps_s, emit_y=True)
        h_fin = _state_from_parts(h_fin)
        a_out = a_rows.reshape(n_new, n_seq, ssm_w).transpose(1, 0, 2).reshape(n_seq * n_new, ssm_w)
        qabs, qr = _absorb(q, w)
        rows_q = n_new * N_HEADS
        kv_lora = ckv.shape[1]
        pad_new = PAGE_SIZE - n_new
        ckv_new = ckv.reshape(n_seq, n_new, kv_lora)
        kpet_new = jnp.pad(kpe.reshape(n_seq, n_new, QK_ROPE).transpose(0, 2, 1), ((0, 0), (0, 0), (0, pad_new)))
        o_lat = _paged_attn(page_table, qabs.reshape(n_seq, rows_q, kv_lora), qr.reshape(n_seq, rows_q, HEAD_BLOCK),
                            ckv_new, kpet_new, w["w_ukt"], cache_ckv, jnp.swapaxes(cache_kpe, 2, 3), i, chunk_pages)
        b_lat = o_lat.reshape(n_seq * n_new, N_HEADS * kv_lora)
        xs = _out_proj(xs, a_out, b_lat, sga, sgb, p_sample[i].reshape(n_seq * n_new, -1), w, tm_s, latent_b=True)
        outs["ckv_s"].append(ckv.reshape(n_seq, n_new, -1))
        outs["kpe_s"].append(kpe.reshape(n_seq, n_new, -1))
        outs["hre_s"].append(h_fin[:, :ns].reshape(n_seq, groups, n_state))
        outs["him_s"].append(h_fin[:, ns:].reshape(n_seq, groups, n_state))

    st = lambda key: jnp.stack(outs[key])
    return (xp.reshape(batch, seq, d_model), xs.reshape(n_seq, n_new, d_model),
            st("ckv_p"), st("kpe_p"), st("hre_p"), st("him_p"),
            st("ckv_s"), st("kpe_s"), st("hre_s"), st("him_s"))
```
